```python
import jax, jax.numpy as jnp
from jax import lax
import numpy as np

D_MODEL = 2048
BATCH = 2
SEQ = 4096
DEPTH = 4
DEC_BATCH = 32
DEC_SEQ = 1
PAST_LEN = 16384
PAGE_SIZE = 128

N_MIXERS = 2
N_META = 16
POOL_WINDOWS = (2, 4, 8, 16)
N_POOL_GROUPS = len(POOL_WINDOWS)
POOL_GROUP = D_MODEL // N_POOL_GROUPS
POOL_STATE = max(POOL_WINDOWS) - 1
HEAD_DIM = 64
N_HEADS = D_MODEL // HEAD_DIM
N_KV_HEADS = 4
GROUP = N_HEADS // N_KV_HEADS
WINDOW = 128
BLOCK = 128
ROT_DIM = HEAD_DIM // 4
ROPE_THETA = 500000.0
D_FF = -(-8 * D_MODEL // (3 * 256)) * 256
N_POOL_LAYERS = (DEPTH + N_MIXERS - 1) // N_MIXERS
N_SWA_LAYERS = DEPTH // N_MIXERS
EPS = 1e-6
NEG = -1e30

kernel_name = "pool_swa_sink_hybrid_step"


def rms_norm(x, g):
    xf = x.astype(jnp.float32)
    y = xf * lax.rsqrt(jnp.mean(xf * xf, axis=-1, keepdims=True) + EPS)
    return (y * g.astype(jnp.float32)).astype(x.dtype)


def rope(x, pos):
    half = ROT_DIM // 2
    inv = jnp.float32(ROPE_THETA) ** (-jnp.arange(half, dtype=jnp.float32) * 2.0 / ROT_DIM)
    ang = pos.astype(jnp.float32)[:, None] * inv[None, :]
    cos = jnp.cos(ang)[:, None, :]
    sin = jnp.sin(ang)[:, None, :]
    xr = x[..., :ROT_DIM].astype(jnp.float32)
    x1, x2 = xr[..., :half], xr[..., half:]
    rot = jnp.concatenate([x1 * cos - x2 * sin, x2 * cos + x1 * sin], axis=-1).astype(x.dtype)
    return jnp.concatenate([rot, x[..., ROT_DIM:]], axis=-1)


def pool_mixer(h, prev, w_grp, scale):
    B, T, _ = h.shape
    P = prev.shape[1]
    ext = jnp.concatenate([prev.astype(h.dtype), h], axis=1)
    cs = jnp.cumsum(ext.astype(jnp.float32), axis=1)
    cs = jnp.concatenate([jnp.zeros((B, 1, D_MODEL), jnp.float32), cs], axis=1)
    end = P + 1 + jnp.arange(T)
    outs = []
    for g, w in enumerate(POOL_WINDOWS):
        start = jnp.maximum(end - w, 0)
        c0, c1 = g * POOL_GROUP, (g + 1) * POOL_GROUP
        s = cs[:, end, c0:c1] - cs[:, start, c0:c1]
        cnt = (end - start).astype(jnp.float32)
        outs.append(s / cnt[None, :, None])
    pooled = jnp.stack(outs, axis=2)
    diff = (pooled - h.reshape(B, T, N_POOL_GROUPS, POOL_GROUP).astype(jnp.float32)).astype(h.dtype)
    y = jnp.einsum('btgc,gcd->btgd', diff, w_grp).reshape(B, T, D_MODEL)
    return y * scale, ext[:, -POOL_STATE:]


def qkv_project(h, w_qkv, q_g, k_g, pos):
    B, L, _ = h.shape
    qkv = h @ w_qkv
    q, k, v = jnp.split(qkv, [N_HEADS * HEAD_DIM, (N_HEADS + N_KV_HEADS) * HEAD_DIM], axis=-1)
    q = q.reshape(B, L, N_HEADS, HEAD_DIM)
    k = k.reshape(B, L, N_KV_HEADS, HEAD_DIM)
    v = v.reshape(B, L, N_KV_HEADS, HEAD_DIM)
    q = rope(rms_norm(q, q_g), pos)
    k = rope(rms_norm(k, k_g), pos)
    return q, k, v


def sink_attention(q, k, v, mask, sinks):
    s = jnp.einsum('...qhgd,...khd->...hgqk', q, k).astype(jnp.float32) * (HEAD_DIM ** -0.5)
    s = jnp.where(mask, s, NEG)
    sk = sinks.astype(jnp.float32).reshape(N_KV_HEADS, GROUP, 1, 1)
    m = jnp.maximum(jnp.max(s, axis=-1, keepdims=True), sk)
    p = jnp.exp(s - m)
    p = p / (jnp.sum(p, axis=-1, keepdims=True) + jnp.exp(sk - m))
    return jnp.einsum('...hgqk,...khd->...qhgd', p.astype(v.dtype), v)


def swa_prompt(h, w_qkv, w_o, q_g, k_g, sinks):
    B, L, _ = h.shape
    q, k, v = qkv_project(h, w_qkv, q_g, k_g, jnp.arange(L))
    pad = (-L) % BLOCK
    nb = (L + pad) // BLOCK
    qb = jnp.pad(q, ((0, 0), (pad, 0), (0, 0), (0, 0))).reshape(B, nb, BLOCK, N_KV_HEADS, GROUP, HEAD_DIM)

    def band(t):
        tb = jnp.pad(t, ((0, 0), (pad + BLOCK, 0), (0, 0), (0, 0))).reshape(B, nb + 1, BLOCK, N_KV_HEADS, HEAD_DIM)
        return jnp.concatenate([tb[:, :-1], tb[:, 1:]], axis=2)

    kb, vb = band(k), band(v)
    qpos = jnp.arange(nb)[:, None] * BLOCK + jnp.arange(BLOCK)[None, :] - pad
    kpos = jnp.arange(nb)[:, None] * BLOCK + jnp.arange(2 * BLOCK)[None, :] - BLOCK - pad
    d = qpos[:, :, None] - kpos[:, None, :]
    mask = ((kpos[:, None, :] >= 0) & (d >= 0) & (d < WINDOW))[:, None, None]
    o = sink_attention(qb, kb, vb, mask, sinks).reshape(B, nb * BLOCK, N_HEADS * HEAD_DIM)[:, pad:]
    return o @ w_o, k[:, -WINDOW:], v[:, -WINDOW:]


def swa_sample(h, ck, cv, w_qkv, w_o, q_g, k_g, sinks):
    B, T, _ = h.shape
    pos = PAST_LEN + jnp.arange(T)
    q, k, v = qkv_project(h, w_qkv, q_g, k_g, pos)
    ka = jnp.concatenate([ck.astype(k.dtype), k], axis=1)
    va = jnp.concatenate([cv.astype(v.dtype), v], axis=1)
    kpos = PAST_LEN - WINDOW + jnp.arange(WINDOW + T)
    d = pos[:, None] - kpos[None, :]
    mask = ((kpos[None, :] >= 0) & (d >= 0) & (d < WINDOW))[None, None]
    o = sink_attention(q.reshape(B, T, N_KV_HEADS, GROUP, HEAD_DIM), ka, va, mask, sinks)
    o = o.reshape(B, T, N_HEADS * HEAD_DIM)
    return o @ w_o, ka[:, -WINDOW:], va[:, -WINDOW:]


def swiglu(h, wg, wu, wd):
    return (jax.nn.silu(h @ wg) * (h @ wu)) @ wd


def setup_inputs(seed: int = 0) -> dict:
    key = jax.random.key(seed)
    ks = jax.random.split(key, 20)
    f32 = jnp.float32
    nrm = lambda k, s: jax.random.normal(k, s, f32)
    QKV = (N_HEADS + 2 * N_KV_HEADS) * HEAD_DIM
    return {
        "x_prompt": nrm(ks[0], (BATCH, SEQ, D_MODEL)),
        "x_sample": nrm(ks[1], (DEC_BATCH, DEC_SEQ, D_MODEL)),
        "state_pool": nrm(ks[2], (N_POOL_LAYERS, DEC_BATCH, POOL_STATE, D_MODEL)),
        "cache_k": nrm(ks[3], (N_SWA_LAYERS, DEC_BATCH, WINDOW, N_KV_HEADS, HEAD_DIM)),
        "cache_v": nrm(ks[4], (N_SWA_LAYERS, DEC_BATCH, WINDOW, N_KV_HEADS, HEAD_DIM)),
        "meta_tokens": nrm(ks[5], (N_META, D_MODEL)),
        "norm_mix": 1.0 + 0.05 * nrm(ks[6], (DEPTH, D_MODEL)),
        "norm_ffn": 1.0 + 0.05 * nrm(ks[7], (DEPTH, D_MODEL)),
        "pool_w": nrm(ks[8], (N_POOL_LAYERS, N_POOL_GROUPS, POOL_GROUP, POOL_GROUP)) * POOL_GROUP ** -0.5,
        "pool_scale": 1.0 + 0.1 * nrm(ks[9], (N_POOL_LAYERS, D_MODEL)),
        "w_qkv": nrm(ks[10], (N_SWA_LAYERS, D_MODEL, QKV)) * D_MODEL ** -0.5,
        "w_o": nrm(ks[11], (N_SWA_LAYERS, N_HEADS * HEAD_DIM, D_MODEL)) * (N_HEADS * HEAD_DIM) ** -0.5,
        "q_norm": 1.0 + 0.05 * nrm(ks[12], (N_SWA_LAYERS, HEAD_DIM)),
        "k_norm": 1.0 + 0.05 * nrm(ks[13], (N_SWA_LAYERS, HEAD_DIM)),
        "sinks": 0.5 * nrm(ks[14], (N_SWA_LAYERS, N_HEADS)),
        "w_gate": nrm(ks[15], (DEPTH, D_MODEL, D_FF)) * D_MODEL ** -0.5,
        "w_up": nrm(ks[16], (DEPTH, D_MODEL, D_FF)) * D_MODEL ** -0.5,
        "w_down": nrm(ks[17], (DEPTH, D_FF, D_MODEL)) * D_FF ** -0.5,
    }


def reference(x_prompt, x_sample, state_pool, cache_k, cache_v, meta_tokens, norm_mix, norm_ffn,
              pool_w, pool_scale, w_qkv, w_o, q_norm, k_norm, sinks, w_gate, w_up, w_down):
    B = x_prompt.shape[0]
    meta = jnp.broadcast_to(meta_tokens[None].astype(x_prompt.dtype), (B, N_META, D_MODEL))
    xp = jnp.concatenate([meta, x_prompt], axis=1)
    xs = x_sample
    pool_p, pool_s, kp_l, vp_l, ks_l, vs_l = [], [], [], [], [], []
    for i in range(DEPTH):
        j = i // N_MIXERS
        hp = rms_norm(xp, norm_mix[i])
        hs = rms_norm(xs, norm_mix[i])
        if i % N_MIXERS == 0:
            yp, sp = pool_mixer(hp, hp[:, :0], pool_w[j], pool_scale[j])
            ys, ss = pool_mixer(hs, state_pool[j], pool_w[j], pool_scale[j])
            pool_p.append(sp)
            pool_s.append(ss)
        else:
            yp, kp, vp = swa_prompt(hp, w_qkv[j], w_o[j], q_norm[j], k_norm[j], sinks[j])
            ys, kn, vn = swa_sample(hs, cache_k[j], cache_v[j], w_qkv[j], w_o[j], q_norm[j], k_norm[j], sinks[j])
            kp_l.append(kp)
            vp_l.append(vp)
            ks_l.append(kn)
            vs_l.append(vn)
        xp = xp + yp
        xs = xs + ys
        xp = xp + swiglu(rms_norm(xp, norm_ffn[i]), w_gate[i], w_up[i], w_down[i])
        xs = xs + swiglu(rms_norm(xs, norm_ffn[i]), w_gate[i], w_up[i], w_down[i])
    y_prompt = xp[:, N_META:]
    return (y_prompt, xs, jnp.stack(pool_p), jnp.stack(kp_l), jnp.stack(vp_l),
            jnp.stack(pool_s), jnp.stack(ks_l), jnp.stack(vs_l))
```

```python
import functools

import jax
import jax.numpy as jnp
from jax import lax
from jax.experimental import pallas as pl
from jax.experimental.pallas import tpu as pltpu

F32 = jnp.float32
BF16 = jnp.bfloat16

D_MODEL = 2048
BATCH = 2
SEQ = 4096
DEPTH = 4
DEC_BATCH = 32
PAST_LEN = 16384
N_META = 16
POOL_WINDOWS = (2, 4, 8, 16)
N_POOL_GROUPS = len(POOL_WINDOWS)
POOL_GROUP = D_MODEL // N_POOL_GROUPS
POOL_STATE = max(POOL_WINDOWS) - 1
HEAD_DIM = 64
N_HEADS = D_MODEL // HEAD_DIM
N_KV_HEADS = 4
GROUP = N_HEADS // N_KV_HEADS
WINDOW = 128
ROT_DIM = HEAD_DIM // 4
ROPE_THETA = 500000.0
D_FF = 5632
EPS = 1e-6
NEG = -1e30

N_MAIN = BATCH * SEQ
N_SMALL = BATCH * N_META + DEC_BATCH
KV_DIM = N_KV_HEADS * HEAD_DIM
QK_DIM = D_MODEL + KV_DIM
QKV_DIM = D_MODEL + 2 * KV_DIM
LANES = 128
KPAD = WINDOW

VMEM_LIMIT = 60 * 1024 * 1024


def _rms(x, g):
    ms = jnp.mean(x * x, axis=-1, keepdims=True)
    return x * lax.rsqrt(ms + EPS) * g


def _dot(a, b):
    return jnp.dot(a, b, preferred_element_type=F32)


def _dot_nt(a, b):
    return lax.dot_general(a, b, (((1,), (1,)), ((), ())), preferred_element_type=F32)


def _split_bf16(x):
    hi = x.astype(BF16)
    lo = (x - hi.astype(F32)).astype(BF16)
    return hi, lo


def _ffn_body(x_ref, g_ref, wg_ref, wu_ref, wd_ref, o_ref, h_ref):
    @pl.when(pl.program_id(1) == 0)
    def _():
        x = x_ref[...]
        h_ref[...] = _rms(x, g_ref[...]).astype(BF16)
        o_ref[...] = x

    h = h_ref[...]
    gate = _dot(h, wg_ref[...].astype(BF16))
    up = _dot(h, wu_ref[...].astype(BF16))
    act = (gate * jax.nn.sigmoid(gate) * up).astype(BF16)
    o_ref[...] += _dot(act, wd_ref[...].astype(BF16))


def _ffn(x, g, wg, wu, wd, *, tm, tf):
    m = x.shape[0]
    single = pl.Buffered(1)
    return pl.pallas_call(
        _ffn_body,
        grid=(m // tm, D_FF // tf),
        in_specs=[
            pl.BlockSpec((tm, D_MODEL), lambda i, j: (i, 0), pipeline_mode=single),
            pl.BlockSpec((1, D_MODEL), lambda i, j: (0, 0)),
            pl.BlockSpec((D_MODEL, tf), lambda i, j: (0, j)),
            pl.BlockSpec((D_MODEL, tf), lambda i, j: (0, j)),
            pl.BlockSpec((tf, D_MODEL), lambda i, j: (j, 0)),
        ],
        out_specs=pl.BlockSpec((tm, D_MODEL), lambda i, j: (i, 0), pipeline_mode=single),
        out_shape=jax.ShapeDtypeStruct((m, D_MODEL), F32),
        scratch_shapes=[pltpu.VMEM((tm, D_MODEL), BF16)],
        compiler_params=pltpu.CompilerParams(
            dimension_semantics=("arbitrary", "arbitrary"), vmem_limit_bytes=VMEM_LIMIT),
        name="ffn",
    )(x, g, wg, wu, wd)


def _pool_project(diffs, w_ref, sc_ref):
    ys = [_dot(d.astype(BF16), w_ref[g]) for g, d in enumerate(diffs)]
    return jnp.concatenate(ys, axis=1) * sc_ref[...]


def _pool_main_body(x_ref, meta_ref, g_ref, w_ref, sc_ref, o_ref, hl_ref, ext_ref, *, tp):
    t = pl.program_id(1)

    @pl.when(t == 0)
    def _():
        ext_ref[0:N_META, :] = _rms(meta_ref[...], g_ref[...])

    x = x_ref[...]
    h = _rms(x, g_ref[...])
    ext_ref[N_META:N_META + tp, :] = h
    diffs = []
    for g, w in enumerate(POOL_WINDOWS):
        c0, c1 = g * POOL_GROUP, (g + 1) * POOL_GROUP
        hg = h[:, c0:c1]
        s = hg
        for k in range(1, w):
            s = s + ext_ref[pl.ds(N_META - k, tp), c0:c1]
        diffs.append(s * (1.0 / w) - hg)
    o_ref[...] = x + _pool_project(diffs, w_ref, sc_ref)
    tail = ext_ref[tp:tp + N_META, :]
    ext_ref[0:N_META, :] = tail

    @pl.when(t == pl.num_programs(1) - 1)
    def _():
        hl_ref[0] = tail


def _pool_main(xm, xs, g, w, sc, *, tp):
    nt = SEQ // tp
    return pl.pallas_call(
        functools.partial(_pool_main_body, tp=tp),
        grid=(BATCH, nt),
        in_specs=[
            pl.BlockSpec((tp, D_MODEL), lambda b, t: (b * nt + t, 0)),
            pl.BlockSpec((N_META, D_MODEL), lambda b, t: (b, 0)),
            pl.BlockSpec((1, D_MODEL), lambda b, t: (0, 0)),
            pl.BlockSpec((N_POOL_GROUPS, POOL_GROUP, POOL_GROUP), lambda b, t: (0, 0, 0)),
            pl.BlockSpec((1, D_MODEL), lambda b, t: (0, 0)),
        ],
        out_specs=[
            pl.BlockSpec((tp, D_MODEL), lambda b, t: (b * nt + t, 0)),
            pl.BlockSpec((1, N_META, D_MODEL), lambda b, t: (b, 0, 0)),
        ],
        out_shape=[
            jax.ShapeDtypeStruct((N_MAIN, D_MODEL), F32),
            jax.ShapeDtypeStruct((BATCH, N_META, D_MODEL), F32),
        ],
        scratch_shapes=[pltpu.VMEM((tp + N_META, D_MODEL), F32)],
        compiler_params=pltpu.CompilerParams(
            dimension_semantics=("arbitrary", "arbitrary"), vmem_limit_bytes=VMEM_LIMIT),
        name="pool_main",
    )(xm, xs, g, w, sc)


def _pool_small_body(x_ref, st_ref, g_ref, w_ref, sc_ref, o_ref, ns_ref, ext_ref):
    nm = BATCH * N_META
    x = x_ref[...]
    h = _rms(x, g_ref[...])
    for b in range(BATCH):
        ext_ref[b, 0:N_META, :] = jnp.zeros((N_META, D_MODEL), F32)
        ext_ref[b, N_META:2 * N_META, :] = h[b * N_META:(b + 1) * N_META]
    hs = h[nm:]
    row = lax.broadcasted_iota(jnp.int32, (N_META, 1), 0)
    diffs = []
    for g, w in enumerate(POOL_WINDOWS):
        c0, c1 = g * POOL_GROUP, (g + 1) * POOL_GROUP
        cnt = jnp.minimum(row + 1, w).astype(F32)
        parts = []
        for b in range(BATCH):
            s = ext_ref[b, N_META:2 * N_META, c0:c1]
            for k in range(1, w):
                s = s + ext_ref[b, N_META - k:2 * N_META - k, c0:c1]
            parts.append(s / cnt)
        s = hs[:, c0:c1]
        for k in range(1, w):
            s = s + st_ref[POOL_STATE - k, :, c0:c1]
        parts.append(s / float(w))
        diffs.append(jnp.concatenate(parts, axis=0) - h[:, c0:c1])
    o_ref[...] = x + _pool_project(diffs, w_ref, sc_ref)
    ns_ref[0:POOL_STATE - 1] = st_ref[1:POOL_STATE]
    ns_ref[POOL_STATE - 1] = hs


def _pool_small(xs, st, g, w, sc):
    return pl.pallas_call(
        _pool_small_body,
        out_shape=[
            jax.ShapeDtypeStruct((N_SMALL, D_MODEL), F32),
            jax.ShapeDtypeStruct((POOL_STATE, DEC_BATCH, D_MODEL), F32),
        ],
        scratch_shapes=[pltpu.VMEM((BATCH, 2 * N_META, D_MODEL), F32)],
        compiler_params=pltpu.CompilerParams(vmem_limit_bytes=VMEM_LIMIT),
        name="pool_small",
    )(xs, st, g, w, sc)


def _qkv_body(x_ref, g_ref, w_ref, gain_ref, cos_ref, sa_ref, sb_ref, seg_ref, segt_ref,
              q_ref, k_ref, v_ref):
    h = _rms(x_ref[...], g_ref[...]).astype(BF16)
    qkv = _dot(h, w_ref[...])
    qk = qkv[:, :QK_DIM]
    v_ref[...] = qkv[:, QK_DIM:]
    sq_hi, sq_lo = _split_bf16(qk * qk)
    ss = _dot(sq_hi, seg_ref[...]) + _dot(sq_lo, seg_ref[...])
    inv = lax.rsqrt(ss * (1.0 / HEAD_DIM) + EPS)
    inv_hi, inv_lo = _split_bf16(inv)
    scale = _dot(inv_hi, segt_ref[...]) + _dot(inv_lo, segt_ref[...])
    y = qk * scale * gain_ref[...]
    reps = QK_DIM // LANES
    cos = jnp.tile(cos_ref[...], (1, reps))
    sa = jnp.tile(sa_ref[...], (1, reps))
    sb = jnp.tile(sb_ref[...], (1, reps))
    half = ROT_DIM // 2
    y = y * cos + pltpu.roll(y, QK_DIM - half, 1) * sa + pltpu.roll(y, half, 1) * sb
    q_ref[...] = (y[:, :D_MODEL] * HEAD_DIM ** -0.5).astype(BF16)
    k_ref[...] = y[:, D_MODEL:]


def _qkv(x, g, w, gain, tabs, seg, segt, *, tm):
    m = x.shape[0]
    nt = tabs[0].shape[0] // tm
    row = lambda i: (i, 0)
    tab = lambda i: (i % nt, 0)
    fixed = lambda i: (0, 0)
    return pl.pallas_call(
        _qkv_body,
        grid=(m // tm,),
        in_specs=[
            pl.BlockSpec((tm, D_MODEL), row),
            pl.BlockSpec((1, D_MODEL), fixed),
            pl.BlockSpec((D_MODEL, QKV_DIM), fixed),
            pl.BlockSpec((1, QK_DIM), fixed),
            pl.BlockSpec((tm, LANES), tab),
            pl.BlockSpec((tm, LANES), tab),
            pl.BlockSpec((tm, LANES), tab),
            pl.BlockSpec((QK_DIM, LANES), fixed),
            pl.BlockSpec((LANES, QK_DIM), fixed),
        ],
        out_specs=[
            pl.BlockSpec((tm, D_MODEL), row),
            pl.BlockSpec((tm, KV_DIM), row),
            pl.BlockSpec((tm, KV_DIM), row),
        ],
        out_shape=[
            jax.ShapeDtypeStruct((m, D_MODEL), BF16),
            jax.ShapeDtypeStruct((m, KV_DIM), F32),
            jax.ShapeDtypeStruct((m, KV_DIM), F32),
        ],
        compiler_params=pltpu.CompilerParams(
            dimension_semantics=("arbitrary",), vmem_limit_bytes=VMEM_LIMIT),
        name="qkv",
    )(x, g, w, gain, *tabs, seg, segt)


def _softmax_av(s, sink, v):
    mx = jnp.maximum(jnp.max(s, axis=-1, keepdims=True), sink)
    p = jnp.exp(s - mx)
    den = jnp.sum(p, axis=-1, keepdims=True) + jnp.exp(sink - mx)
    return _dot(p.astype(BF16), v) / den


def _attn_main_body(sink_ref, q_ref, k_ref, v_ref, x_ref, wo_ref, o_ref, oh_ref):
    base = pl.multiple_of(pl.program_id(1) * WINDOW, WINDOW)
    kk = k_ref[0, pl.ds(base, 2 * WINDOW), :]
    vv = v_ref[0, pl.ds(base, 2 * WINDOW), :]
    r = lax.broadcasted_iota(jnp.int32, (WINDOW, 2 * WINDOW), 0)
    c = lax.broadcasted_iota(jnp.int32, (WINDOW, 2 * WINDOW), 1)
    mask = (c > r) & (c <= r + WINDOW) & (c + base >= KPAD - N_META)
    for h in range(N_HEADS):
        kv = h // GROUP
        hs = slice(h * HEAD_DIM, (h + 1) * HEAD_DIM)
        ks = slice(kv * HEAD_DIM, (kv + 1) * HEAD_DIM)
        s = jnp.where(mask, _dot_nt(q_ref[:, hs], kk[:, ks]), NEG)
        oh_ref[:, hs] = _softmax_av(s, sink_ref[h], vv[:, ks]).astype(BF16)
    o_ref[...] = x_ref[...] + _dot(oh_ref[...], wo_ref[...])


def _attn_main(sinks, q, kp, vp, x, wo):
    nb = SEQ // WINDOW
    row = lambda b, m: (b * nb + m, 0)
    return pl.pallas_call(
        _attn_main_body,
        grid=(BATCH, nb),
        in_specs=[
            pl.BlockSpec(memory_space=pltpu.SMEM),
            pl.BlockSpec((WINDOW, D_MODEL), row),
            pl.BlockSpec((1, KPAD + SEQ, KV_DIM), lambda b, m: (b, 0, 0)),
            pl.BlockSpec((1, KPAD + SEQ, KV_DIM), lambda b, m: (b, 0, 0)),
            pl.BlockSpec((WINDOW, D_MODEL), row),
            pl.BlockSpec((D_MODEL, D_MODEL), lambda b, m: (0, 0)),
        ],
        out_specs=pl.BlockSpec((WINDOW, D_MODEL), row),
        out_shape=jax.ShapeDtypeStruct((N_MAIN, D_MODEL), F32),
        scratch_shapes=[pltpu.VMEM((WINDOW, D_MODEL), BF16)],
        compiler_params=pltpu.CompilerParams(
            dimension_semantics=("arbitrary", "arbitrary"), vmem_limit_bytes=VMEM_LIMIT),
        name="attn_main",
    )(sinks, q, kp, vp, x, wo)


def _attn_small_body(sink_ref, q_ref, k_ref, v_ref, ck_ref, cv_ref, x_ref, wo_ref,
                     o_ref, nk_ref, nv_ref, oh_ref):
    nm = BATCH * N_META
    r = lax.broadcasted_iota(jnp.int32, (nm, nm), 0)
    c = lax.broadcasted_iota(jnp.int32, (nm, nm), 1)
    mmask = (c <= r) & ((r < N_META) == (c < N_META))
    km = k_ref[0:nm, :].astype(BF16)
    vm = v_ref[0:nm, :].astype(BF16)
    for h in range(N_HEADS):
        kv = h // GROUP
        hs = slice(h * HEAD_DIM, (h + 1) * HEAD_DIM)
        ks = slice(kv * HEAD_DIM, (kv + 1) * HEAD_DIM)
        s = jnp.where(mmask, _dot_nt(q_ref[0:nm, hs], km[:, ks]), NEG)
        oh_ref[0:nm, hs] = _softmax_av(s, sink_ref[h], vm[:, ks]).astype(BF16)

    nq = GROUP * DEC_BATCH
    nkeys = DEC_BATCH * WINDOW
    r = lax.broadcasted_iota(jnp.int32, (nq, nkeys), 0)
    c = lax.broadcasted_iota(jnp.int32, (nq, nkeys), 1)
    smask = ((c // WINDOW) == (r % DEC_BATCH)) & ((c % WINDOW) >= 1)
    for kv in range(N_KV_HEADS):
        ks = slice(kv * HEAD_DIM, (kv + 1) * HEAD_DIM)
        heads = range(kv * GROUP, (kv + 1) * GROUP)
        qs = jnp.concatenate(
            [q_ref[nm:, h * HEAD_DIM:(h + 1) * HEAD_DIM] for h in heads], axis=0)
        sink = jnp.concatenate(
            [jnp.full((DEC_BATCH, 1), sink_ref[h], F32) for h in heads], axis=0)
        k_new = jnp.tile(k_ref[nm:, ks], (GROUP, 1))
        v_new = jnp.tile(v_ref[nm:, ks], (GROUP, 1))
        s = jnp.where(smask, _dot_nt(qs, ck_ref[:, ks].astype(BF16)), NEG)
        s_new = jnp.sum(qs.astype(F32) * k_new, axis=-1, keepdims=True)
        mx = jnp.maximum(jnp.maximum(jnp.max(s, axis=-1, keepdims=True), s_new), sink)
        p = jnp.exp(s - mx)
        p_new = jnp.exp(s_new - mx)
        den = jnp.sum(p, axis=-1, keepdims=True) + p_new + jnp.exp(sink - mx)
        o = (_dot(p.astype(BF16), cv_ref[:, ks].astype(BF16)) + p_new * v_new) / den
        for gi, h in enumerate(heads):
            oh_ref[nm:, h * HEAD_DIM:(h + 1) * HEAD_DIM] = (
                o[gi * DEC_BATCH:(gi + 1) * DEC_BATCH].astype(BF16))

    o_ref[...] = x_ref[...] + _dot(oh_ref[...], wo_ref[...])

    last = lax.broadcasted_iota(jnp.int32, (WINDOW, 1), 0) == WINDOW - 1

    def shift(b, carry):
        rows = pl.ds(pl.multiple_of(b * WINDOW, WINDOW), WINDOW)
        for cache, new, out in ((ck_ref, k_ref, nk_ref), (cv_ref, v_ref, nv_ref)):
            rolled = pltpu.roll(cache[rows, :], WINDOW - 1, 0)
            out[rows, :] = jnp.where(last, new[pl.ds(nm + b, 1), :], rolled)
        return carry

    lax.fori_loop(0, DEC_BATCH, shift, 0)


def _attn_small(sinks, q, k, v, ck, cv, x, wo):
    vmem = pl.BlockSpec(memory_space=pltpu.VMEM)
    return pl.pallas_call(
        _attn_small_body,
        in_specs=[pl.BlockSpec(memory_space=pltpu.SMEM)] + [vmem] * 7,
        out_specs=[vmem] * 3,
        out_shape=[
            jax.ShapeDtypeStruct((N_SMALL, D_MODEL), F32),
            jax.ShapeDtypeStruct((DEC_BATCH * WINDOW, KV_DIM), F32),
            jax.ShapeDtypeStruct((DEC_BATCH * WINDOW, KV_DIM), F32),
        ],
        scratch_shapes=[pltpu.VMEM((N_SMALL, D_MODEL), BF16)],
        compiler_params=pltpu.CompilerParams(vmem_limit_bytes=VMEM_LIMIT),
        name="attn_small",
    )(sinks, q, k, v, ck, cv, x, wo)


def _rope_tables(pos):
    half = ROT_DIM // 2
    n = pos.shape[0]
    inv = jnp.float32(ROPE_THETA) ** (-jnp.arange(half, dtype=F32) * 2.0 / ROT_DIM)
    ang = pos.astype(F32)[:, None] * inv[None, :]
    cos, sin = jnp.cos(ang), jnp.sin(ang)
    rest = HEAD_DIM - ROT_DIM
    cos64 = jnp.concatenate([cos, cos, jnp.ones((n, rest), F32)], axis=1)
    sa64 = jnp.concatenate([-sin, jnp.zeros((n, half + rest), F32)], axis=1)
    sb64 = jnp.concatenate([jnp.zeros((n, half), F32), sin, jnp.zeros((n, rest), F32)], axis=1)
    rep = LANES // HEAD_DIM
    return tuple(jnp.tile(t, (1, rep)) for t in (cos64, sa64, sb64))


def kernel(x_prompt, x_sample, state_pool, cache_k, cache_v, meta_tokens, norm_mix, norm_ffn,
           pool_w, pool_scale, w_qkv, w_o, q_norm, k_norm, sinks, w_gate, w_up, w_down):
    assert x_prompt.shape == (BATCH, SEQ, D_MODEL) and x_sample.shape == (DEC_BATCH, 1, D_MODEL)
    nm = BATCH * N_META
    meta = meta_tokens.astype(F32)
    xm = x_prompt.reshape(N_MAIN, D_MODEL)
    xs = jnp.concatenate([meta, meta, x_sample.reshape(DEC_BATCH, D_MODEL)], axis=0)

    tabs_main = _rope_tables(N_META + jnp.arange(SEQ))
    tabs_small = _rope_tables(jnp.concatenate(
        [jnp.arange(N_META), jnp.arange(N_META), jnp.full((DEC_BATCH,), PAST_LEN)]))
    head_of_col = jnp.arange(QK_DIM) // HEAD_DIM
    seg = (head_of_col[:, None] == jnp.arange(LANES)[None, :]).astype(BF16)
    segt = seg.T

    pool_p, pool_s, kp_l, vp_l, ks_l, vs_l = [], [], [], [], [], []
    for i in range(DEPTH):
        j = i // 2
        g_mix = norm_mix[i].reshape(1, D_MODEL)
        if i % 2 == 0:
            w = pool_w[j].astype(BF16)
            sc = pool_scale[j].reshape(1, D_MODEL)
            st = jnp.swapaxes(state_pool[j], 0, 1)
            xm_new, h_last = _pool_main(xm, xs, g_mix, w, sc, tp=512)
            xs, ns = _pool_small(xs, st, g_mix, w, sc)
            xm = xm_new
            pool_p.append(h_last[:, N_META - POOL_STATE:])
            pool_s.append(jnp.swapaxes(ns, 0, 1))
        else:
            wqkv = w_qkv[j].astype(BF16)
            wo = w_o[j].astype(BF16)
            gain = jnp.concatenate(
                [jnp.tile(q_norm[j], N_HEADS), jnp.tile(k_norm[j], N_KV_HEADS)]).reshape(1, QK_DIM)
            q_m, k_m, v_m = _qkv(xm, g_mix, wqkv, gain, tabs_main, seg, segt, tm=512)
            q_s, k_s, v_s = _qkv(xs, g_mix, wqkv, gain, tabs_small, seg, segt, tm=N_SMALL)

            def padded(t_small, t_main):
                return jnp.concatenate(
                    [jnp.zeros((BATCH, KPAD - N_META, KV_DIM), F32),
                     t_small[:nm].reshape(BATCH, N_META, KV_DIM),
                     t_main.reshape(BATCH, SEQ, KV_DIM)], axis=1).astype(BF16)

            ck = cache_k[j].reshape(DEC_BATCH * WINDOW, KV_DIM)
            cv = cache_v[j].reshape(DEC_BATCH * WINDOW, KV_DIM)
            xm = _attn_main(sinks[j], q_m, padded(k_s, k_m), padded(v_s, v_m), xm, wo)
            xs, nk, nv = _attn_small(sinks[j], q_s, k_s, v_s, ck, cv, xs, wo)
            kp_l.append(k_m.reshape(BATCH, SEQ, N_KV_HEADS, HEAD_DIM)[:, -WINDOW:])
            vp_l.append(v_m.reshape(BATCH, SEQ, N_KV_HEADS, HEAD_DIM)[:, -WINDOW:])
            ks_l.append(nk.reshape(DEC_BATCH, WINDOW, N_KV_HEADS, HEAD_DIM))
            vs_l.append(nv.reshape(DEC_BATCH, WINDOW, N_KV_HEADS, HEAD_DIM))
        g_ffn = norm_ffn[i].reshape(1, D_MODEL)
        xm = _ffn(xm, g_ffn, w_gate[i], w_up[i], w_down[i], tm=1024, tf=512)
        xs = _ffn(xs, g_ffn, w_gate[i], w_up[i], w_down[i], tm=N_SMALL, tf=512)

    return (xm.reshape(BATCH, SEQ, D_MODEL), xs[nm:].reshape(DEC_BATCH, 1, D_MODEL),
            jnp.stack(pool_p), jnp.stack(kp_l), jnp.stack(vp_l),
            jnp.stack(pool_s), jnp.stack(ks_l), jnp.stack(vs_l))
```

```python
import functools

import jax
import jax.numpy as jnp
from jax import lax
from jax.experimental import pallas as pl
from jax.experimental.pallas import tpu as pltpu

F32 = jnp.float32
BF16 = jnp.bfloat16

D_MODEL = 2048
BATCH = 2
SEQ = 4096
DEPTH = 4
DEC_BATCH = 32
PAST_LEN = 16384
N_META = 16
POOL_WINDOWS = (2, 4, 8, 16)
N_POOL_GROUPS = len(POOL_WINDOWS)
POOL_GROUP = D_MODEL // N_POOL_GROUPS
POOL_STATE = max(POOL_WINDOWS) - 1
HEAD_DIM = 64
N_HEADS = D_MODEL // HEAD_DIM
N_KV_HEADS = 4
GROUP = N_HEADS // N_KV_HEADS
WINDOW = 128
ROT_DIM = HEAD_DIM // 4
ROPE_THETA = 500000.0
D_FF = 5632
EPS = 1e-6
NEG = -1e30

N_MAIN = BATCH * SEQ
N_SMALL = BATCH * N_META + DEC_BATCH
KV_DIM = N_KV_HEADS * HEAD_DIM
QK_DIM = D_MODEL + KV_DIM
QKV_DIM = D_MODEL + 2 * KV_DIM
LANES = 128
KPAD = WINDOW

VMEM_LIMIT = 60 * 1024 * 1024


def _rms(x, g):
    ms = jnp.mean(x * x, axis=-1, keepdims=True)
    return x * lax.rsqrt(ms + EPS) * g


def _dot(a, b):
    return jnp.dot(a, b, preferred_element_type=F32)


def _dot_nt(a, b):
    return lax.dot_general(a, b, (((1,), (1,)), ((), ())), preferred_element_type=F32)


def _split_bf16(x):
    hi = x.astype(BF16)
    lo = (x - hi.astype(F32)).astype(BF16)
    return hi, lo


def _ffn_body(x_ref, xs_ref, g_ref, wg_ref, wu_ref, wd_ref, o_ref, os_ref, h_ref, hs_ref):
    i, j = pl.program_id(0), pl.program_id(1)

    def start(src_ref, dst_ref, hid_ref):
        x = src_ref[...]
        hid_ref[...] = _rms(x, g_ref[...]).astype(BF16)
        dst_ref[...] = x

    def accumulate(hid_ref, dst_ref):
        h = hid_ref[...]
        gate = _dot(h, wg_ref[...].astype(BF16))
        up = _dot(h, wu_ref[...].astype(BF16))
        act = (gate * jax.nn.sigmoid(gate) * up).astype(BF16)
        dst_ref[...] += _dot(act, wd_ref[...].astype(BF16))

    pl.when(j == 0)(lambda: start(x_ref, o_ref, h_ref))
    pl.when((i == 0) & (j == 0))(lambda: start(xs_ref, os_ref, hs_ref))
    accumulate(h_ref, o_ref)
    pl.when(i == 0)(lambda: accumulate(hs_ref, os_ref))


def _ffn(x, xs, g, wg, wu, wd, layer, *, tm, tf):
    m, ms = x.shape[0], xs.shape[0]
    single = pl.Buffered(1)
    return pl.pallas_call(
        _ffn_body,
        grid=(m // tm, D_FF // tf),
        in_specs=[
            pl.BlockSpec((tm, D_MODEL), lambda i, j: (i, 0)),
            pl.BlockSpec((ms, D_MODEL), lambda i, j: (0, 0)),
            pl.BlockSpec((1, D_MODEL), lambda i, j: (0, 0)),
            pl.BlockSpec((None, D_MODEL, tf), lambda i, j: (layer, 0, j)),
            pl.BlockSpec((None, D_MODEL, tf), lambda i, j: (layer, 0, j)),
            pl.BlockSpec((None, tf, D_MODEL), lambda i, j: (layer, j, 0)),
        ],
        out_specs=[
            pl.BlockSpec((tm, D_MODEL), lambda i, j: (i, 0), pipeline_mode=single),
            pl.BlockSpec((ms, D_MODEL), lambda i, j: (0, 0)),
        ],
        out_shape=[
            jax.ShapeDtypeStruct((m, D_MODEL), F32),
            jax.ShapeDtypeStruct((ms, D_MODEL), F32),
        ],
        scratch_shapes=[pltpu.VMEM((tm, D_MODEL), BF16), pltpu.VMEM((ms, D_MODEL), BF16)],
        compiler_params=pltpu.CompilerParams(
            dimension_semantics=("arbitrary", "arbitrary"), vmem_limit_bytes=VMEM_LIMIT),
        name="ffn",
    )(x, xs, g, wg, wu, wd)


def _pool_project(diffs, w_ref, sc_ref):
    ys = [_dot(d.astype(BF16), w_ref[g]) for g, d in enumerate(diffs)]
    return jnp.concatenate(ys, axis=1) * sc_ref[...]


def _pool_main_body(x_ref, meta_ref, g_ref, w_ref, sc_ref, o_ref, hl_ref, ext_ref, *, tp):
    t = pl.program_id(1)

    @pl.when(t == 0)
    def _():
        ext_ref[0:N_META, :] = _rms(meta_ref[...], g_ref[...])

    x = x_ref[...]
    h = _rms(x, g_ref[...])
    ext_ref[N_META:N_META + tp, :] = h
    diffs = []
    for g, w in enumerate(POOL_WINDOWS):
        c0, c1 = g * POOL_GROUP, (g + 1) * POOL_GROUP
        s = ext_ref[:, c0:c1]
        shift = 1
        while shift < w:
            s = s + pltpu.roll(s, shift, 0)
            shift *= 2
        diffs.append(s[N_META:] * (1.0 / w) - h[:, c0:c1])
    o_ref[...] = x + _pool_project(diffs, w_ref, sc_ref)
    tail = ext_ref[tp:tp + N_META, :]
    ext_ref[0:N_META, :] = tail

    @pl.when(t == pl.num_programs(1) - 1)
    def _():
        hl_ref[0] = tail


def _pool_main(xm, xs, g, w, sc, *, tp):
    nt = SEQ // tp
    return pl.pallas_call(
        functools.partial(_pool_main_body, tp=tp),
        grid=(BATCH, nt),
        in_specs=[
            pl.BlockSpec((tp, D_MODEL), lambda b, t: (b * nt + t, 0)),
            pl.BlockSpec((N_META, D_MODEL), lambda b, t: (b, 0)),
            pl.BlockSpec((1, D_MODEL), lambda b, t: (0, 0)),
            pl.BlockSpec((N_POOL_GROUPS, POOL_GROUP, POOL_GROUP), lambda b, t: (0, 0, 0)),
            pl.BlockSpec((1, D_MODEL), lambda b, t: (0, 0)),
        ],
        out_specs=[
            pl.BlockSpec((tp, D_MODEL), lambda b, t: (b * nt + t, 0)),
            pl.BlockSpec((1, N_META, D_MODEL), lambda b, t: (b, 0, 0)),
        ],
        out_shape=[
            jax.ShapeDtypeStruct((N_MAIN, D_MODEL), F32),
            jax.ShapeDtypeStruct((BATCH, N_META, D_MODEL), F32),
        ],
        scratch_shapes=[pltpu.VMEM((tp + N_META, D_MODEL), F32)],
        compiler_params=pltpu.CompilerParams(
            dimension_semantics=("arbitrary", "arbitrary"), vmem_limit_bytes=VMEM_LIMIT),
        name="pool_main",
    )(xm, xs, g, w, sc)


def _pool_small_body(x_ref, st_ref, g_ref, w_ref, sc_ref, o_ref, ns_ref, ext_ref):
    nm = BATCH * N_META
    x = x_ref[...]
    h = _rms(x, g_ref[...])
    for b in range(BATCH):
        ext_ref[b, 0:N_META, :] = jnp.zeros((N_META, D_MODEL), F32)
        ext_ref[b, N_META:2 * N_META, :] = h[b * N_META:(b + 1) * N_META]
    hs = h[nm:]
    row = lax.broadcasted_iota(jnp.int32, (N_META, 1), 0)
    diffs = []
    for g, w in enumerate(POOL_WINDOWS):
        c0, c1 = g * POOL_GROUP, (g + 1) * POOL_GROUP
        cnt = jnp.minimum(row + 1, w).astype(F32)
        parts = []
        for b in range(BATCH):
            s = ext_ref[b, N_META:2 * N_META, c0:c1]
            for k in range(1, w):
                s = s + ext_ref[b, N_META - k:2 * N_META - k, c0:c1]
            parts.append(s / cnt)
        s = hs[:, c0:c1]
        for k in range(1, w):
            s = s + st_ref[POOL_STATE - k, :, c0:c1]
        parts.append(s / float(w))
        diffs.append(jnp.concatenate(parts, axis=0) - h[:, c0:c1])
    o_ref[...] = x + _pool_project(diffs, w_ref, sc_ref)
    ns_ref[0:POOL_STATE - 1] = st_ref[1:POOL_STATE]
    ns_ref[POOL_STATE - 1] = hs


def _pool_small(xs, st, g, w, sc):
    return pl.pallas_call(
        _pool_small_body,
        out_shape=[
            jax.ShapeDtypeStruct((N_SMALL, D_MODEL), F32),
            jax.ShapeDtypeStruct((POOL_STATE, DEC_BATCH, D_MODEL), F32),
        ],
        scratch_shapes=[pltpu.VMEM((BATCH, 2 * N_META, D_MODEL), F32)],
        compiler_params=pltpu.CompilerParams(vmem_limit_bytes=VMEM_LIMIT),
        name="pool_small",
    )(xs, st, g, w, sc)


QKV_CHUNK = 4 * HEAD_DIM


def _qkv_body(x_ref, g_ref, w_ref, gain_ref, cos_ref, sa_ref, sb_ref, ones_ref,
              q_ref, k_ref, v_ref, h_ref):
    h_ref[...] = _rms(x_ref[...], g_ref[...]).astype(BF16)
    cos, sa, sb = cos_ref[...], sa_ref[...], sb_ref[...]
    half = ROT_DIM // 2

    def norm_rope(a, c0):
        sq_hi, sq_lo = _split_bf16(a * a)
        ss = _dot(sq_hi, ones_ref[...]) + _dot(sq_lo, ones_ref[...])
        y = a * lax.rsqrt(ss * (1.0 / HEAD_DIM) + EPS) * gain_ref[:, c0:c0 + QKV_CHUNK]
        tiles = []
        for u in range(QKV_CHUNK // LANES):
            yu = y[:, u * LANES:(u + 1) * LANES]
            tiles.append(yu * cos + pltpu.roll(yu, LANES - half, 1) * sa + pltpu.roll(yu, half, 1) * sb)
        return jnp.concatenate(tiles, axis=1)

    wide = 2 * QKV_CHUNK
    nwide = QKV_DIM // wide
    project = lambda c: _dot(h_ref[...], w_ref[:, c * wide:(c + 1) * wide])
    a = project(0)
    for c in range(nwide):
        a_next = project(c + 1) if c + 1 < nwide else None
        c0 = c * wide
        if c0 < D_MODEL:
            for u in range(2):
                y = norm_rope(a[:, u * QKV_CHUNK:(u + 1) * QKV_CHUNK], c0 + u * QKV_CHUNK)
                q_ref[:, c0 + u * QKV_CHUNK:c0 + (u + 1) * QKV_CHUNK] = (
                    y * HEAD_DIM ** -0.5).astype(BF16)
        else:
            k_ref[...] = norm_rope(a[:, :KV_DIM], D_MODEL)
            v_ref[...] = a[:, KV_DIM:]
        a = a_next


def _qkv(x, g, w, gain, tabs, ones, *, tm):
    m = x.shape[0]
    nt = tabs[0].shape[0] // tm
    row = lambda i: (i, 0)
    tab = lambda i: (i % nt, 0)
    fixed = lambda i: (0, 0)
    return pl.pallas_call(
        _qkv_body,
        grid=(m // tm,),
        in_specs=[
            pl.BlockSpec((tm, D_MODEL), row),
            pl.BlockSpec((1, D_MODEL), fixed),
            pl.BlockSpec((D_MODEL, QKV_DIM), fixed),
            pl.BlockSpec((1, QK_DIM), fixed),
            pl.BlockSpec((tm, LANES), tab),
            pl.BlockSpec((tm, LANES), tab),
            pl.BlockSpec((tm, LANES), tab),
            pl.BlockSpec((QKV_CHUNK, QKV_CHUNK), fixed),
        ],
        out_specs=[
            pl.BlockSpec((tm, D_MODEL), row),
            pl.BlockSpec((tm, KV_DIM), row),
            pl.BlockSpec((tm, KV_DIM), row),
        ],
        out_shape=[
            jax.ShapeDtypeStruct((m, D_MODEL), BF16),
            jax.ShapeDtypeStruct((m, KV_DIM), F32),
            jax.ShapeDtypeStruct((m, KV_DIM), F32),
        ],
        scratch_shapes=[pltpu.VMEM((tm, D_MODEL), BF16)],
        compiler_params=pltpu.CompilerParams(
            dimension_semantics=("arbitrary",), vmem_limit_bytes=VMEM_LIMIT),
        name="qkv",
    )(x, g, w, gain, *tabs, ones)


def _softmax_av(s, sink, v):
    mx = jnp.maximum(jnp.max(s, axis=-1, keepdims=True), sink)
    p = jnp.exp(s - mx)
    den = jnp.sum(p, axis=-1, keepdims=True) + jnp.exp(sink - mx)
    return _dot(p.astype(BF16), v) / den


SLOTS = 4
CHUNK = SLOTS * HEAD_DIM


def _attn_main_body(sink_ref, q_ref, k_ref, v_ref, x_ref, wo_ref, o_ref, oh_ref, *, nblk):
    rows2 = 2 * WINDOW
    r = lax.broadcasted_iota(jnp.int32, (rows2, WINDOW), 0) % WINDOW
    c = lax.broadcasted_iota(jnp.int32, (rows2, WINDOW), 1)
    prev = c > r
    top = lax.broadcasted_iota(jnp.int32, (rows2, 1), 0) < WINDOW
    lane_slot = lax.broadcasted_iota(jnp.int32, (rows2, CHUNK), 1) // HEAD_DIM
    slot_mask = [(lane_slot == s).astype(F32).astype(BF16) for s in range(SLOTS)]
    zero = jnp.zeros((rows2, WINDOW), F32)
    step_base = pl.program_id(1) * (nblk * WINDOW)

    def block(i, carry):
        row0 = pl.multiple_of(i * WINDOW, WINDOW)
        base = pl.multiple_of(step_base + row0, WINDOW)
        seen = jnp.logical_not(prev) | (c + base >= KPAD - N_META)
        q_rows = pl.ds(row0, WINDOW)
        k_rows = pl.ds(base, rows2)
        for kv in range(N_KV_HEADS):
            kd = k_ref[0, k_rows, kv * LANES:(kv + 1) * LANES]
            vd = v_ref[0, k_rows, kv * LANES:(kv + 1) * LANES]
            k4 = jnp.concatenate([kd, kd], axis=1)
            v4 = jnp.concatenate([vd, vd], axis=1)
            g0 = kv * GROUP * HEAD_DIM
            qst = jnp.concatenate(
                [q_ref[q_rows, g0:g0 + CHUNK], q_ref[q_rows, g0 + CHUNK:g0 + 2 * CHUNK]], axis=0)
            acc = jnp.zeros((rows2, CHUNK), F32)
            for s in range(SLOTS):
                sc = _dot_nt(qst, k4 * slot_mask[s])
                sf = jnp.where(prev, sc[:, :WINDOW], sc[:, WINDOW:])
                sf = jnp.where(seen, sf, NEG)
                sink = jnp.where(top, sink_ref[kv * GROUP + s], sink_ref[kv * GROUP + SLOTS + s])
                mx = jnp.maximum(jnp.max(sf, axis=-1, keepdims=True), sink)
                p = jnp.exp(sf - mx)
                den = jnp.sum(p, axis=-1, keepdims=True) + jnp.exp(sink - mx)
                p = p * (1.0 / den)
                p2 = jnp.concatenate([jnp.where(prev, p, zero), jnp.where(prev, zero, p)], axis=1)
                acc = acc + _dot(p2.astype(BF16), v4 * slot_mask[s])
            oh_ref[q_rows, g0:g0 + CHUNK] = acc[:WINDOW].astype(BF16)
            oh_ref[q_rows, g0 + CHUNK:g0 + 2 * CHUNK] = acc[WINDOW:].astype(BF16)
        return carry

    lax.fori_loop(0, nblk, block, 0)
    o_ref[...] = x_ref[...] + _dot(oh_ref[...], wo_ref[...])


def _attn_main(sinks, q, kp, vp, x, wo, *, tq):
    nt = SEQ // tq
    row = lambda b, t: (b * nt + t, 0)
    single = pl.Buffered(1)
    kv_spec = pl.BlockSpec((1, KPAD + SEQ, 2 * KV_DIM), lambda b, t: (b, 0, 0), pipeline_mode=single)
    return pl.pallas_call(
        functools.partial(_attn_main_body, nblk=tq // WINDOW),
        grid=(BATCH, nt),
        in_specs=[
            pl.BlockSpec(memory_space=pltpu.SMEM),
            pl.BlockSpec((tq, D_MODEL), row),
            kv_spec,
            kv_spec,
            pl.BlockSpec((tq, D_MODEL), row),
            pl.BlockSpec((D_MODEL, D_MODEL), lambda b, t: (0, 0), pipeline_mode=single),
        ],
        out_specs=pl.BlockSpec((tq, D_MODEL), row),
        out_shape=jax.ShapeDtypeStruct((N_MAIN, D_MODEL), F32),
        scratch_shapes=[pltpu.VMEM((tq, D_MODEL), BF16)],
        compiler_params=pltpu.CompilerParams(
            dimension_semantics=("arbitrary", "arbitrary"), vmem_limit_bytes=VMEM_LIMIT),
        name="attn_main",
    )(sinks, q, kp, vp, x, wo)


def _attn_small_body(sink_ref, q_ref, k_ref, v_ref, ck_ref, cv_ref, x_ref, wo_ref,
                     o_ref, nk_ref, nv_ref, oh_ref):
    nm = BATCH * N_META
    r = lax.broadcasted_iota(jnp.int32, (nm, nm), 0)
    c = lax.broadcasted_iota(jnp.int32, (nm, nm), 1)
    mmask = (c <= r) & ((r < N_META) == (c < N_META))
    km = k_ref[0:nm, :].astype(BF16)
    vm = v_ref[0:nm, :].astype(BF16)
    for h in range(N_HEADS):
        kv = h // GROUP
        hs = slice(h * HEAD_DIM, (h + 1) * HEAD_DIM)
        ks = slice(kv * HEAD_DIM, (kv + 1) * HEAD_DIM)
        s = jnp.where(mmask, _dot_nt(q_ref[0:nm, hs], km[:, ks]), NEG)
        oh_ref[0:nm, hs] = _softmax_av(s, sink_ref[h], vm[:, ks]).astype(BF16)

    nq = GROUP * DEC_BATCH
    nkeys = DEC_BATCH * WINDOW
    r = lax.broadcasted_iota(jnp.int32, (nq, nkeys), 0)
    c = lax.broadcasted_iota(jnp.int32, (nq, nkeys), 1)
    smask = ((c // WINDOW) == (r % DEC_BATCH)) & ((c % WINDOW) >= 1)
    for kv in range(N_KV_HEADS):
        ks = slice(kv * HEAD_DIM, (kv + 1) * HEAD_DIM)
        heads = range(kv * GROUP, (kv + 1) * GROUP)
        qs = jnp.concatenate(
            [q_ref[nm:, h * HEAD_DIM:(h + 1) * HEAD_DIM] for h in heads], axis=0)
        sink = jnp.concatenate(
            [jnp.full((DEC_BATCH, 1), sink_ref[h], F32) for h in heads], axis=0)
        k_new = jnp.tile(k_ref[nm:, ks], (GROUP, 1))
        v_new = jnp.tile(v_ref[nm:, ks], (GROUP, 1))
        s = jnp.where(smask, _dot_nt(qs, ck_ref[:, ks].astype(BF16)), NEG)
        s_new = jnp.sum(qs.astype(F32) * k_new, axis=-1, keepdims=True)
        mx = jnp.maximum(jnp.maximum(jnp.max(s, axis=-1, keepdims=True), s_new), sink)
        p = jnp.exp(s - mx)
        p_new = jnp.exp(s_new - mx)
        den = jnp.sum(p, axis=-1, keepdims=True) + p_new + jnp.exp(sink - mx)
        o = (_dot(p.astype(BF16), cv_ref[:, ks].astype(BF16)) + p_new * v_new) / den
        for gi, h in enumerate(heads):
            oh_ref[nm:, h * HEAD_DIM:(h + 1) * HEAD_DIM] = (
                o[gi * DEC_BATCH:(gi + 1) * DEC_BATCH].astype(BF16))

    o_ref[...] = x_ref[...] + _dot(oh_ref[...], wo_ref[...])

    last = lax.broadcasted_iota(jnp.int32, (WINDOW, 1), 0) == WINDOW - 1

    def shift(b, carry):
        rows = pl.ds(pl.multiple_of(b * WINDOW, WINDOW), WINDOW)
        for cache, new, out in ((ck_ref, k_ref, nk_ref), (cv_ref, v_ref, nv_ref)):
            rolled = pltpu.roll(cache[rows, :], WINDOW - 1, 0)
            out[rows, :] = jnp.where(last, new[pl.ds(nm + b, 1), :], rolled)
        return carry

    lax.fori_loop(0, DEC_BATCH, shift, 0)


def _attn_small(sinks, q, k, v, ck, cv, x, wo):
    vmem = pl.BlockSpec(memory_space=pltpu.VMEM)
    return pl.pallas_call(
        _attn_small_body,
        in_specs=[pl.BlockSpec(memory_space=pltpu.SMEM)] + [vmem] * 7,
        out_specs=[vmem] * 3,
        out_shape=[
            jax.ShapeDtypeStruct((N_SMALL, D_MODEL), F32),
            jax.ShapeDtypeStruct((DEC_BATCH * WINDOW, KV_DIM), F32),
            jax.ShapeDtypeStruct((DEC_BATCH * WINDOW, KV_DIM), F32),
        ],
        scratch_shapes=[pltpu.VMEM((N_SMALL, D_MODEL), BF16)],
        compiler_params=pltpu.CompilerParams(vmem_limit_bytes=VMEM_LIMIT),
        name="attn_small",
    )(sinks, q, k, v, ck, cv, x, wo)


def _rope_tables(pos):
    half = ROT_DIM // 2
    n = pos.shape[0]
    inv = jnp.float32(ROPE_THETA) ** (-jnp.arange(half, dtype=F32) * 2.0 / ROT_DIM)
    ang = pos.astype(F32)[:, None] * inv[None, :]
    cos, sin = jnp.cos(ang), jnp.sin(ang)
    rest = HEAD_DIM - ROT_DIM
    cos64 = jnp.concatenate([cos, cos, jnp.ones((n, rest), F32)], axis=1)
    sa64 = jnp.concatenate([-sin, jnp.zeros((n, half + rest), F32)], axis=1)
    sb64 = jnp.concatenate([jnp.zeros((n, half), F32), sin, jnp.zeros((n, rest), F32)], axis=1)
    rep = LANES // HEAD_DIM
    return tuple(jnp.tile(t, (1, rep)) for t in (cos64, sa64, sb64))


def kernel(x_prompt, x_sample, state_pool, cache_k, cache_v, meta_tokens, norm_mix, norm_ffn,
           pool_w, pool_scale, w_qkv, w_o, q_norm, k_norm, sinks, w_gate, w_up, w_down):
    assert x_prompt.shape == (BATCH, SEQ, D_MODEL) and x_sample.shape == (DEC_BATCH, 1, D_MODEL)
    nm = BATCH * N_META
    meta = meta_tokens.astype(F32)
    xm = x_prompt.reshape(N_MAIN, D_MODEL)
    xs = jnp.concatenate([meta, meta, x_sample.reshape(DEC_BATCH, D_MODEL)], axis=0)

    tabs_main = _rope_tables(N_META + jnp.arange(SEQ))
    tabs_small = _rope_tables(jnp.concatenate(
        [jnp.arange(N_META), jnp.arange(N_META), jnp.full((DEC_BATCH,), PAST_LEN)]))
    head_of_col = jnp.arange(QKV_CHUNK) // HEAD_DIM
    ones = (head_of_col[:, None] == head_of_col[None, :]).astype(BF16)

    pool_p, pool_s, kp_l, vp_l, ks_l, vs_l = [], [], [], [], [], []
    for i in range(DEPTH):
        j = i // 2
        g_mix = norm_mix[i].reshape(1, D_MODEL)
        if i % 2 == 0:
            w = pool_w[j].astype(BF16)
            sc = pool_scale[j].reshape(1, D_MODEL)
            st = jnp.swapaxes(state_pool[j], 0, 1)
            xm_new, h_last = _pool_main(xm, xs, g_mix, w, sc, tp=512)
            xs, ns = _pool_small(xs, st, g_mix, w, sc)
            xm = xm_new
            pool_p.append(h_last[:, N_META - POOL_STATE:])
            pool_s.append(jnp.swapaxes(ns, 0, 1))
        else:
            wqkv = w_qkv[j].astype(BF16)
            wo = w_o[j].astype(BF16)
            gain = jnp.concatenate(
                [jnp.tile(q_norm[j], N_HEADS), jnp.tile(k_norm[j], N_KV_HEADS)]).reshape(1, QK_DIM)
            q_m, k_m, v_m = _qkv(xm, g_mix, wqkv, gain, tabs_main, ones, tm=512)
            q_s, k_s, v_s = _qkv(xs, g_mix, wqkv, gain, tabs_small, ones, tm=N_SMALL)

            def padded(t_small, t_main):
                t = jnp.concatenate(
                    [jnp.zeros((BATCH, KPAD - N_META, KV_DIM), F32),
                     t_small[:nm].reshape(BATCH, N_META, KV_DIM),
                     t_main.reshape(BATCH, SEQ, KV_DIM)], axis=1).astype(BF16)
                t = t.reshape(BATCH, KPAD + SEQ, N_KV_HEADS, 1, HEAD_DIM)
                return jnp.tile(t, (1, 1, 1, 2, 1)).reshape(BATCH, KPAD + SEQ, 2 * KV_DIM)

            ck = cache_k[j].reshape(DEC_BATCH * WINDOW, KV_DIM)
            cv = cache_v[j].reshape(DEC_BATCH * WINDOW, KV_DIM)
            xm = _attn_main(sinks[j], q_m, padded(k_s, k_m), padded(v_s, v_m), xm, wo, tq=512)
            xs, nk, nv = _attn_small(sinks[j], q_s, k_s, v_s, ck, cv, xs, wo)
            kp_l.append(k_m.reshape(BATCH, SEQ, N_KV_HEADS, HEAD_DIM)[:, -WINDOW:])
            vp_l.append(v_m.reshape(BATCH, SEQ, N_KV_HEADS, HEAD_DIM)[:, -WINDOW:])
            ks_l.append(nk.reshape(DEC_BATCH, WINDOW, N_KV_HEADS, HEAD_DIM))
            vs_l.append(nv.reshape(DEC_BATCH, WINDOW, N_KV_HEADS, HEAD_DIM))
        g_ffn = norm_ffn[i].reshape(1, D_MODEL)
        xm, xs = _ffn(xm, xs, g_ffn, w_gate, w_up, w_down, i, tm=1024, tf=512)

    return (xm.reshape(BATCH, SEQ, D_MODEL), xs[nm:].reshape(DEC_BATCH, 1, D_MODEL),
            jnp.stack(pool_p), jnp.stack(kp_l), jnp.stack(vp_l),
            jnp.stack(pool_s), jnp.stack(ks_l), jnp.stack(vs_l))
```

```python
import functools

import jax
import jax.numpy as jnp
from jax import lax
from jax.experimental import pallas as pl
from jax.experimental.pallas import tpu as pltpu

F32 = jnp.float32
BF16 = jnp.bfloat16

D_MODEL = 2048
BATCH = 2
SEQ = 4096
DEPTH = 4
DEC_BATCH = 32
PAST_LEN = 16384
N_META = 16
POOL_WINDOWS = (2, 4, 8, 16)
N_POOL_GROUPS = len(POOL_WINDOWS)
POOL_GROUP = D_MODEL // N_POOL_GROUPS
POOL_STATE = max(POOL_WINDOWS) - 1
HEAD_DIM = 64
N_HEADS = D_MODEL // HEAD_DIM
N_KV_HEADS = 4
GROUP = N_HEADS // N_KV_HEADS
WINDOW = 128
ROT_DIM = HEAD_DIM // 4
ROPE_THETA = 500000.0
D_FF = 5632
EPS = 1e-6
NEG = -1e30

N_MAIN = BATCH * SEQ
N_SMALL = BATCH * N_META + DEC_BATCH
KV_DIM = N_KV_HEADS * HEAD_DIM
QK_DIM = D_MODEL + KV_DIM
QKV_DIM = D_MODEL + 2 * KV_DIM
LANES = 128

VMEM_LIMIT = 60 * 1024 * 1024


def _rms(x, g):
    ms = jnp.mean(x * x, axis=-1, keepdims=True)
    return x * lax.rsqrt(ms + EPS) * g


def _dot(a, b):
    return jnp.dot(a, b, preferred_element_type=F32)


def _dot_nt(a, b):
    return lax.dot_general(a, b, (((1,), (1,)), ((), ())), preferred_element_type=F32)


def _split_bf16(x):
    hi = x.astype(BF16)
    lo = (x - hi.astype(F32)).astype(BF16)
    return hi, lo


def _ffn_body(x_ref, xs_ref, g_ref, wg_ref, wu_ref, wd_ref, o_ref, os_ref, h_ref, hs_ref):
    i, j = pl.program_id(0), pl.program_id(1)

    def start(src_ref, dst_ref, hid_ref):
        x = src_ref[...]
        hid_ref[...] = _rms(x, g_ref[...]).astype(BF16)
        dst_ref[...] = x

    def accumulate(hid_ref, dst_ref):
        h = hid_ref[...]
        gate = _dot(h, wg_ref[...].astype(BF16))
        up = _dot(h, wu_ref[...].astype(BF16))
        act = (gate * jax.nn.sigmoid(gate) * up).astype(BF16)
        dst_ref[...] += _dot(act, wd_ref[...].astype(BF16))

    pl.when(j == 0)(lambda: start(x_ref, o_ref, h_ref))
    pl.when((i == 0) & (j == 0))(lambda: start(xs_ref, os_ref, hs_ref))
    accumulate(h_ref, o_ref)
    pl.when(i == 0)(lambda: accumulate(hs_ref, os_ref))


def _ffn(x, xs, g, wg, wu, wd, layer, *, tm, tf):
    m, ms = x.shape[0], xs.shape[0]
    single = pl.Buffered(1)
    return pl.pallas_call(
        _ffn_body,
        grid=(m // tm, D_FF // tf),
        in_specs=[
            pl.BlockSpec((tm, D_MODEL), lambda i, j: (i, 0)),
            pl.BlockSpec((ms, D_MODEL), lambda i, j: (0, 0)),
            pl.BlockSpec((1, D_MODEL), lambda i, j: (0, 0)),
            pl.BlockSpec((None, D_MODEL, tf), lambda i, j: (layer, 0, j)),
            pl.BlockSpec((None, D_MODEL, tf), lambda i, j: (layer, 0, j)),
            pl.BlockSpec((None, tf, D_MODEL), lambda i, j: (layer, j, 0)),
        ],
        out_specs=[
            pl.BlockSpec((tm, D_MODEL), lambda i, j: (i, 0), pipeline_mode=single),
            pl.BlockSpec((ms, D_MODEL), lambda i, j: (0, 0)),
        ],
        out_shape=[
            jax.ShapeDtypeStruct((m, D_MODEL), F32),
            jax.ShapeDtypeStruct((ms, D_MODEL), F32),
        ],
        scratch_shapes=[pltpu.VMEM((tm, D_MODEL), BF16), pltpu.VMEM((ms, D_MODEL), BF16)],
        compiler_params=pltpu.CompilerParams(
            dimension_semantics=("arbitrary", "arbitrary"), vmem_limit_bytes=VMEM_LIMIT),
        name="ffn",
    )(x, xs, g, wg, wu, wd)


def _pool_project(diffs, w_ref, sc_ref):
    ys = [_dot(d.astype(BF16), w_ref[g]) for g, d in enumerate(diffs)]
    return jnp.concatenate(ys, axis=1) * sc_ref[...]


def _pool_main_body(x_ref, meta_ref, g_ref, w_ref, sc_ref, o_ref, hl_ref, ext_ref, *, tp):
    t = pl.program_id(1)

    @pl.when(t == 0)
    def _():
        ext_ref[0:N_META, :] = _rms(meta_ref[...], g_ref[...])

    x = x_ref[...]
    h = _rms(x, g_ref[...])
    ext_ref[N_META:N_META + tp, :] = h
    diffs = []
    for g, w in enumerate(POOL_WINDOWS):
        c0, c1 = g * POOL_GROUP, (g + 1) * POOL_GROUP
        s = ext_ref[:, c0:c1]
        shift = 1
        while shift < w:
            s = s + pltpu.roll(s, shift, 0)
            shift *= 2
        diffs.append(s[N_META:] * (1.0 / w) - h[:, c0:c1])
    o_ref[...] = x + _pool_project(diffs, w_ref, sc_ref)
    tail = ext_ref[tp:tp + N_META, :]
    ext_ref[0:N_META, :] = tail

    @pl.when(t == pl.num_programs(1) - 1)
    def _():
        hl_ref[0] = tail


def _pool_main(xm, xs, g, w, sc, *, tp):
    nt = SEQ // tp
    return pl.pallas_call(
        functools.partial(_pool_main_body, tp=tp),
        grid=(BATCH, nt),
        in_specs=[
            pl.BlockSpec((tp, D_MODEL), lambda b, t: (b * nt + t, 0)),
            pl.BlockSpec((N_META, D_MODEL), lambda b, t: (b, 0)),
            pl.BlockSpec((1, D_MODEL), lambda b, t: (0, 0)),
            pl.BlockSpec((N_POOL_GROUPS, POOL_GROUP, POOL_GROUP), lambda b, t: (0, 0, 0)),
            pl.BlockSpec((1, D_MODEL), lambda b, t: (0, 0)),
        ],
        out_specs=[
            pl.BlockSpec((tp, D_MODEL), lambda b, t: (b * nt + t, 0)),
            pl.BlockSpec((1, N_META, D_MODEL), lambda b, t: (b, 0, 0)),
        ],
        out_shape=[
            jax.ShapeDtypeStruct((N_MAIN, D_MODEL), F32),
            jax.ShapeDtypeStruct((BATCH, N_META, D_MODEL), F32),
        ],
        scratch_shapes=[pltpu.VMEM((tp + N_META, D_MODEL), F32)],
        compiler_params=pltpu.CompilerParams(
            dimension_semantics=("arbitrary", "arbitrary"), vmem_limit_bytes=VMEM_LIMIT),
        name="pool_main",
    )(xm, xs, g, w, sc)


def _pool_small_body(x_ref, st_ref, g_ref, w_ref, sc_ref, o_ref, ns_ref, ext_ref):
    nm = BATCH * N_META
    x = x_ref[...]
    h = _rms(x, g_ref[...])
    for b in range(BATCH):
        ext_ref[b, 0:N_META, :] = jnp.zeros((N_META, D_MODEL), F32)
        ext_ref[b, N_META:2 * N_META, :] = h[b * N_META:(b + 1) * N_META]
    hs = h[nm:]
    row = lax.broadcasted_iota(jnp.int32, (N_META, 1), 0)
    diffs = []
    for g, w in enumerate(POOL_WINDOWS):
        c0, c1 = g * POOL_GROUP, (g + 1) * POOL_GROUP
        cnt = jnp.minimum(row + 1, w).astype(F32)
        parts = []
        for b in range(BATCH):
            s = ext_ref[b, N_META:2 * N_META, c0:c1]
            for k in range(1, w):
                s = s + ext_ref[b, N_META - k:2 * N_META - k, c0:c1]
            parts.append(s / cnt)
        s = hs[:, c0:c1]
        for k in range(1, w):
            s = s + st_ref[POOL_STATE - k, :, c0:c1]
        parts.append(s / float(w))
        diffs.append(jnp.concatenate(parts, axis=0) - h[:, c0:c1])
    o_ref[...] = x + _pool_project(diffs, w_ref, sc_ref)
    ns_ref[0:POOL_STATE - 1] = st_ref[1:POOL_STATE]
    ns_ref[POOL_STATE - 1] = hs


def _pool_small(xs, st, g, w, sc):
    return pl.pallas_call(
        _pool_small_body,
        out_shape=[
            jax.ShapeDtypeStruct((N_SMALL, D_MODEL), F32),
            jax.ShapeDtypeStruct((POOL_STATE, DEC_BATCH, D_MODEL), F32),
        ],
        scratch_shapes=[pltpu.VMEM((BATCH, 2 * N_META, D_MODEL), F32)],
        compiler_params=pltpu.CompilerParams(vmem_limit_bytes=VMEM_LIMIT),
        name="pool_small",
    )(xs, st, g, w, sc)


QKV_CHUNK = 4 * HEAD_DIM


def _qkv_body(x_ref, g_ref, w_ref, gain_ref, cos_ref, sa_ref, sb_ref, ones_ref,
              q_ref, k_ref, v_ref, kd_ref, vd_ref, h_ref):
    h_ref[...] = _rms(x_ref[...], g_ref[...]).astype(BF16)
    cos, sa, sb = cos_ref[...], sa_ref[...], sb_ref[...]
    half = ROT_DIM // 2

    def norm_rope(a, c0):
        sq_hi, sq_lo = _split_bf16(a * a)
        ss = _dot(sq_hi, ones_ref[...]) + _dot(sq_lo, ones_ref[...])
        y = a * lax.rsqrt(ss * (1.0 / HEAD_DIM) + EPS) * gain_ref[:, c0:c0 + QKV_CHUNK]
        tiles = []
        for u in range(QKV_CHUNK // LANES):
            yu = y[:, u * LANES:(u + 1) * LANES]
            tiles.append(yu * cos + pltpu.roll(yu, LANES - half, 1) * sa + pltpu.roll(yu, half, 1) * sb)
        return jnp.concatenate(tiles, axis=1)

    lower = lax.broadcasted_iota(jnp.int32, (1, LANES), 1) < HEAD_DIM

    def twice(t):
        tiles = []
        for u in range(KV_DIM // LANES):
            tu = t[:, u * LANES:(u + 1) * LANES]
            swapped = pltpu.roll(tu, HEAD_DIM, 1)
            tiles += [jnp.where(lower, tu, swapped), jnp.where(lower, swapped, tu)]
        return jnp.concatenate(tiles, axis=1).astype(BF16)

    wide = 2 * QKV_CHUNK
    nwide = QKV_DIM // wide
    project = lambda c: _dot(h_ref[...], w_ref[:, c * wide:(c + 1) * wide])
    a = project(0)
    for c in range(nwide):
        a_next = project(c + 1) if c + 1 < nwide else None
        c0 = c * wide
        if c0 < D_MODEL:
            for u in range(2):
                y = norm_rope(a[:, u * QKV_CHUNK:(u + 1) * QKV_CHUNK], c0 + u * QKV_CHUNK)
                q_ref[:, c0 + u * QKV_CHUNK:c0 + (u + 1) * QKV_CHUNK] = (
                    y * HEAD_DIM ** -0.5).astype(BF16)
        else:
            k = norm_rope(a[:, :KV_DIM], D_MODEL)
            v = a[:, KV_DIM:]
            k_ref[...] = k
            v_ref[...] = v
            kd_ref[...] = twice(k)
            vd_ref[...] = twice(v)
        a = a_next


def _qkv(x, g, w, layer, gain, tabs, ones, *, tm):
    m = x.shape[0]
    nt = tabs[0].shape[0] // tm
    row = lambda i: (i, 0)
    tab = lambda i: (i % nt, 0)
    fixed = lambda i: (0, 0)
    return pl.pallas_call(
        _qkv_body,
        grid=(m // tm,),
        in_specs=[
            pl.BlockSpec((tm, D_MODEL), row),
            pl.BlockSpec((1, D_MODEL), fixed),
            pl.BlockSpec((None, D_MODEL, QKV_DIM), lambda i: (layer, 0, 0)),
            pl.BlockSpec((1, QK_DIM), fixed),
            pl.BlockSpec((tm, LANES), tab),
            pl.BlockSpec((tm, LANES), tab),
            pl.BlockSpec((tm, LANES), tab),
            pl.BlockSpec((QKV_CHUNK, QKV_CHUNK), fixed),
        ],
        out_specs=[
            pl.BlockSpec((tm, D_MODEL), row),
            pl.BlockSpec((tm, KV_DIM), row),
            pl.BlockSpec((tm, KV_DIM), row),
            pl.BlockSpec((tm, 2 * KV_DIM), row),
            pl.BlockSpec((tm, 2 * KV_DIM), row),
        ],
        out_shape=[
            jax.ShapeDtypeStruct((m, D_MODEL), BF16),
            jax.ShapeDtypeStruct((m, KV_DIM), F32),
            jax.ShapeDtypeStruct((m, KV_DIM), F32),
            jax.ShapeDtypeStruct((m, 2 * KV_DIM), BF16),
            jax.ShapeDtypeStruct((m, 2 * KV_DIM), BF16),
        ],
        scratch_shapes=[pltpu.VMEM((tm, D_MODEL), BF16)],
        compiler_params=pltpu.CompilerParams(
            dimension_semantics=("arbitrary",), vmem_limit_bytes=VMEM_LIMIT),
        name="qkv",
    )(x, g, w, gain, *tabs, ones)


def _softmax_av(s, sink, v):
    mx = jnp.maximum(jnp.max(s, axis=-1, keepdims=True), sink)
    p = jnp.exp(s - mx)
    den = jnp.sum(p, axis=-1, keepdims=True) + jnp.exp(sink - mx)
    return _dot(p.astype(BF16), v) / den


SLOTS = 4
CHUNK = SLOTS * HEAD_DIM


def _attn_main_body(sink_ref, q_ref, k_ref, v_ref, km_ref, vm_ref, x_ref, wo_ref, o_ref, oh_ref,
                    *, nblk):
    rows2 = 2 * WINDOW
    r = lax.broadcasted_iota(jnp.int32, (rows2, WINDOW), 0) % WINDOW
    c = lax.broadcasted_iota(jnp.int32, (rows2, WINDOW), 1)
    prev = c > r
    top = lax.broadcasted_iota(jnp.int32, (rows2, 1), 0) < WINDOW
    lane_slot = lax.broadcasted_iota(jnp.int32, (rows2, CHUNK), 1) // HEAD_DIM
    slot_mask = [(lane_slot == s).astype(F32).astype(BF16) for s in range(SLOTS)]
    zero = jnp.zeros((rows2, WINDOW), F32)
    step = pl.program_id(1)
    seen = jnp.logical_not(prev) | (c >= WINDOW - N_META)

    def block(i, carry):
        row0 = pl.multiple_of(i * WINDOW, WINDOW)
        cur0 = pl.multiple_of(step * (nblk * WINDOW) + row0, WINDOW)
        past0 = pl.multiple_of(jnp.maximum(cur0 - WINDOW, 0), WINDOW)
        opens = cur0 == 0
        visible = seen | jnp.logical_not(opens)
        q_rows = pl.ds(row0, WINDOW)

        def keys_values(kv):
            lanes = slice(kv * LANES, (kv + 1) * LANES)
            k_past = jnp.where(opens, km_ref[0, :, lanes], k_ref[0, pl.ds(past0, WINDOW), lanes])
            v_past = jnp.where(opens, vm_ref[0, :, lanes], v_ref[0, pl.ds(past0, WINDOW), lanes])
            kd = jnp.concatenate([k_past, k_ref[0, pl.ds(cur0, WINDOW), lanes]], axis=0)
            vd = jnp.concatenate([v_past, v_ref[0, pl.ds(cur0, WINDOW), lanes]], axis=0)
            return jnp.concatenate([kd, kd], axis=1), jnp.concatenate([vd, vd], axis=1)

        def scores(kv, k4):
            g0 = kv * GROUP * HEAD_DIM
            qst = jnp.concatenate(
                [q_ref[q_rows, g0:g0 + CHUNK], q_ref[q_rows, g0 + CHUNK:g0 + 2 * CHUNK]], axis=0)
            return [_dot_nt(qst, k4 * slot_mask[s]) for s in range(SLOTS)]

        def weights(kv, s, sc):
            sf = jnp.where(prev, sc[:, :WINDOW], sc[:, WINDOW:])
            sf = jnp.where(visible, sf, NEG)
            sink = jnp.where(top, sink_ref[kv * GROUP + s], sink_ref[kv * GROUP + SLOTS + s])
            mx = jnp.maximum(jnp.max(sf, axis=-1, keepdims=True), sink)
            p = jnp.exp(sf - mx)
            den = jnp.sum(p, axis=-1, keepdims=True) + jnp.exp(sink - mx)
            p = p * (1.0 / den)
            p2 = jnp.concatenate([jnp.where(prev, p, zero), jnp.where(prev, zero, p)], axis=1)
            return p2.astype(BF16)

        k4, v4 = keys_values(0)
        sc = scores(0, k4)
        for kv in range(N_KV_HEADS):
            if kv + 1 < N_KV_HEADS:
                k4_next, v4_next = keys_values(kv + 1)
                sc_next = scores(kv + 1, k4_next)
            acc = jnp.zeros((rows2, CHUNK), F32)
            for s in range(SLOTS):
                acc = acc + _dot(weights(kv, s, sc[s]), v4 * slot_mask[s])
            g0 = kv * GROUP * HEAD_DIM
            oh_ref[q_rows, g0:g0 + CHUNK] = acc[:WINDOW].astype(BF16)
            oh_ref[q_rows, g0 + CHUNK:g0 + 2 * CHUNK] = acc[WINDOW:].astype(BF16)
            if kv + 1 < N_KV_HEADS:
                v4, sc = v4_next, sc_next
        return carry

    lax.fori_loop(0, nblk, block, 0)
    o_ref[...] = x_ref[...] + _dot(oh_ref[...], wo_ref[...])


def _attn_main(sinks, q, kd, vd, km, vm, x, wo, layer, *, tq):
    nt = SEQ // tq
    row = lambda b, t: (b * nt + t, 0)
    batch = lambda b, t: (b, 0, 0)
    single = pl.Buffered(1)
    return pl.pallas_call(
        functools.partial(_attn_main_body, nblk=tq // WINDOW),
        grid=(BATCH, nt),
        in_specs=[
            pl.BlockSpec(memory_space=pltpu.SMEM),
            pl.BlockSpec((tq, D_MODEL), row),
            pl.BlockSpec((1, SEQ, 2 * KV_DIM), batch, pipeline_mode=single),
            pl.BlockSpec((1, SEQ, 2 * KV_DIM), batch, pipeline_mode=single),
            pl.BlockSpec((1, WINDOW, 2 * KV_DIM), batch),
            pl.BlockSpec((1, WINDOW, 2 * KV_DIM), batch),
            pl.BlockSpec((tq, D_MODEL), row),
            pl.BlockSpec((None, D_MODEL, D_MODEL), lambda b, t: (layer, 0, 0), pipeline_mode=single),
        ],
        out_specs=pl.BlockSpec((tq, D_MODEL), row),
        out_shape=jax.ShapeDtypeStruct((N_MAIN, D_MODEL), F32),
        scratch_shapes=[pltpu.VMEM((tq, D_MODEL), BF16)],
        compiler_params=pltpu.CompilerParams(
            dimension_semantics=("arbitrary", "arbitrary"), vmem_limit_bytes=VMEM_LIMIT),
        name="attn_main",
    )(sinks, q, kd, vd, km, vm, x, wo)


def _attn_small_body(sink_ref, q_ref, k_ref, v_ref, ck_ref, cv_ref, x_ref, wo_ref,
                     o_ref, nk_ref, nv_ref, oh_ref):
    nm = BATCH * N_META
    r = lax.broadcasted_iota(jnp.int32, (nm, nm), 0)
    c = lax.broadcasted_iota(jnp.int32, (nm, nm), 1)
    mmask = (c <= r) & ((r < N_META) == (c < N_META))
    km = k_ref[0:nm, :].astype(BF16)
    vm = v_ref[0:nm, :].astype(BF16)
    for h in range(N_HEADS):
        kv = h // GROUP
        hs = slice(h * HEAD_DIM, (h + 1) * HEAD_DIM)
        ks = slice(kv * HEAD_DIM, (kv + 1) * HEAD_DIM)
        s = jnp.where(mmask, _dot_nt(q_ref[0:nm, hs], km[:, ks]), NEG)
        oh_ref[0:nm, hs] = _softmax_av(s, sink_ref[h], vm[:, ks]).astype(BF16)

    nq = GROUP * DEC_BATCH
    nkeys = DEC_BATCH * WINDOW
    r = lax.broadcasted_iota(jnp.int32, (nq, nkeys), 0)
    c = lax.broadcasted_iota(jnp.int32, (nq, nkeys), 1)
    smask = ((c // WINDOW) == (r % DEC_BATCH)) & ((c % WINDOW) >= 1)
    for kv in range(N_KV_HEADS):
        ks = slice(kv * HEAD_DIM, (kv + 1) * HEAD_DIM)
        heads = range(kv * GROUP, (kv + 1) * GROUP)
        qs = jnp.concatenate(
            [q_ref[nm:, h * HEAD_DIM:(h + 1) * HEAD_DIM] for h in heads], axis=0)
        sink = jnp.concatenate(
            [jnp.full((DEC_BATCH, 1), sink_ref[h], F32) for h in heads], axis=0)
        k_new = jnp.tile(k_ref[nm:, ks], (GROUP, 1))
        v_new = jnp.tile(v_ref[nm:, ks], (GROUP, 1))
        s = jnp.where(smask, _dot_nt(qs, ck_ref[:, ks].astype(BF16)), NEG)
        s_new = jnp.sum(qs.astype(F32) * k_new, axis=-1, keepdims=True)
        mx = jnp.maximum(jnp.maximum(jnp.max(s, axis=-1, keepdims=True), s_new), sink)
        p = jnp.exp(s - mx)
        p_new = jnp.exp(s_new - mx)
        den = jnp.sum(p, axis=-1, keepdims=True) + p_new + jnp.exp(sink - mx)
        o = (_dot(p.astype(BF16), cv_ref[:, ks].astype(BF16)) + p_new * v_new) / den
        for gi, h in enumerate(heads):
            oh_ref[nm:, h * HEAD_DIM:(h + 1) * HEAD_DIM] = (
                o[gi * DEC_BATCH:(gi + 1) * DEC_BATCH].astype(BF16))

    o_ref[...] = x_ref[...] + _dot(oh_ref[...], wo_ref[...])

    last = lax.broadcasted_iota(jnp.int32, (WINDOW, 1), 0) == WINDOW - 1

    def shift(b, carry):
        rows = pl.ds(pl.multiple_of(b * WINDOW, WINDOW), WINDOW)
        for cache, new, out in ((ck_ref, k_ref, nk_ref), (cv_ref, v_ref, nv_ref)):
            rolled = pltpu.roll(cache[rows, :], WINDOW - 1, 0)
            out[rows, :] = jnp.where(last, new[pl.ds(nm + b, 1), :], rolled)
        return carry

    lax.fori_loop(0, DEC_BATCH, shift, 0)


def _attn_small(sinks, q, k, v, ck, cv, x, wo, layer):
    single = pl.Buffered(1)

    def whole(shape):
        return pl.BlockSpec(shape, lambda i: (0,) * len(shape), pipeline_mode=single)

    return pl.pallas_call(
        _attn_small_body,
        grid=(1,),
        in_specs=[pl.BlockSpec(memory_space=pltpu.SMEM)]
        + [whole(a.shape) for a in (q, k, v, ck, cv, x)]
        + [pl.BlockSpec((None, D_MODEL, D_MODEL), lambda i: (layer, 0, 0), pipeline_mode=single)],
        out_specs=[
            whole((N_SMALL, D_MODEL)),
            whole((DEC_BATCH * WINDOW, KV_DIM)),
            whole((DEC_BATCH * WINDOW, KV_DIM)),
        ],
        out_shape=[
            jax.ShapeDtypeStruct((N_SMALL, D_MODEL), F32),
            jax.ShapeDtypeStruct((DEC_BATCH * WINDOW, KV_DIM), F32),
            jax.ShapeDtypeStruct((DEC_BATCH * WINDOW, KV_DIM), F32),
        ],
        scratch_shapes=[pltpu.VMEM((N_SMALL, D_MODEL), BF16)],
        compiler_params=pltpu.CompilerParams(vmem_limit_bytes=VMEM_LIMIT),
        name="attn_small",
    )(sinks, q, k, v, ck, cv, x, wo)


def _rope_tables(pos):
    half = ROT_DIM // 2
    n = pos.shape[0]
    inv = jnp.float32(ROPE_THETA) ** (-jnp.arange(half, dtype=F32) * 2.0 / ROT_DIM)
    ang = pos.astype(F32)[:, None] * inv[None, :]
    cos, sin = jnp.cos(ang), jnp.sin(ang)
    rest = HEAD_DIM - ROT_DIM
    cos64 = jnp.concatenate([cos, cos, jnp.ones((n, rest), F32)], axis=1)
    sa64 = jnp.concatenate([-sin, jnp.zeros((n, half + rest), F32)], axis=1)
    sb64 = jnp.concatenate([jnp.zeros((n, half), F32), sin, jnp.zeros((n, rest), F32)], axis=1)
    rep = LANES // HEAD_DIM
    return tuple(jnp.tile(t, (1, rep)) for t in (cos64, sa64, sb64))


def kernel(x_prompt, x_sample, state_pool, cache_k, cache_v, meta_tokens, norm_mix, norm_ffn,
           pool_w, pool_scale, w_qkv, w_o, q_norm, k_norm, sinks, w_gate, w_up, w_down):
    assert x_prompt.shape == (BATCH, SEQ, D_MODEL) and x_sample.shape == (DEC_BATCH, 1, D_MODEL)
    nm = BATCH * N_META
    meta = meta_tokens.astype(F32)
    xm = x_prompt.reshape(N_MAIN, D_MODEL)
    xs = jnp.concatenate([meta, meta, x_sample.reshape(DEC_BATCH, D_MODEL)], axis=0)

    tabs_main = _rope_tables(N_META + jnp.arange(SEQ))
    tabs_small = _rope_tables(jnp.concatenate(
        [jnp.arange(N_META), jnp.arange(N_META), jnp.full((DEC_BATCH,), PAST_LEN)]))
    head_of_col = jnp.arange(QKV_CHUNK) // HEAD_DIM
    ones = (head_of_col[:, None] == head_of_col[None, :]).astype(BF16)

    wqkv = w_qkv.astype(BF16)
    wo = w_o.astype(BF16)

    pool_p, pool_s, kp_l, vp_l, ks_l, vs_l = [], [], [], [], [], []
    for i in range(DEPTH):
        j = i // 2
        g_mix = norm_mix[i].reshape(1, D_MODEL)
        if i % 2 == 0:
            w = pool_w[j].astype(BF16)
            sc = pool_scale[j].reshape(1, D_MODEL)
            st = jnp.swapaxes(state_pool[j], 0, 1)
            xm_new, h_last = _pool_main(xm, xs, g_mix, w, sc, tp=512)
            xs, ns = _pool_small(xs, st, g_mix, w, sc)
            xm = xm_new
            pool_p.append(h_last[:, N_META - POOL_STATE:])
            pool_s.append(jnp.swapaxes(ns, 0, 1))
        else:
            gain = jnp.concatenate(
                [jnp.tile(q_norm[j], N_HEADS), jnp.tile(k_norm[j], N_KV_HEADS)]).reshape(1, QK_DIM)
            q_m, k_m, v_m, kd_m, vd_m = _qkv(xm, g_mix, wqkv, j, gain, tabs_main, ones, tm=512)
            q_s, k_s, v_s, kd_s, vd_s = _qkv(xs, g_mix, wqkv, j, gain, tabs_small, ones, tm=N_SMALL)

            def meta_past(t):
                t = t[:nm].reshape(BATCH, N_META, 2 * KV_DIM)
                return jnp.pad(t, ((0, 0), (WINDOW - N_META, 0), (0, 0)))

            ck = cache_k[j].reshape(DEC_BATCH * WINDOW, KV_DIM)
            cv = cache_v[j].reshape(DEC_BATCH * WINDOW, KV_DIM)
            xm = _attn_main(sinks[j], q_m, kd_m.reshape(BATCH, SEQ, 2 * KV_DIM),
                            vd_m.reshape(BATCH, SEQ, 2 * KV_DIM), meta_past(kd_s), meta_past(vd_s),
                            xm, wo, j, tq=512)
            xs, nk, nv = _attn_small(sinks[j], q_s, k_s, v_s, ck, cv, xs, wo, j)
            last = lambda t: t.reshape(BATCH, SEQ, KV_DIM)[:, -WINDOW:].reshape(
                BATCH, WINDOW, N_KV_HEADS, HEAD_DIM)
            kp_l.append(last(k_m))
            vp_l.append(last(v_m))
            ks_l.append(nk.reshape(DEC_BATCH, WINDOW, N_KV_HEADS, HEAD_DIM))
            vs_l.append(nv.reshape(DEC_BATCH, WINDOW, N_KV_HEADS, HEAD_DIM))
        g_ffn = norm_ffn[i].reshape(1, D_MODEL)
        xm, xs = _ffn(xm, xs, g_ffn, w_gate, w_up, w_down, i, tm=1024, tf=512)

    return (xm.reshape(BATCH, SEQ, D_MODEL), xs[nm:].reshape(DEC_BATCH, 1, D_MODEL),
            jnp.stack(pool_p), jnp.stack(kp_l), jnp.stack(vp_l),
            jnp.stack(pool_s), jnp.stack(ks_l), jnp.stack(vs_l))
```

```python
import functools

import jax
import jax.numpy as jnp
from jax import lax
from jax.experimental import pallas as pl
from jax.experimental.pallas import tpu as pltpu

F32 = jnp.float32
BF16 = jnp.bfloat16

D_MODEL = 2048
BATCH = 2
SEQ = 4096
DEPTH = 4
DEC_BATCH = 32
PAST_LEN = 16384
N_META = 16
POOL_WINDOWS = (2, 4, 8, 16)
N_POOL_GROUPS = len(POOL_WINDOWS)
POOL_GROUP = D_MODEL // N_POOL_GROUPS
POOL_STATE = max(POOL_WINDOWS) - 1
HEAD_DIM = 64
N_HEADS = D_MODEL // HEAD_DIM
N_KV_HEADS = 4
GROUP = N_HEADS // N_KV_HEADS
WINDOW = 128
ROT_DIM = HEAD_DIM // 4
ROPE_THETA = 500000.0
D_FF = 5632
EPS = 1e-6
NEG = -1e30

N_MAIN = BATCH * SEQ
N_SMALL = BATCH * N_META + DEC_BATCH
KV_DIM = N_KV_HEADS * HEAD_DIM
QK_DIM = D_MODEL + KV_DIM
QKV_DIM = D_MODEL + 2 * KV_DIM
LANES = 128

VMEM_LIMIT = 60 * 1024 * 1024


def _rms(x, g):
    ms = jnp.mean(x * x, axis=-1, keepdims=True)
    return x * lax.rsqrt(ms + EPS) * g


def _dot(a, b):
    return jnp.dot(a, b, preferred_element_type=F32)


def _dot_nt(a, b):
    return lax.dot_general(a, b, (((1,), (1,)), ((), ())), preferred_element_type=F32)


def _split_bf16(x):
    hi = x.astype(BF16)
    lo = (x - hi.astype(F32)).astype(BF16)
    return hi, lo


def _ffn_body(x_ref, xs_ref, g_ref, wg_ref, wu_ref, wd_ref, o_ref, os_ref, h_ref, hs_ref):
    i, j = pl.program_id(0), pl.program_id(1)

    def start(src_ref, dst_ref, hid_ref):
        x = src_ref[...]
        hid_ref[...] = _rms(x, g_ref[...]).astype(BF16)
        dst_ref[...] = x

    def accumulate(hid_ref, dst_ref):
        h = hid_ref[...]
        gate = _dot(h, wg_ref[...].astype(BF16))
        up = _dot(h, wu_ref[...].astype(BF16))
        act = (gate * jax.nn.sigmoid(gate) * up).astype(BF16)
        dst_ref[...] += _dot(act, wd_ref[...].astype(BF16))

    pl.when(j == 0)(lambda: start(x_ref, o_ref, h_ref))
    pl.when((i == 0) & (j == 0))(lambda: start(xs_ref, os_ref, hs_ref))
    accumulate(h_ref, o_ref)
    pl.when(i == 0)(lambda: accumulate(hs_ref, os_ref))


def _ffn(x, xs, g, wg, wu, wd, layer, *, tm, tf):
    m, ms = x.shape[0], xs.shape[0]
    single = pl.Buffered(1)
    return pl.pallas_call(
        _ffn_body,
        grid=(m // tm, D_FF // tf),
        in_specs=[
            pl.BlockSpec((tm, D_MODEL), lambda i, j: (i, 0)),
            pl.BlockSpec((ms, D_MODEL), lambda i, j: (0, 0)),
            pl.BlockSpec((1, D_MODEL), lambda i, j: (0, 0)),
            pl.BlockSpec((None, D_MODEL, tf), lambda i, j: (layer, 0, j)),
            pl.BlockSpec((None, D_MODEL, tf), lambda i, j: (layer, 0, j)),
            pl.BlockSpec((None, tf, D_MODEL), lambda i, j: (layer, j, 0)),
        ],
        out_specs=[
            pl.BlockSpec((tm, D_MODEL), lambda i, j: (i, 0), pipeline_mode=single),
            pl.BlockSpec((ms, D_MODEL), lambda i, j: (0, 0)),
        ],
        out_shape=[
            jax.ShapeDtypeStruct((m, D_MODEL), F32),
            jax.ShapeDtypeStruct((ms, D_MODEL), F32),
        ],
        scratch_shapes=[pltpu.VMEM((tm, D_MODEL), BF16), pltpu.VMEM((ms, D_MODEL), BF16)],
        compiler_params=pltpu.CompilerParams(
            dimension_semantics=("arbitrary", "arbitrary"), vmem_limit_bytes=VMEM_LIMIT),
        name="ffn",
    )(x, xs, g, wg, wu, wd)


def _pool_project(diffs, w_ref, sc_ref):
    ys = [_dot(d.astype(BF16), w_ref[g]) for g, d in enumerate(diffs)]
    return jnp.concatenate(ys, axis=1) * sc_ref[...]


def _pool_main_body(x_ref, meta_ref, g_ref, w_ref, sc_ref, o_ref, hl_ref, ext_ref, *, tp):
    t = pl.program_id(1)

    @pl.when(t == 0)
    def _():
        ext_ref[0:N_META, :] = _rms(meta_ref[...], g_ref[...])

    x = x_ref[...]
    h = _rms(x, g_ref[...])
    ext_ref[N_META:N_META + tp, :] = h
    diffs = []
    for g, w in enumerate(POOL_WINDOWS):
        c0, c1 = g * POOL_GROUP, (g + 1) * POOL_GROUP
        s = ext_ref[:, c0:c1]
        shift = 1
        while shift < w:
            s = s + pltpu.roll(s, shift, 0)
            shift *= 2
        diffs.append(s[N_META:] * (1.0 / w) - h[:, c0:c1])
    o_ref[...] = x + _pool_project(diffs, w_ref, sc_ref)
    tail = ext_ref[tp:tp + N_META, :]
    ext_ref[0:N_META, :] = tail

    @pl.when(t == pl.num_programs(1) - 1)
    def _():
        hl_ref[0] = tail


def _pool_main(xm, xs, g, w, sc, *, tp):
    nt = SEQ // tp
    return pl.pallas_call(
        functools.partial(_pool_main_body, tp=tp),
        grid=(BATCH, nt),
        in_specs=[
            pl.BlockSpec((tp, D_MODEL), lambda b, t: (b * nt + t, 0)),
            pl.BlockSpec((N_META, D_MODEL), lambda b, t: (b, 0)),
            pl.BlockSpec((1, D_MODEL), lambda b, t: (0, 0)),
            pl.BlockSpec((N_POOL_GROUPS, POOL_GROUP, POOL_GROUP), lambda b, t: (0, 0, 0)),
            pl.BlockSpec((1, D_MODEL), lambda b, t: (0, 0)),
        ],
        out_specs=[
            pl.BlockSpec((tp, D_MODEL), lambda b, t: (b * nt + t, 0)),
            pl.BlockSpec((1, N_META, D_MODEL), lambda b, t: (b, 0, 0)),
        ],
        out_shape=[
            jax.ShapeDtypeStruct((N_MAIN, D_MODEL), F32),
            jax.ShapeDtypeStruct((BATCH, N_META, D_MODEL), F32),
        ],
        scratch_shapes=[pltpu.VMEM((tp + N_META, D_MODEL), F32)],
        compiler_params=pltpu.CompilerParams(
            dimension_semantics=("arbitrary", "arbitrary"), vmem_limit_bytes=VMEM_LIMIT),
        name="pool_main",
    )(xm, xs, g, w, sc)


def _pool_small_body(x_ref, st_ref, g_ref, w_ref, sc_ref, o_ref, ns_ref, ext_ref):
    nm = BATCH * N_META
    x = x_ref[...]
    h = _rms(x, g_ref[...])
    for b in range(BATCH):
        ext_ref[b, 0:N_META, :] = jnp.zeros((N_META, D_MODEL), F32)
        ext_ref[b, N_META:2 * N_META, :] = h[b * N_META:(b + 1) * N_META]
    hs = h[nm:]
    row = lax.broadcasted_iota(jnp.int32, (N_META, 1), 0)
    diffs = []
    for g, w in enumerate(POOL_WINDOWS):
        c0, c1 = g * POOL_GROUP, (g + 1) * POOL_GROUP
        cnt = jnp.minimum(row + 1, w).astype(F32)
        parts = []
        for b in range(BATCH):
            s = ext_ref[b, N_META:2 * N_META, c0:c1]
            for k in range(1, w):
                s = s + ext_ref[b, N_META - k:2 * N_META - k, c0:c1]
            parts.append(s / cnt)
        s = hs[:, c0:c1]
        for k in range(1, w):
            s = s + st_ref[POOL_STATE - k, :, c0:c1]
        parts.append(s / float(w))
        diffs.append(jnp.concatenate(parts, axis=0) - h[:, c0:c1])
    o_ref[...] = x + _pool_project(diffs, w_ref, sc_ref)
    ns_ref[0:POOL_STATE - 1] = st_ref[1:POOL_STATE]
    ns_ref[POOL_STATE - 1] = hs


def _pool_small(xs, st, g, w, sc):
    return pl.pallas_call(
        _pool_small_body,
        out_shape=[
            jax.ShapeDtypeStruct((N_SMALL, D_MODEL), F32),
            jax.ShapeDtypeStruct((POOL_STATE, DEC_BATCH, D_MODEL), F32),
        ],
        scratch_shapes=[pltpu.VMEM((BATCH, 2 * N_META, D_MODEL), F32)],
        compiler_params=pltpu.CompilerParams(vmem_limit_bytes=VMEM_LIMIT),
        name="pool_small",
    )(xs, st, g, w, sc)


QKV_CHUNK = 4 * HEAD_DIM


def _qkv_body(x_ref, g_ref, w_ref, gain_ref, cos_ref, sa_ref, sb_ref, ones_ref,
              q_ref, k_ref, v_ref, kd_ref, vd_ref, h_ref):
    h_ref[...] = _rms(x_ref[...], g_ref[...]).astype(BF16)
    cos, sa, sb = cos_ref[...], sa_ref[...], sb_ref[...]
    half = ROT_DIM // 2

    def norm_rope(a, c0):
        sq_hi, sq_lo = _split_bf16(a * a)
        ss = _dot(sq_hi, ones_ref[...]) + _dot(sq_lo, ones_ref[...])
        y = a * lax.rsqrt(ss * (1.0 / HEAD_DIM) + EPS) * gain_ref[:, c0:c0 + QKV_CHUNK]
        tiles = []
        for u in range(QKV_CHUNK // LANES):
            yu = y[:, u * LANES:(u + 1) * LANES]
            tiles.append(yu * cos + pltpu.roll(yu, LANES - half, 1) * sa + pltpu.roll(yu, half, 1) * sb)
        return jnp.concatenate(tiles, axis=1)

    lower = lax.broadcasted_iota(jnp.int32, (1, LANES), 1) < HEAD_DIM

    def twice(t):
        tiles = []
        for u in range(KV_DIM // LANES):
            tu = t[:, u * LANES:(u + 1) * LANES]
            swapped = pltpu.roll(tu, HEAD_DIM, 1)
            tiles += [jnp.where(lower, tu, swapped), jnp.where(lower, swapped, tu)]
        return jnp.concatenate(tiles, axis=1).astype(BF16)

    wide = 2 * QKV_CHUNK
    nwide = QKV_DIM // wide
    project = lambda c: _dot(h_ref[...], w_ref[:, c * wide:(c + 1) * wide])
    a = project(0)
    for c in range(nwide):
        a_next = project(c + 1) if c + 1 < nwide else None
        c0 = c * wide
        if c0 < D_MODEL:
            for u in range(2):
                y = norm_rope(a[:, u * QKV_CHUNK:(u + 1) * QKV_CHUNK], c0 + u * QKV_CHUNK)
                q_ref[:, c0 + u * QKV_CHUNK:c0 + (u + 1) * QKV_CHUNK] = (
                    y * HEAD_DIM ** -0.5).astype(BF16)
        else:
            k = norm_rope(a[:, :KV_DIM], D_MODEL)
            v = a[:, KV_DIM:]
            k_ref[...] = k
            v_ref[...] = v
            kd_ref[...] = twice(k)
            vd_ref[...] = twice(v)
        a = a_next


def _qkv(x, g, w, layer, gain, tabs, ones, *, tm):
    m = x.shape[0]
    nt = tabs[0].shape[0] // tm
    row = lambda i: (i, 0)
    tab = lambda i: (i % nt, 0)
    fixed = lambda i: (0, 0)
    return pl.pallas_call(
        _qkv_body,
        grid=(m // tm,),
        in_specs=[
            pl.BlockSpec((tm, D_MODEL), row),
            pl.BlockSpec((1, D_MODEL), fixed),
            pl.BlockSpec((None, D_MODEL, QKV_DIM), lambda i: (layer, 0, 0)),
            pl.BlockSpec((1, QK_DIM), fixed),
            pl.BlockSpec((tm, LANES), tab),
            pl.BlockSpec((tm, LANES), tab),
            pl.BlockSpec((tm, LANES), tab),
            pl.BlockSpec((QKV_CHUNK, QKV_CHUNK), fixed),
        ],
        out_specs=[
            pl.BlockSpec((tm, D_MODEL), row),
            pl.BlockSpec((tm, KV_DIM), row),
            pl.BlockSpec((tm, KV_DIM), row),
            pl.BlockSpec((tm, 2 * KV_DIM), row),
            pl.BlockSpec((tm, 2 * KV_DIM), row),
        ],
        out_shape=[
            jax.ShapeDtypeStruct((m, D_MODEL), BF16),
            jax.ShapeDtypeStruct((m, KV_DIM), F32),
            jax.ShapeDtypeStruct((m, KV_DIM), F32),
            jax.ShapeDtypeStruct((m, 2 * KV_DIM), BF16),
            jax.ShapeDtypeStruct((m, 2 * KV_DIM), BF16),
        ],
        scratch_shapes=[pltpu.VMEM((tm, D_MODEL), BF16)],
        compiler_params=pltpu.CompilerParams(
            dimension_semantics=("arbitrary",), vmem_limit_bytes=VMEM_LIMIT),
        name="qkv",
    )(x, g, w, gain, *tabs, ones)


def _softmax_av(s, sink, v):
    mx = jnp.maximum(jnp.max(s, axis=-1, keepdims=True), sink)
    p = jnp.exp(s - mx)
    den = jnp.sum(p, axis=-1, keepdims=True) + jnp.exp(sink - mx)
    return _dot(p.astype(BF16), v) / den


SLOTS = 4
CHUNK = SLOTS * HEAD_DIM


def _attn_main_body(sink_ref, q_ref, k_ref, v_ref, km_ref, vm_ref, x_ref, wo_ref, o_ref, oh_ref,
                    *, nblk):
    rows2 = 2 * WINDOW
    r = lax.broadcasted_iota(jnp.int32, (rows2, WINDOW), 0) % WINDOW
    c = lax.broadcasted_iota(jnp.int32, (rows2, WINDOW), 1)
    prev = c > r
    top = lax.broadcasted_iota(jnp.int32, (rows2, 1), 0) < WINDOW
    lane_slot = lax.broadcasted_iota(jnp.int32, (rows2, CHUNK), 1) // HEAD_DIM
    slot_mask = [(lane_slot == s).astype(F32).astype(BF16) for s in range(SLOTS)]
    zero = jnp.zeros((rows2, WINDOW), F32)
    step = pl.program_id(1)
    seen = jnp.logical_not(prev) | (c >= WINDOW - N_META)

    def block(i, carry):
        row0 = pl.multiple_of(i * WINDOW, WINDOW)
        cur0 = pl.multiple_of(step * (nblk * WINDOW) + row0, WINDOW)
        past0 = pl.multiple_of(jnp.maximum(cur0 - WINDOW, 0), WINDOW)
        opens = cur0 == 0
        visible = seen | jnp.logical_not(opens)
        q_rows = pl.ds(row0, WINDOW)

        def keys_values(kv):
            lanes = slice(kv * LANES, (kv + 1) * LANES)
            k_past = jnp.where(opens, km_ref[0, :, lanes], k_ref[0, pl.ds(past0, WINDOW), lanes])
            v_past = jnp.where(opens, vm_ref[0, :, lanes], v_ref[0, pl.ds(past0, WINDOW), lanes])
            kd = jnp.concatenate([k_past, k_ref[0, pl.ds(cur0, WINDOW), lanes]], axis=0)
            vd = jnp.concatenate([v_past, v_ref[0, pl.ds(cur0, WINDOW), lanes]], axis=0)
            return jnp.concatenate([kd, kd], axis=1), jnp.concatenate([vd, vd], axis=1)

        def scores(kv, k4):
            g0 = kv * GROUP * HEAD_DIM
            qst = jnp.concatenate(
                [q_ref[q_rows, g0:g0 + CHUNK], q_ref[q_rows, g0 + CHUNK:g0 + 2 * CHUNK]], axis=0)
            return [_dot_nt(qst, k4 * slot_mask[s]) for s in range(SLOTS)]

        def weights(kv, s, sc):
            sf = jnp.where(prev, sc[:, :WINDOW], sc[:, WINDOW:])
            sf = jnp.where(visible, sf, NEG)
            sink = jnp.where(top, sink_ref[kv * GROUP + s], sink_ref[kv * GROUP + SLOTS + s])
            mx = jnp.maximum(jnp.max(sf, axis=-1, keepdims=True), sink)
            p = jnp.exp(sf - mx)
            den = jnp.sum(p, axis=-1, keepdims=True) + jnp.exp(sink - mx)
            p = p * (1.0 / den)
            p2 = jnp.concatenate([jnp.where(prev, p, zero), jnp.where(prev, zero, p)], axis=1)
            return p2.astype(BF16)

        def outputs(kv, ps, v4):
            acc = jnp.zeros((rows2, CHUNK), F32)
            for s in range(SLOTS):
                acc = acc + _dot(ps[s], v4 * slot_mask[s])
            g0 = kv * GROUP * HEAD_DIM
            oh_ref[q_rows, g0:g0 + CHUNK] = acc[:WINDOW].astype(BF16)
            oh_ref[q_rows, g0 + CHUNK:g0 + 2 * CHUNK] = acc[WINDOW:].astype(BF16)

        kvs = [keys_values(kv) for kv in range(N_KV_HEADS)]
        sc = scores(0, kvs[0][0])
        ps = None
        for kv in range(N_KV_HEADS):
            sc_next = scores(kv + 1, kvs[kv + 1][0]) if kv + 1 < N_KV_HEADS else None
            if ps is not None:
                outputs(kv - 1, ps, kvs[kv - 1][1])
            ps = [weights(kv, s, sc[s]) for s in range(SLOTS)]
            sc = sc_next
        outputs(N_KV_HEADS - 1, ps, kvs[N_KV_HEADS - 1][1])
        return carry

    lax.fori_loop(0, nblk, block, 0)
    o_ref[...] = x_ref[...] + _dot(oh_ref[...], wo_ref[...])


def _attn_main(sinks, q, kd, vd, km, vm, x, wo, layer, *, tq):
    nt = SEQ // tq
    row = lambda b, t: (b * nt + t, 0)
    batch = lambda b, t: (b, 0, 0)
    single = pl.Buffered(1)
    return pl.pallas_call(
        functools.partial(_attn_main_body, nblk=tq // WINDOW),
        grid=(BATCH, nt),
        in_specs=[
            pl.BlockSpec(memory_space=pltpu.SMEM),
            pl.BlockSpec((tq, D_MODEL), row),
            pl.BlockSpec((1, SEQ, 2 * KV_DIM), batch, pipeline_mode=single),
            pl.BlockSpec((1, SEQ, 2 * KV_DIM), batch, pipeline_mode=single),
            pl.BlockSpec((1, WINDOW, 2 * KV_DIM), batch),
            pl.BlockSpec((1, WINDOW, 2 * KV_DIM), batch),
            pl.BlockSpec((tq, D_MODEL), row),
            pl.BlockSpec((None, D_MODEL, D_MODEL), lambda b, t: (layer, 0, 0), pipeline_mode=single),
        ],
        out_specs=pl.BlockSpec((tq, D_MODEL), row),
        out_shape=jax.ShapeDtypeStruct((N_MAIN, D_MODEL), F32),
        scratch_shapes=[pltpu.VMEM((tq, D_MODEL), BF16)],
        compiler_params=pltpu.CompilerParams(
            dimension_semantics=("arbitrary", "arbitrary"), vmem_limit_bytes=VMEM_LIMIT),
        name="attn_main",
    )(sinks, q, kd, vd, km, vm, x, wo)


def _attn_small_body(sink_ref, q_ref, k_ref, v_ref, ck_ref, cv_ref, kn_ref, vn_ref, x_ref, wo_ref,
                     o_ref, nk_ref, nv_ref, oh_ref):
    nm = BATCH * N_META
    r = lax.broadcasted_iota(jnp.int32, (nm, nm), 0)
    c = lax.broadcasted_iota(jnp.int32, (nm, nm), 1)
    mmask = (c <= r) & ((r < N_META) == (c < N_META))
    km = k_ref[0:nm, :].astype(BF16)
    vm = v_ref[0:nm, :].astype(BF16)
    for h in range(N_HEADS):
        kv = h // GROUP
        hs = slice(h * HEAD_DIM, (h + 1) * HEAD_DIM)
        ks = slice(kv * HEAD_DIM, (kv + 1) * HEAD_DIM)
        s = jnp.where(mmask, _dot_nt(q_ref[0:nm, hs], km[:, ks]), NEG)
        oh_ref[0:nm, hs] = _softmax_av(s, sink_ref[h], vm[:, ks]).astype(BF16)

    nq = GROUP * DEC_BATCH
    wide = DEC_BATCH * HEAD_DIM
    own = (lax.broadcasted_iota(jnp.int32, (nq, wide), 0) % DEC_BATCH
           == lax.broadcasted_iota(jnp.int32, (nq, wide), 1) // HEAD_DIM).astype(F32)
    live = lax.broadcasted_iota(jnp.int32, (nq, WINDOW), 1) >= 1
    lower = lax.broadcasted_iota(jnp.int32, (1, LANES), 1) < HEAD_DIM
    for kv in range(N_KV_HEADS):
        ks = slice(kv * HEAD_DIM, (kv + 1) * HEAD_DIM)
        heads = range(kv * GROUP, (kv + 1) * GROUP)
        kts = ck_ref[:, kv].reshape(wide, WINDOW).astype(BF16)
        vts = cv_ref[:, kv].reshape(wide, WINDOW).astype(BF16)
        twice = []
        for h in heads[::2]:
            pair = q_ref[nm:, h * HEAD_DIM:(h + 2) * HEAD_DIM].astype(F32)
            swapped = pltpu.roll(pair, HEAD_DIM, 1)
            twice += [jnp.where(lower, pair, swapped), jnp.where(lower, swapped, pair)]
        qd = jnp.concatenate(twice, axis=0)
        qs = qd[:, :HEAD_DIM]
        qblk = (jnp.tile(qd, (1, wide // LANES)) * own).astype(BF16)
        sink = jnp.concatenate(
            [jnp.full((DEC_BATCH, 1), sink_ref[h], F32) for h in heads], axis=0)
        k_new = jnp.tile(k_ref[nm:, ks], (GROUP, 1))
        v_new = jnp.tile(v_ref[nm:, ks], (GROUP, 1))
        s = jnp.where(live, _dot(qblk, kts), NEG)
        s_new = jnp.sum(qs * k_new, axis=-1, keepdims=True)
        mx = jnp.maximum(jnp.maximum(jnp.max(s, axis=-1, keepdims=True), s_new), sink)
        p = jnp.exp(s - mx)
        p_new = jnp.exp(s_new - mx)
        den = jnp.sum(p, axis=-1, keepdims=True) + p_new + jnp.exp(sink - mx)
        spread = _dot_nt(p.astype(BF16), vts) * own
        folded = spread[:, :LANES]
        for t in range(1, wide // LANES):
            folded = folded + spread[:, t * LANES:(t + 1) * LANES]
        pv = (folded + pltpu.roll(folded, HEAD_DIM, 1))[:, :HEAD_DIM]
        o = (pv + p_new * v_new) / den
        for gi, h in enumerate(heads):
            oh_ref[nm:, h * HEAD_DIM:(h + 1) * HEAD_DIM] = (
                o[gi * DEC_BATCH:(gi + 1) * DEC_BATCH].astype(BF16))

    o_ref[...] = x_ref[...] + _dot(oh_ref[...], wo_ref[...])

    last = lax.broadcasted_iota(jnp.int32, (KV_DIM, WINDOW), 1) == WINDOW - 1

    def shift(b, carry):
        rows = pl.ds(pl.multiple_of(b * KV_DIM, KV_DIM), KV_DIM)
        for cache, new, out in ((ck_ref, kn_ref, nk_ref), (cv_ref, vn_ref, nv_ref)):
            rolled = pltpu.roll(cache[b].reshape(KV_DIM, WINDOW), WINDOW - 1, 1)
            out[b] = jnp.where(last, new[rows, :], rolled).reshape(N_KV_HEADS, HEAD_DIM, WINDOW)
        return carry

    lax.fori_loop(0, DEC_BATCH, shift, 0)


def _attn_small(sinks, q, k, v, ck, cv, kn, vn, x, wo, layer):
    single = pl.Buffered(1)
    cache_shape = (DEC_BATCH, N_KV_HEADS, HEAD_DIM, WINDOW)

    def whole(shape):
        return pl.BlockSpec(shape, lambda i: (0,) * len(shape), pipeline_mode=single)

    cache = pl.BlockSpec((None,) + cache_shape, lambda i: (layer, 0, 0, 0, 0), pipeline_mode=single)
    return pl.pallas_call(
        _attn_small_body,
        grid=(1,),
        in_specs=[pl.BlockSpec(memory_space=pltpu.SMEM)]
        + [whole(a.shape) for a in (q, k, v)] + [cache, cache] + [whole(a.shape) for a in (kn, vn, x)]
        + [pl.BlockSpec((None, D_MODEL, D_MODEL), lambda i: (layer, 0, 0), pipeline_mode=single)],
        out_specs=[whole((N_SMALL, D_MODEL)), whole(cache_shape), whole(cache_shape)],
        out_shape=[
            jax.ShapeDtypeStruct((N_SMALL, D_MODEL), F32),
            jax.ShapeDtypeStruct(cache_shape, F32),
            jax.ShapeDtypeStruct(cache_shape, F32),
        ],
        scratch_shapes=[pltpu.VMEM((N_SMALL, D_MODEL), BF16)],
        compiler_params=pltpu.CompilerParams(vmem_limit_bytes=VMEM_LIMIT),
        name="attn_small",
    )(sinks, q, k, v, ck, cv, kn, vn, x, wo)


def _rope_tables(pos):
    half = ROT_DIM // 2
    n = pos.shape[0]
    inv = jnp.float32(ROPE_THETA) ** (-jnp.arange(half, dtype=F32) * 2.0 / ROT_DIM)
    ang = pos.astype(F32)[:, None] * inv[None, :]
    cos, sin = jnp.cos(ang), jnp.sin(ang)
    rest = HEAD_DIM - ROT_DIM
    cos64 = jnp.concatenate([cos, cos, jnp.ones((n, rest), F32)], axis=1)
    sa64 = jnp.concatenate([-sin, jnp.zeros((n, half + rest), F32)], axis=1)
    sb64 = jnp.concatenate([jnp.zeros((n, half), F32), sin, jnp.zeros((n, rest), F32)], axis=1)
    rep = LANES // HEAD_DIM
    return tuple(jnp.tile(t, (1, rep)) for t in (cos64, sa64, sb64))


def kernel(x_prompt, x_sample, state_pool, cache_k, cache_v, meta_tokens, norm_mix, norm_ffn,
           pool_w, pool_scale, w_qkv, w_o, q_norm, k_norm, sinks, w_gate, w_up, w_down):
    assert x_prompt.shape == (BATCH, SEQ, D_MODEL) and x_sample.shape == (DEC_BATCH, 1, D_MODEL)
    nm = BATCH * N_META
    meta = meta_tokens.astype(F32)
    xm = x_prompt.reshape(N_MAIN, D_MODEL)
    xs = jnp.concatenate([meta, meta, x_sample.reshape(DEC_BATCH, D_MODEL)], axis=0)

    tabs_main = _rope_tables(N_META + jnp.arange(SEQ))
    tabs_small = _rope_tables(jnp.concatenate(
        [jnp.arange(N_META), jnp.arange(N_META), jnp.full((DEC_BATCH,), PAST_LEN)]))
    head_of_col = jnp.arange(QKV_CHUNK) // HEAD_DIM
    ones = (head_of_col[:, None] == head_of_col[None, :]).astype(BF16)

    wqkv = w_qkv.astype(BF16)
    wo = w_o.astype(BF16)
    ck = jnp.transpose(cache_k, (0, 1, 3, 4, 2))
    cv = jnp.transpose(cache_v, (0, 1, 3, 4, 2))

    pool_p, pool_s, kp_l, vp_l, ks_l, vs_l = [], [], [], [], [], []
    for i in range(DEPTH):
        j = i // 2
        g_mix = norm_mix[i].reshape(1, D_MODEL)
        if i % 2 == 0:
            w = pool_w[j].astype(BF16)
            sc = pool_scale[j].reshape(1, D_MODEL)
            st = jnp.swapaxes(state_pool[j], 0, 1)
            xm_new, h_last = _pool_main(xm, xs, g_mix, w, sc, tp=512)
            xs, ns = _pool_small(xs, st, g_mix, w, sc)
            xm = xm_new
            pool_p.append(h_last[:, N_META - POOL_STATE:])
            pool_s.append(jnp.swapaxes(ns, 0, 1))
        else:
            gain = jnp.concatenate(
                [jnp.tile(q_norm[j], N_HEADS), jnp.tile(k_norm[j], N_KV_HEADS)]).reshape(1, QK_DIM)
            q_m, k_m, v_m, kd_m, vd_m = _qkv(xm, g_mix, wqkv, j, gain, tabs_main, ones, tm=512)
            q_s, k_s, v_s, kd_s, vd_s = _qkv(xs, g_mix, wqkv, j, gain, tabs_small, ones, tm=N_SMALL)

            def meta_past(t):
                t = t[:nm].reshape(BATCH, N_META, 2 * KV_DIM)
                return jnp.pad(t, ((0, 0), (WINDOW - N_META, 0), (0, 0)))

            kn = k_s[nm:].reshape(DEC_BATCH * KV_DIM, 1)
            vn = v_s[nm:].reshape(DEC_BATCH * KV_DIM, 1)
            xm = _attn_main(sinks[j], q_m, kd_m.reshape(BATCH, SEQ, 2 * KV_DIM),
                            vd_m.reshape(BATCH, SEQ, 2 * KV_DIM), meta_past(kd_s), meta_past(vd_s),
                            xm, wo, j, tq=512)
            xs, nk, nv = _attn_small(sinks[j], q_s, k_s, v_s, ck, cv, kn, vn, xs, wo, j)
            last = lambda t: t.reshape(BATCH, SEQ, KV_DIM)[:, -WINDOW:].reshape(
                BATCH, WINDOW, N_KV_HEADS, HEAD_DIM)
            kp_l.append(last(k_m))
            vp_l.append(last(v_m))
            ks_l.append(jnp.transpose(nk, (0, 3, 1, 2)))
            vs_l.append(jnp.transpose(nv, (0, 3, 1, 2)))
        g_ffn = norm_ffn[i].reshape(1, D_MODEL)
        xm, xs = _ffn(xm, xs, g_ffn, w_gate, w_up, w_down, i, tm=1024, tf=512)

    return (xm.reshape(BATCH, SEQ, D_MODEL), xs[nm:].reshape(DEC_BATCH, 1, D_MODEL),
            jnp.stack(pool_p), jnp.stack(kp_l), jnp.stack(vp_l),
            jnp.stack(pool_s), jnp.stack(ks_l), jnp.stack(vs_l))
```

```python
import functools

import jax
import jax.numpy as jnp
from jax import lax
from jax.experimental import pallas as pl
from jax.experimental.pallas import tpu as pltpu

F32 = jnp.float32
BF16 = jnp.bfloat16

D_MODEL = 2048
BATCH = 2
SEQ = 4096
DEPTH = 4
DEC_BATCH = 32
PAST_LEN = 16384
N_META = 16
POOL_WINDOWS = (2, 4, 8, 16)
N_POOL_GROUPS = len(POOL_WINDOWS)
POOL_GROUP = D_MODEL // N_POOL_GROUPS
POOL_STATE = max(POOL_WINDOWS) - 1
HEAD_DIM = 64
N_HEADS = D_MODEL // HEAD_DIM
N_KV_HEADS = 4
GROUP = N_HEADS // N_KV_HEADS
WINDOW = 128
ROT_DIM = HEAD_DIM // 4
ROPE_THETA = 500000.0
D_FF = 5632
EPS = 1e-6
NEG = -1e30

N_MAIN = BATCH * SEQ
N_SMALL = BATCH * N_META + DEC_BATCH
KV_DIM = N_KV_HEADS * HEAD_DIM
QK_DIM = D_MODEL + KV_DIM
QKV_DIM = D_MODEL + 2 * KV_DIM
LANES = 128

VMEM_LIMIT = 60 * 1024 * 1024


def _rms(x, g):
    ms = jnp.mean(x * x, axis=-1, keepdims=True)
    return x * lax.rsqrt(ms + EPS) * g


def _dot(a, b):
    return jnp.dot(a, b, preferred_element_type=F32)


def _dot_nt(a, b):
    return lax.dot_general(a, b, (((1,), (1,)), ((), ())), preferred_element_type=F32)


def _split_bf16(x):
    hi = x.astype(BF16)
    lo = (x - hi.astype(F32)).astype(BF16)
    return hi, lo


def _ffn_body(x_ref, xs_ref, g_ref, wg_ref, wu_ref, wd_ref, o_ref, os_ref, h_ref):
    i, j = pl.program_id(0), pl.program_id(1)
    tm, ms = x_ref.shape[0], xs_ref.shape[0]

    def start(src_ref, dst_ref, row0):
        x = src_ref[...]
        h_ref[row0:row0 + x.shape[0], :] = _rms(x, g_ref[...]).astype(BF16)
        dst_ref[...] = x

    def accumulate(rows):
        h = h_ref[0:rows, :]
        gate = _dot(h, wg_ref[...].astype(BF16))
        up = _dot(h, wu_ref[...].astype(BF16))
        act = (gate * jax.nn.sigmoid(gate) * up).astype(BF16)
        y = _dot(act, wd_ref[...].astype(BF16))
        o_ref[...] += y[:tm]
        if rows > tm:
            os_ref[...] += y[tm:]

    pl.when(j == 0)(lambda: start(x_ref, o_ref, 0))
    pl.when((i == 0) & (j == 0))(lambda: start(xs_ref, os_ref, tm))
    pl.when(i == 0)(lambda: accumulate(tm + ms))
    pl.when(i > 0)(lambda: accumulate(tm))


def _ffn(x, xs, g, wg, wu, wd, layer, *, tm, tf):
    m, ms = x.shape[0], xs.shape[0]
    single = pl.Buffered(1)
    return pl.pallas_call(
        _ffn_body,
        grid=(m // tm, D_FF // tf),
        in_specs=[
            pl.BlockSpec((tm, D_MODEL), lambda i, j: (i, 0)),
            pl.BlockSpec((ms, D_MODEL), lambda i, j: (0, 0)),
            pl.BlockSpec((1, D_MODEL), lambda i, j: (0, 0)),
            pl.BlockSpec((None, D_MODEL, tf), lambda i, j: (layer, 0, j)),
            pl.BlockSpec((None, D_MODEL, tf), lambda i, j: (layer, 0, j)),
            pl.BlockSpec((None, tf, D_MODEL), lambda i, j: (layer, j, 0)),
        ],
        out_specs=[
            pl.BlockSpec((tm, D_MODEL), lambda i, j: (i, 0), pipeline_mode=single),
            pl.BlockSpec((ms, D_MODEL), lambda i, j: (0, 0)),
        ],
        out_shape=[
            jax.ShapeDtypeStruct((m, D_MODEL), F32),
            jax.ShapeDtypeStruct((ms, D_MODEL), F32),
        ],
        scratch_shapes=[pltpu.VMEM((tm + ms, D_MODEL), BF16)],
        compiler_params=pltpu.CompilerParams(
            dimension_semantics=("arbitrary", "arbitrary"), vmem_limit_bytes=VMEM_LIMIT),
        name="ffn",
    )(x, xs, g, wg, wu, wd)


def _pool_project(diffs, w_ref, sc_ref):
    ys = [_dot(d.astype(BF16), w_ref[g]) for g, d in enumerate(diffs)]
    return jnp.concatenate(ys, axis=1) * sc_ref[...]


def _pool_main_body(x_ref, meta_ref, g_ref, w_ref, sc_ref, o_ref, hl_ref, ext_ref, *, tp):
    t = pl.program_id(1)

    @pl.when(t == 0)
    def _():
        ext_ref[0:N_META, :] = _rms(meta_ref[...], g_ref[...])

    x = x_ref[...]
    h = _rms(x, g_ref[...])
    ext_ref[N_META:N_META + tp, :] = h
    diffs = []
    for g, w in enumerate(POOL_WINDOWS):
        c0, c1 = g * POOL_GROUP, (g + 1) * POOL_GROUP
        s = ext_ref[:, c0:c1]
        shift = 1
        while shift < w:
            s = s + pltpu.roll(s, shift, 0)
            shift *= 2
        diffs.append(s[N_META:] * (1.0 / w) - h[:, c0:c1])
    o_ref[...] = x + _pool_project(diffs, w_ref, sc_ref)
    tail = ext_ref[tp:tp + N_META, :]
    ext_ref[0:N_META, :] = tail

    @pl.when(t == pl.num_programs(1) - 1)
    def _():
        hl_ref[0] = tail


def _pool_main(xm, xs, g, w, sc, *, tp):
    nt = SEQ // tp
    return pl.pallas_call(
        functools.partial(_pool_main_body, tp=tp),
        grid=(BATCH, nt),
        in_specs=[
            pl.BlockSpec((tp, D_MODEL), lambda b, t: (b * nt + t, 0)),
            pl.BlockSpec((N_META, D_MODEL), lambda b, t: (b, 0)),
            pl.BlockSpec((1, D_MODEL), lambda b, t: (0, 0)),
            pl.BlockSpec((N_POOL_GROUPS, POOL_GROUP, POOL_GROUP), lambda b, t: (0, 0, 0)),
            pl.BlockSpec((1, D_MODEL), lambda b, t: (0, 0)),
        ],
        out_specs=[
            pl.BlockSpec((tp, D_MODEL), lambda b, t: (b * nt + t, 0)),
            pl.BlockSpec((1, N_META, D_MODEL), lambda b, t: (b, 0, 0)),
        ],
        out_shape=[
            jax.ShapeDtypeStruct((N_MAIN, D_MODEL), F32),
            jax.ShapeDtypeStruct((BATCH, N_META, D_MODEL), F32),
        ],
        scratch_shapes=[pltpu.VMEM((tp + N_META, D_MODEL), F32)],
        compiler_params=pltpu.CompilerParams(
            dimension_semantics=("arbitrary", "arbitrary"), vmem_limit_bytes=VMEM_LIMIT),
        name="pool_main",
    )(xm, xs, g, w, sc)


def _pool_small_body(x_ref, st_ref, g_ref, w_ref, sc_ref, o_ref, ns_ref, ext_ref):
    nm = BATCH * N_META
    x = x_ref[...]
    h = _rms(x, g_ref[...])
    for b in range(BATCH):
        ext_ref[b, 0:N_META, :] = jnp.zeros((N_META, D_MODEL), F32)
        ext_ref[b, N_META:2 * N_META, :] = h[b * N_META:(b + 1) * N_META]
    hs = h[nm:]
    row = lax.broadcasted_iota(jnp.int32, (N_META, 1), 0)
    diffs = []
    for g, w in enumerate(POOL_WINDOWS):
        c0, c1 = g * POOL_GROUP, (g + 1) * POOL_GROUP
        cnt = jnp.minimum(row + 1, w).astype(F32)
        parts = []
        for b in range(BATCH):
            s = ext_ref[b, N_META:2 * N_META, c0:c1]
            for k in range(1, w):
                s = s + ext_ref[b, N_META - k:2 * N_META - k, c0:c1]
            parts.append(s / cnt)
        s = hs[:, c0:c1]
        for k in range(1, w):
            s = s + st_ref[POOL_STATE - k, :, c0:c1]
        parts.append(s / float(w))
        diffs.append(jnp.concatenate(parts, axis=0) - h[:, c0:c1])
    o_ref[...] = x + _pool_project(diffs, w_ref, sc_ref)
    ns_ref[0:POOL_STATE - 1] = st_ref[1:POOL_STATE]
    ns_ref[POOL_STATE - 1] = hs


def _pool_small(xs, st, g, w, sc):
    return pl.pallas_call(
        _pool_small_body,
        out_shape=[
            jax.ShapeDtypeStruct((N_SMALL, D_MODEL), F32),
            jax.ShapeDtypeStruct((POOL_STATE, DEC_BATCH, D_MODEL), F32),
        ],
        scratch_shapes=[pltpu.VMEM((BATCH, 2 * N_META, D_MODEL), F32)],
        compiler_params=pltpu.CompilerParams(vmem_limit_bytes=VMEM_LIMIT),
        name="pool_small",
    )(xs, st, g, w, sc)


QKV_CHUNK = 4 * HEAD_DIM


def _qkv_rows(x_ref, tab_refs, g_ref, w_ref, gain_ref, ones_ref, out_refs, h_ref):
    cos_ref, sa_ref, sb_ref = tab_refs
    q_ref, k_ref, v_ref, kd_ref, vd_ref = out_refs
    h_ref[...] = _rms(x_ref[...], g_ref[...]).astype(BF16)
    cos, sa, sb = cos_ref[...], sa_ref[...], sb_ref[...]
    half = ROT_DIM // 2

    def norm_rope(a, c0):
        sq_hi, sq_lo = _split_bf16(a * a)
        ss = _dot(sq_hi, ones_ref[...]) + _dot(sq_lo, ones_ref[...])
        y = a * lax.rsqrt(ss * (1.0 / HEAD_DIM) + EPS) * gain_ref[:, c0:c0 + QKV_CHUNK]
        tiles = []
        for u in range(QKV_CHUNK // LANES):
            yu = y[:, u * LANES:(u + 1) * LANES]
            tiles.append(yu * cos + pltpu.roll(yu, LANES - half, 1) * sa + pltpu.roll(yu, half, 1) * sb)
        return jnp.concatenate(tiles, axis=1)

    lower = lax.broadcasted_iota(jnp.int32, (1, LANES), 1) < HEAD_DIM

    def twice(t):
        tiles = []
        for u in range(KV_DIM // LANES):
            tu = t[:, u * LANES:(u + 1) * LANES]
            swapped = pltpu.roll(tu, HEAD_DIM, 1)
            tiles += [jnp.where(lower, tu, swapped), jnp.where(lower, swapped, tu)]
        return jnp.concatenate(tiles, axis=1).astype(BF16)

    wide = 2 * QKV_CHUNK
    nwide = QKV_DIM // wide
    project = lambda c: _dot(h_ref[...], w_ref[:, c * wide:(c + 1) * wide])
    a = project(0)
    for c in range(nwide):
        a_next = project(c + 1) if c + 1 < nwide else None
        c0 = c * wide
        if c0 < D_MODEL:
            for u in range(2):
                y = norm_rope(a[:, u * QKV_CHUNK:(u + 1) * QKV_CHUNK], c0 + u * QKV_CHUNK)
                q_ref[:, c0 + u * QKV_CHUNK:c0 + (u + 1) * QKV_CHUNK] = (
                    y * HEAD_DIM ** -0.5).astype(BF16)
        else:
            k = norm_rope(a[:, :KV_DIM], D_MODEL)
            v = a[:, KV_DIM:]
            k_ref[...] = k
            v_ref[...] = v
            kd_ref[...] = twice(k)
            vd_ref[...] = twice(v)
        a = a_next


def _qkv_body(*refs):
    x_ref, xs_ref, g_ref, w_ref, gain_ref = refs[:5]
    tabs, tabs_s, ones_ref = refs[5:8], refs[8:11], refs[11]
    outs, outs_s = refs[12:17], refs[17:22]
    wb_ref, h_ref, hs_ref = refs[22:]

    @pl.when(pl.program_id(0) == 0)
    def _():
        wb_ref[...] = w_ref[...].astype(BF16)
        _qkv_rows(xs_ref, tabs_s, g_ref, wb_ref, gain_ref, ones_ref, outs_s, hs_ref)

    _qkv_rows(x_ref, tabs, g_ref, wb_ref, gain_ref, ones_ref, outs, h_ref)


def _qkv(x, xs, g, w, layer, gain, tabs, tabs_s, ones, *, tm):
    m, ms = x.shape[0], xs.shape[0]
    nt = tabs[0].shape[0] // tm
    row = lambda i: (i, 0)
    tab = lambda i: (i % nt, 0)
    fixed = lambda i: (0, 0)
    widths = ((D_MODEL, BF16), (KV_DIM, F32), (KV_DIM, F32), (2 * KV_DIM, BF16), (2 * KV_DIM, BF16))
    return pl.pallas_call(
        _qkv_body,
        grid=(m // tm,),
        in_specs=[
            pl.BlockSpec((tm, D_MODEL), row),
            pl.BlockSpec((ms, D_MODEL), fixed),
            pl.BlockSpec((1, D_MODEL), fixed),
            pl.BlockSpec((None, D_MODEL, QKV_DIM), lambda i: (layer, 0, 0),
                         pipeline_mode=pl.Buffered(1)),
            pl.BlockSpec((1, QK_DIM), fixed),
        ]
        + [pl.BlockSpec((tm, LANES), tab)] * 3
        + [pl.BlockSpec((ms, LANES), fixed)] * 3
        + [pl.BlockSpec((QKV_CHUNK, QKV_CHUNK), fixed)],
        out_specs=[pl.BlockSpec((tm, n), row) for n, _ in widths]
        + [pl.BlockSpec((ms, n), fixed) for n, _ in widths],
        out_shape=[jax.ShapeDtypeStruct((m, n), dt) for n, dt in widths]
        + [jax.ShapeDtypeStruct((ms, n), dt) for n, dt in widths],
        scratch_shapes=[
            pltpu.VMEM((D_MODEL, QKV_DIM), BF16),
            pltpu.VMEM((tm, D_MODEL), BF16),
            pltpu.VMEM((ms, D_MODEL), BF16),
        ],
        compiler_params=pltpu.CompilerParams(
            dimension_semantics=("arbitrary",), vmem_limit_bytes=VMEM_LIMIT),
        name="qkv",
    )(x, xs, g, w, gain, *tabs, *tabs_s, ones)


def _softmax_av(s, sink, v):
    mx = jnp.maximum(jnp.max(s, axis=-1, keepdims=True), sink)
    p = jnp.exp(s - mx)
    den = jnp.sum(p, axis=-1, keepdims=True) + jnp.exp(sink - mx)
    return _dot(p.astype(BF16), v) / den


SLOTS = 4
CHUNK = SLOTS * HEAD_DIM


def _attn_main_body(sink_ref, q_ref, k_ref, v_ref, km_ref, vm_ref, x_ref, wo_ref, o_ref, oh_ref,
                    *, nblk):
    rows2 = 2 * WINDOW
    r = lax.broadcasted_iota(jnp.int32, (rows2, WINDOW), 0) % WINDOW
    c = lax.broadcasted_iota(jnp.int32, (rows2, WINDOW), 1)
    prev = c > r
    top = lax.broadcasted_iota(jnp.int32, (rows2, 1), 0) < WINDOW
    lane_slot = lax.broadcasted_iota(jnp.int32, (rows2, CHUNK), 1) // HEAD_DIM
    slot_mask = [(lane_slot == s).astype(F32).astype(BF16) for s in range(SLOTS)]
    zero = jnp.zeros((rows2, WINDOW), F32)
    step = pl.program_id(1)
    seen = jnp.logical_not(prev) | (c >= WINDOW - N_META)

    def block(i, carry):
        row0 = pl.multiple_of(i * WINDOW, WINDOW)
        cur0 = pl.multiple_of(step * (nblk * WINDOW) + row0, WINDOW)
        past0 = pl.multiple_of(jnp.maximum(cur0 - WINDOW, 0), WINDOW)
        opens = cur0 == 0
        visible = seen | jnp.logical_not(opens)
        q_rows = pl.ds(row0, WINDOW)

        def keys_values(kv):
            lanes = slice(kv * LANES, (kv + 1) * LANES)
            k_past = jnp.where(opens, km_ref[0, :, lanes], k_ref[0, pl.ds(past0, WINDOW), lanes])
            v_past = jnp.where(opens, vm_ref[0, :, lanes], v_ref[0, pl.ds(past0, WINDOW), lanes])
            kd = jnp.concatenate([k_past, k_ref[0, pl.ds(cur0, WINDOW), lanes]], axis=0)
            vd = jnp.concatenate([v_past, v_ref[0, pl.ds(cur0, WINDOW), lanes]], axis=0)
            return jnp.concatenate([kd, kd], axis=1), jnp.concatenate([vd, vd], axis=1)

        def scores(kv, k4):
            g0 = kv * GROUP * HEAD_DIM
            qst = jnp.concatenate(
                [q_ref[q_rows, g0:g0 + CHUNK], q_ref[q_rows, g0 + CHUNK:g0 + 2 * CHUNK]], axis=0)
            return [_dot_nt(qst, k4 * slot_mask[s]) for s in range(SLOTS)]

        def weights(kv, s, sc):
            sf = jnp.where(prev, sc[:, :WINDOW], sc[:, WINDOW:])
            sf = jnp.where(visible, sf, NEG)
            sink = jnp.where(top, sink_ref[kv * GROUP + s], sink_ref[kv * GROUP + SLOTS + s])
            mx = jnp.maximum(jnp.max(sf, axis=-1, keepdims=True), sink)
            p = jnp.exp(sf - mx)
            den = jnp.sum(p, axis=-1, keepdims=True) + jnp.exp(sink - mx)
            p = p * (1.0 / den)
            p2 = jnp.concatenate([jnp.where(prev, p, zero), jnp.where(prev, zero, p)], axis=1)
            return p2.astype(BF16)

        def outputs(kv, ps, v4):
            acc = jnp.zeros((rows2, CHUNK), F32)
            for s in range(SLOTS):
                acc = acc + _dot(ps[s], v4 * slot_mask[s])
            g0 = kv * GROUP * HEAD_DIM
            oh_ref[q_rows, g0:g0 + CHUNK] = acc[:WINDOW].astype(BF16)
            oh_ref[q_rows, g0 + CHUNK:g0 + 2 * CHUNK] = acc[WINDOW:].astype(BF16)

        kvs = [keys_values(kv) for kv in range(N_KV_HEADS)]
        sc = scores(0, kvs[0][0])
        ps = None
        for kv in range(N_KV_HEADS):
            sc_next = scores(kv + 1, kvs[kv + 1][0]) if kv + 1 < N_KV_HEADS else None
            if ps is not None:
                outputs(kv - 1, ps, kvs[kv - 1][1])
            ps = [weights(kv, s, sc[s]) for s in range(SLOTS)]
            sc = sc_next
        outputs(N_KV_HEADS - 1, ps, kvs[N_KV_HEADS - 1][1])
        return carry

    lax.fori_loop(0, nblk, block, 0)
    o_ref[...] = x_ref[...] + _dot(oh_ref[...], wo_ref[...])


def _attn_main(sinks, q, kd, vd, km, vm, x, wo, layer, *, tq):
    nt = SEQ // tq
    row = lambda b, t: (b * nt + t, 0)
    batch = lambda b, t: (b, 0, 0)
    single = pl.Buffered(1)
    return pl.pallas_call(
        functools.partial(_attn_main_body, nblk=tq // WINDOW),
        grid=(BATCH, nt),
        in_specs=[
            pl.BlockSpec(memory_space=pltpu.SMEM),
            pl.BlockSpec((tq, D_MODEL), row),
            pl.BlockSpec((1, SEQ, 2 * KV_DIM), batch, pipeline_mode=single),
            pl.BlockSpec((1, SEQ, 2 * KV_DIM), batch, pipeline_mode=single),
            pl.BlockSpec((1, WINDOW, 2 * KV_DIM), batch),
            pl.BlockSpec((1, WINDOW, 2 * KV_DIM), batch),
            pl.BlockSpec((tq, D_MODEL), row),
            pl.BlockSpec((None, D_MODEL, D_MODEL), lambda b, t: (layer, 0, 0), pipeline_mode=single),
        ],
        out_specs=pl.BlockSpec((tq, D_MODEL), row),
        out_shape=jax.ShapeDtypeStruct((N_MAIN, D_MODEL), F32),
        scratch_shapes=[pltpu.VMEM((tq, D_MODEL), BF16)],
        compiler_params=pltpu.CompilerParams(
            dimension_semantics=("arbitrary", "arbitrary"), vmem_limit_bytes=VMEM_LIMIT),
        name="attn_main",
    )(sinks, q, kd, vd, km, vm, x, wo)


def _attn_small_body(sink_ref, q_ref, k_ref, v_ref, ck_ref, cv_ref, kn_ref, vn_ref, x_ref, wo_ref,
                     o_ref, nk_ref, nv_ref, oh_ref):
    nm = BATCH * N_META
    r = lax.broadcasted_iota(jnp.int32, (nm, nm), 0)
    c = lax.broadcasted_iota(jnp.int32, (nm, nm), 1)
    mmask = (c <= r) & ((r < N_META) == (c < N_META))
    km = k_ref[0:nm, :].astype(BF16)
    vm = v_ref[0:nm, :].astype(BF16)
    for h in range(N_HEADS):
        kv = h // GROUP
        hs = slice(h * HEAD_DIM, (h + 1) * HEAD_DIM)
        ks = slice(kv * HEAD_DIM, (kv + 1) * HEAD_DIM)
        s = jnp.where(mmask, _dot_nt(q_ref[0:nm, hs], km[:, ks]), NEG)
        oh_ref[0:nm, hs] = _softmax_av(s, sink_ref[h], vm[:, ks]).astype(BF16)

    nq = GROUP * DEC_BATCH
    wide = DEC_BATCH * HEAD_DIM
    own = (lax.broadcasted_iota(jnp.int32, (nq, wide), 0) % DEC_BATCH
           == lax.broadcasted_iota(jnp.int32, (nq, wide), 1) // HEAD_DIM).astype(F32)
    live = lax.broadcasted_iota(jnp.int32, (nq, WINDOW), 1) >= 1
    lower = lax.broadcasted_iota(jnp.int32, (1, LANES), 1) < HEAD_DIM
    for kv in range(N_KV_HEADS):
        ks = slice(kv * HEAD_DIM, (kv + 1) * HEAD_DIM)
        heads = range(kv * GROUP, (kv + 1) * GROUP)
        kts = ck_ref[:, kv].reshape(wide, WINDOW).astype(BF16)
        vts = cv_ref[:, kv].reshape(wide, WINDOW).astype(BF16)
        twice = []
        for h in heads[::2]:
            pair = q_ref[nm:, h * HEAD_DIM:(h + 2) * HEAD_DIM].astype(F32)
            swapped = pltpu.roll(pair, HEAD_DIM, 1)
            twice += [jnp.where(lower, pair, swapped), jnp.where(lower, swapped, pair)]
        qd = jnp.concatenate(twice, axis=0)
        qs = qd[:, :HEAD_DIM]
        qblk = (jnp.tile(qd, (1, wide // LANES)) * own).astype(BF16)
        sink = jnp.concatenate(
            [jnp.full((DEC_BATCH, 1), sink_ref[h], F32) for h in heads], axis=0)
        k_new = jnp.tile(k_ref[nm:, ks], (GROUP, 1))
        v_new = jnp.tile(v_ref[nm:, ks], (GROUP, 1))
        s = jnp.where(live, _dot(qblk, kts), NEG)
        s_new = jnp.sum(qs * k_new, axis=-1, keepdims=True)
        mx = jnp.maximum(jnp.maximum(jnp.max(s, axis=-1, keepdims=True), s_new), sink)
        p = jnp.exp(s - mx)
        p_new = jnp.exp(s_new - mx)
        den = jnp.sum(p, axis=-1, keepdims=True) + p_new + jnp.exp(sink - mx)
        spread = _dot_nt(p.astype(BF16), vts) * own
        folded = spread[:, :LANES]
        for t in range(1, wide // LANES):
            folded = folded + spread[:, t * LANES:(t + 1) * LANES]
        pv = (folded + pltpu.roll(folded, HEAD_DIM, 1))[:, :HEAD_DIM]
        o = (pv + p_new * v_new) / den
        for gi, h in enumerate(heads):
            oh_ref[nm:, h * HEAD_DIM:(h + 1) * HEAD_DIM] = (
                o[gi * DEC_BATCH:(gi + 1) * DEC_BATCH].astype(BF16))

    o_ref[...] = x_ref[...] + _dot(oh_ref[...], wo_ref[...])

    last = lax.broadcasted_iota(jnp.int32, (KV_DIM, WINDOW), 1) == WINDOW - 1

    def shift(b, carry):
        rows = pl.ds(pl.multiple_of(b * KV_DIM, KV_DIM), KV_DIM)
        for cache, new, out in ((ck_ref, kn_ref, nk_ref), (cv_ref, vn_ref, nv_ref)):
            rolled = pltpu.roll(cache[b].reshape(KV_DIM, WINDOW), WINDOW - 1, 1)
            out[b] = jnp.where(last, new[rows, :], rolled).reshape(N_KV_HEADS, HEAD_DIM, WINDOW)
        return carry

    lax.fori_loop(0, DEC_BATCH, shift, 0)


def _attn_small(sinks, q, k, v, ck, cv, kn, vn, x, wo, layer):
    single = pl.Buffered(1)
    cache_shape = (DEC_BATCH, N_KV_HEADS, HEAD_DIM, WINDOW)

    def whole(shape):
        return pl.BlockSpec(shape, lambda i: (0,) * len(shape), pipeline_mode=single)

    cache = pl.BlockSpec((None,) + cache_shape, lambda i: (layer, 0, 0, 0, 0), pipeline_mode=single)
    return pl.pallas_call(
        _attn_small_body,
        grid=(1,),
        in_specs=[pl.BlockSpec(memory_space=pltpu.SMEM)]
        + [whole(a.shape) for a in (q, k, v)] + [cache, cache] + [whole(a.shape) for a in (kn, vn, x)]
        + [pl.BlockSpec((None, D_MODEL, D_MODEL), lambda i: (layer, 0, 0), pipeline_mode=single)],
        out_specs=[whole((N_SMALL, D_MODEL)), whole(cache_shape), whole(cache_shape)],
        out_shape=[
            jax.ShapeDtypeStruct((N_SMALL, D_MODEL), F32),
            jax.ShapeDtypeStruct(cache_shape, F32),
            jax.ShapeDtypeStruct(cache_shape, F32),
        ],
        scratch_shapes=[pltpu.VMEM((N_SMALL, D_MODEL), BF16)],
        compiler_params=pltpu.CompilerParams(vmem_limit_bytes=VMEM_LIMIT),
        name="attn_small",
    )(sinks, q, k, v, ck, cv, kn, vn, x, wo)


def _rope_tables(pos):
    half = ROT_DIM // 2
    n = pos.shape[0]
    inv = jnp.float32(ROPE_THETA) ** (-jnp.arange(half, dtype=F32) * 2.0 / ROT_DIM)
    ang = pos.astype(F32)[:, None] * inv[None, :]
    cos, sin = jnp.cos(ang), jnp.sin(ang)
    rest = HEAD_DIM - ROT_DIM
    cos64 = jnp.concatenate([cos, cos, jnp.ones((n, rest), F32)], axis=1)
    sa64 = jnp.concatenate([-sin, jnp.zeros((n, half + rest), F32)], axis=1)
    sb64 = jnp.concatenate([jnp.zeros((n, half), F32), sin, jnp.zeros((n, rest), F32)], axis=1)
    rep = LANES // HEAD_DIM
    return tuple(jnp.tile(t, (1, rep)) for t in (cos64, sa64, sb64))


def kernel(x_prompt, x_sample, state_pool, cache_k, cache_v, meta_tokens, norm_mix, norm_ffn,
           pool_w, pool_scale, w_qkv, w_o, q_norm, k_norm, sinks, w_gate, w_up, w_down):
    assert x_prompt.shape == (BATCH, SEQ, D_MODEL) and x_sample.shape == (DEC_BATCH, 1, D_MODEL)
    nm = BATCH * N_META
    meta = meta_tokens.astype(F32)
    xm = x_prompt.reshape(N_MAIN, D_MODEL)
    xs = jnp.concatenate([meta, meta, x_sample.reshape(DEC_BATCH, D_MODEL)], axis=0)

    tabs_main = _rope_tables(N_META + jnp.arange(SEQ))
    tabs_small = _rope_tables(jnp.concatenate(
        [jnp.arange(N_META), jnp.arange(N_META), jnp.full((DEC_BATCH,), PAST_LEN)]))
    head_of_col = jnp.arange(QKV_CHUNK) // HEAD_DIM
    ones = (head_of_col[:, None] == head_of_col[None, :]).astype(BF16)

    wo = w_o.astype(BF16)
    ck = jnp.transpose(cache_k, (0, 1, 3, 4, 2))
    cv = jnp.transpose(cache_v, (0, 1, 3, 4, 2))

    pool_p, pool_s, kp_l, vp_l, ks_l, vs_l = [], [], [], [], [], []
    for i in range(DEPTH):
        j = i // 2
        g_mix = norm_mix[i].reshape(1, D_MODEL)
        if i % 2 == 0:
            w = pool_w[j].astype(BF16)
            sc = pool_scale[j].reshape(1, D_MODEL)
            st = jnp.swapaxes(state_pool[j], 0, 1)
            xm_new, h_last = _pool_main(xm, xs, g_mix, w, sc, tp=512)
            xs, ns = _pool_small(xs, st, g_mix, w, sc)
            xm = xm_new
            pool_p.append(h_last[:, N_META - POOL_STATE:])
            pool_s.append(jnp.swapaxes(ns, 0, 1))
        else:
            gain = jnp.concatenate(
                [jnp.tile(q_norm[j], N_HEADS), jnp.tile(k_norm[j], N_KV_HEADS)]).reshape(1, QK_DIM)
            (q_m, k_m, v_m, kd_m, vd_m, q_s, k_s, v_s, kd_s, vd_s) = _qkv(
                xm, xs, g_mix, w_qkv, j, gain, tabs_main, tabs_small, ones, tm=512)

            def meta_past(t):
                t = t[:nm].reshape(BATCH, N_META, 2 * KV_DIM)
                return jnp.pad(t, ((0, 0), (WINDOW - N_META, 0), (0, 0)))

            kn = k_s[nm:].reshape(DEC_BATCH * KV_DIM, 1)
            vn = v_s[nm:].reshape(DEC_BATCH * KV_DIM, 1)
            xm = _attn_main(sinks[j], q_m, kd_m.reshape(BATCH, SEQ, 2 * KV_DIM),
                            vd_m.reshape(BATCH, SEQ, 2 * KV_DIM), meta_past(kd_s), meta_past(vd_s),
                            xm, wo, j, tq=512)
            xs, nk, nv = _attn_small(sinks[j], q_s, k_s, v_s, ck, cv, kn, vn, xs, wo, j)
            last = lambda t: t.reshape(BATCH, SEQ, KV_DIM)[:, -WINDOW:].reshape(
                BATCH, WINDOW, N_KV_HEADS, HEAD_DIM)
            kp_l.append(last(k_m))
            vp_l.append(last(v_m))
            ks_l.append(jnp.transpose(nk, (0, 3, 1, 2)))
            vs_l.append(jnp.transpose(nv, (0, 3, 1, 2)))
        g_ffn = norm_ffn[i].reshape(1, D_MODEL)
        xm, xs = _ffn(xm, xs, g_ffn, w_gate, w_up, w_down, i, tm=1024, tf=512)

    return (xm.reshape(BATCH, SEQ, D_MODEL), xs[nm:].reshape(DEC_BATCH, 1, D_MODEL),
            jnp.stack(pool_p), jnp.stack(kp_l), jnp.stack(vp_l),
            jnp.stack(pool_s), jnp.stack(ks_l), jnp.stack(vs_l))
```

```python
import functools

import jax
import jax.numpy as jnp
from jax import lax
from jax.experimental import pallas as pl
from jax.experimental.pallas import tpu as pltpu

F32 = jnp.float32
BF16 = jnp.bfloat16

D_MODEL = 2048
BATCH = 2
SEQ = 4096
DEPTH = 4
DEC_BATCH = 32
PAST_LEN = 16384
N_META = 16
POOL_WINDOWS = (2, 4, 8, 16)
N_POOL_GROUPS = len(POOL_WINDOWS)
POOL_GROUP = D_MODEL // N_POOL_GROUPS
POOL_STATE = max(POOL_WINDOWS) - 1
HEAD_DIM = 64
N_HEADS = D_MODEL // HEAD_DIM
N_KV_HEADS = 4
GROUP = N_HEADS // N_KV_HEADS
WINDOW = 128
ROT_DIM = HEAD_DIM // 4
ROPE_THETA = 500000.0
D_FF = 5632
EPS = 1e-6
NEG = -1e30

N_MAIN = BATCH * SEQ
N_SMALL = BATCH * N_META + DEC_BATCH
KV_DIM = N_KV_HEADS * HEAD_DIM
QK_DIM = D_MODEL + KV_DIM
QKV_DIM = D_MODEL + 2 * KV_DIM
LANES = 128

VMEM_LIMIT = 60 * 1024 * 1024


def _rms(x, g):
    ms = jnp.mean(x * x, axis=-1, keepdims=True)
    return x * lax.rsqrt(ms + EPS) * g


def _dot(a, b):
    return jnp.dot(a, b, preferred_element_type=F32)


def _dot_nt(a, b):
    return lax.dot_general(a, b, (((1,), (1,)), ((), ())), preferred_element_type=F32)


def _split_bf16(x):
    hi = x.astype(BF16)
    lo = (x - hi.astype(F32)).astype(BF16)
    return hi, lo


def _ffn_body(x_ref, xs_ref, g_ref, wg_ref, wu_ref, wd_ref, o_hbm, os_ref, h_ref, acc_ref, sem):
    i, j = pl.program_id(0), pl.program_id(1)
    last_i, last_j = pl.num_programs(0) - 1, pl.num_programs(1) - 1
    tm, ms = x_ref.shape[0], xs_ref.shape[0]

    def writeback(tile):
        rows = pl.ds(pl.multiple_of(tile * tm, tm), tm)
        return pltpu.make_async_copy(acc_ref, o_hbm.at[rows, :], sem.at[0])

    def normalise(src_ref, row0):
        h_ref[row0:row0 + src_ref.shape[0], :] = _rms(src_ref[...], g_ref[...]).astype(BF16)

    def accumulate(rows, first):
        h = h_ref[0:rows, :]
        gate = _dot(h, wg_ref[...].astype(BF16))
        up = _dot(h, wu_ref[...].astype(BF16))
        act = (gate * jax.nn.sigmoid(gate) * up).astype(BF16)
        if first and rows == tm:
            writeback(i - 1).wait()
        y = _dot(act, wd_ref[...].astype(BF16))
        acc_ref[...] = (x_ref[...] if first else acc_ref[...]) + y[:tm]
        if rows > tm:
            os_ref[...] = (xs_ref[...] if first else os_ref[...]) + y[tm:]

    pl.when(j == 0)(lambda: normalise(x_ref, 0))
    pl.when((i == 0) & (j == 0))(lambda: normalise(xs_ref, tm))
    pl.when((i == 0) & (j == 0))(lambda: accumulate(tm + ms, True))
    pl.when((i == 0) & (j > 0))(lambda: accumulate(tm + ms, False))
    pl.when((i > 0) & (j == 0))(lambda: accumulate(tm, True))
    pl.when((i > 0) & (j > 0))(lambda: accumulate(tm, False))
    pl.when(j == last_j)(lambda: writeback(i).start())
    pl.when((i == last_i) & (j == last_j))(lambda: writeback(i).wait())


def _ffn(x, xs, g, wg, wu, wd, layer, *, tm, tf):
    m, ms = x.shape[0], xs.shape[0]
    return pl.pallas_call(
        _ffn_body,
        grid=(m // tm, D_FF // tf),
        in_specs=[
            pl.BlockSpec((tm, D_MODEL), lambda i, j: (i, 0)),
            pl.BlockSpec((ms, D_MODEL), lambda i, j: (0, 0)),
            pl.BlockSpec((1, D_MODEL), lambda i, j: (0, 0)),
            pl.BlockSpec((None, D_MODEL, tf), lambda i, j: (layer, 0, j)),
            pl.BlockSpec((None, D_MODEL, tf), lambda i, j: (layer, 0, j)),
            pl.BlockSpec((None, tf, D_MODEL), lambda i, j: (layer, j, 0)),
        ],
        out_specs=[
            pl.BlockSpec(memory_space=pl.ANY),
            pl.BlockSpec((ms, D_MODEL), lambda i, j: (0, 0)),
        ],
        out_shape=[
            jax.ShapeDtypeStruct((m, D_MODEL), F32),
            jax.ShapeDtypeStruct((ms, D_MODEL), F32),
        ],
        scratch_shapes=[
            pltpu.VMEM((tm + ms, D_MODEL), BF16),
            pltpu.VMEM((tm, D_MODEL), F32),
            pltpu.SemaphoreType.DMA((1,)),
        ],
        compiler_params=pltpu.CompilerParams(
            dimension_semantics=("arbitrary", "arbitrary"), vmem_limit_bytes=VMEM_LIMIT),
        name="ffn",
    )(x, xs, g, wg, wu, wd)


def _pool_project(diffs, w_ref, sc_ref):
    ys = [_dot(d.astype(BF16), w_ref[g]) for g, d in enumerate(diffs)]
    return jnp.concatenate(ys, axis=1) * sc_ref[...]


def _pool_main_body(x_ref, meta_ref, g_ref, w_ref, sc_ref, o_ref, hl_ref, ext_ref, *, tp):
    t = pl.program_id(1)

    @pl.when(t == 0)
    def _():
        ext_ref[0:N_META, :] = _rms(meta_ref[...], g_ref[...])

    x = x_ref[...]
    h = _rms(x, g_ref[...])
    ext_ref[N_META:N_META + tp, :] = h
    diffs = []
    for g, w in enumerate(POOL_WINDOWS):
        c0, c1 = g * POOL_GROUP, (g + 1) * POOL_GROUP
        s = ext_ref[:, c0:c1]
        shift = 1
        while shift < w:
            s = s + pltpu.roll(s, shift, 0)
            shift *= 2
        diffs.append(s[N_META:] * (1.0 / w) - h[:, c0:c1])
    o_ref[...] = x + _pool_project(diffs, w_ref, sc_ref)
    tail = ext_ref[tp:tp + N_META, :]
    ext_ref[0:N_META, :] = tail

    @pl.when(t == pl.num_programs(1) - 1)
    def _():
        hl_ref[0] = tail


def _pool_main(xm, xs, g, w, sc, *, tp):
    nt = SEQ // tp
    return pl.pallas_call(
        functools.partial(_pool_main_body, tp=tp),
        grid=(BATCH, nt),
        in_specs=[
            pl.BlockSpec((tp, D_MODEL), lambda b, t: (b * nt + t, 0)),
            pl.BlockSpec((N_META, D_MODEL), lambda b, t: (b, 0)),
            pl.BlockSpec((1, D_MODEL), lambda b, t: (0, 0)),
            pl.BlockSpec((N_POOL_GROUPS, POOL_GROUP, POOL_GROUP), lambda b, t: (0, 0, 0)),
            pl.BlockSpec((1, D_MODEL), lambda b, t: (0, 0)),
        ],
        out_specs=[
            pl.BlockSpec((tp, D_MODEL), lambda b, t: (b * nt + t, 0)),
            pl.BlockSpec((1, N_META, D_MODEL), lambda b, t: (b, 0, 0)),
        ],
        out_shape=[
            jax.ShapeDtypeStruct((N_MAIN, D_MODEL), F32),
            jax.ShapeDtypeStruct((BATCH, N_META, D_MODEL), F32),
        ],
        scratch_shapes=[pltpu.VMEM((tp + N_META, D_MODEL), F32)],
        compiler_params=pltpu.CompilerParams(
            dimension_semantics=("arbitrary", "arbitrary"), vmem_limit_bytes=VMEM_LIMIT),
        name="pool_main",
    )(xm, xs, g, w, sc)


def _pool_small_body(x_ref, st_ref, g_ref, w_ref, sc_ref, o_ref, ns_ref, ext_ref):
    nm = BATCH * N_META
    x = x_ref[...]
    h = _rms(x, g_ref[...])
    for b in range(BATCH):
        ext_ref[b, 0:N_META, :] = jnp.zeros((N_META, D_MODEL), F32)
        ext_ref[b, N_META:2 * N_META, :] = h[b * N_META:(b + 1) * N_META]
    hs = h[nm:]
    row = lax.broadcasted_iota(jnp.int32, (N_META, 1), 0)
    diffs = []
    for g, w in enumerate(POOL_WINDOWS):
        c0, c1 = g * POOL_GROUP, (g + 1) * POOL_GROUP
        cnt = jnp.minimum(row + 1, w).astype(F32)
        parts = []
        for b in range(BATCH):
            s = ext_ref[b, N_META:2 * N_META, c0:c1]
            for k in range(1, w):
                s = s + ext_ref[b, N_META - k:2 * N_META - k, c0:c1]
            parts.append(s / cnt)
        s = hs[:, c0:c1]
        for k in range(1, w):
            s = s + st_ref[POOL_STATE - k, :, c0:c1]
        parts.append(s / float(w))
        diffs.append(jnp.concatenate(parts, axis=0) - h[:, c0:c1])
    o_ref[...] = x + _pool_project(diffs, w_ref, sc_ref)
    ns_ref[0:POOL_STATE - 1] = st_ref[1:POOL_STATE]
    ns_ref[POOL_STATE - 1] = hs


def _pool_small(xs, st, g, w, sc):
    return pl.pallas_call(
        _pool_small_body,
        out_shape=[
            jax.ShapeDtypeStruct((N_SMALL, D_MODEL), F32),
            jax.ShapeDtypeStruct((POOL_STATE, DEC_BATCH, D_MODEL), F32),
        ],
        scratch_shapes=[pltpu.VMEM((BATCH, 2 * N_META, D_MODEL), F32)],
        compiler_params=pltpu.CompilerParams(vmem_limit_bytes=VMEM_LIMIT),
        name="pool_small",
    )(xs, st, g, w, sc)


QKV_CHUNK = 4 * HEAD_DIM


def _qkv_rows(x_ref, tab_refs, g_ref, w_ref, gain_ref, ones_ref, out_refs, h_ref):
    cos_ref, sa_ref, sb_ref = tab_refs
    q_ref, k_ref, v_ref, kd_ref, vd_ref = out_refs
    h_ref[...] = _rms(x_ref[...], g_ref[...]).astype(BF16)
    cos, sa, sb = cos_ref[...], sa_ref[...], sb_ref[...]
    half = ROT_DIM // 2

    def norm_rope(a, c0):
        sq_hi, sq_lo = _split_bf16(a * a)
        ss = _dot(sq_hi, ones_ref[...]) + _dot(sq_lo, ones_ref[...])
        y = a * lax.rsqrt(ss * (1.0 / HEAD_DIM) + EPS) * gain_ref[:, c0:c0 + QKV_CHUNK]
        tiles = []
        for u in range(QKV_CHUNK // LANES):
            yu = y[:, u * LANES:(u + 1) * LANES]
            tiles.append(yu * cos + pltpu.roll(yu, LANES - half, 1) * sa + pltpu.roll(yu, half, 1) * sb)
        return jnp.concatenate(tiles, axis=1)

    lower = lax.broadcasted_iota(jnp.int32, (1, LANES), 1) < HEAD_DIM

    def twice(t):
        tiles = []
        for u in range(KV_DIM // LANES):
            tu = t[:, u * LANES:(u + 1) * LANES]
            swapped = pltpu.roll(tu, HEAD_DIM, 1)
            tiles += [jnp.where(lower, tu, swapped), jnp.where(lower, swapped, tu)]
        return jnp.concatenate(tiles, axis=1).astype(BF16)

    wide = 2 * QKV_CHUNK
    nwide = QKV_DIM // wide
    project = lambda c: _dot(h_ref[...], w_ref[:, c * wide:(c + 1) * wide])
    a = project(0)
    for c in range(nwide):
        a_next = project(c + 1) if c + 1 < nwide else None
        c0 = c * wide
        if c0 < D_MODEL:
            for u in range(2):
                y = norm_rope(a[:, u * QKV_CHUNK:(u + 1) * QKV_CHUNK], c0 + u * QKV_CHUNK)
                q_ref[:, c0 + u * QKV_CHUNK:c0 + (u + 1) * QKV_CHUNK] = (
                    y * HEAD_DIM ** -0.5).astype(BF16)
        else:
            k = norm_rope(a[:, :KV_DIM], D_MODEL)
            v = a[:, KV_DIM:]
            k_ref[...] = k
            v_ref[...] = v
            kd_ref[...] = twice(k)
            vd_ref[...] = twice(v)
        a = a_next


def _qkv_body(*refs):
    x_ref, xs_ref, g_ref, w_ref, gain_ref = refs[:5]
    tabs, tabs_s, ones_ref = refs[5:8], refs[8:11], refs[11]
    outs, outs_s = refs[12:17], refs[17:22]
    wb_ref, h_ref, hs_ref = refs[22:]

    @pl.when(pl.program_id(0) == 0)
    def _():
        wb_ref[...] = w_ref[...].astype(BF16)
        _qkv_rows(xs_ref, tabs_s, g_ref, wb_ref, gain_ref, ones_ref, outs_s, hs_ref)

    _qkv_rows(x_ref, tabs, g_ref, wb_ref, gain_ref, ones_ref, outs, h_ref)


def _qkv(x, xs, g, w, layer, gain, tabs, tabs_s, ones, *, tm):
    m, ms = x.shape[0], xs.shape[0]
    nt = tabs[0].shape[0] // tm
    row = lambda i: (i, 0)
    tab = lambda i: (i % nt, 0)
    fixed = lambda i: (0, 0)
    widths = ((D_MODEL, BF16), (KV_DIM, F32), (KV_DIM, F32), (2 * KV_DIM, BF16), (2 * KV_DIM, BF16))
    return pl.pallas_call(
        _qkv_body,
        grid=(m // tm,),
        in_specs=[
            pl.BlockSpec((tm, D_MODEL), row),
            pl.BlockSpec((ms, D_MODEL), fixed),
            pl.BlockSpec((1, D_MODEL), fixed),
            pl.BlockSpec((None, D_MODEL, QKV_DIM), lambda i: (layer, 0, 0),
                         pipeline_mode=pl.Buffered(1)),
            pl.BlockSpec((1, QK_DIM), fixed),
        ]
        + [pl.BlockSpec((tm, LANES), tab)] * 3
        + [pl.BlockSpec((ms, LANES), fixed)] * 3
        + [pl.BlockSpec((QKV_CHUNK, QKV_CHUNK), fixed)],
        out_specs=[pl.BlockSpec((tm, n), row) for n, _ in widths]
        + [pl.BlockSpec((ms, n), fixed) for n, _ in widths],
        out_shape=[jax.ShapeDtypeStruct((m, n), dt) for n, dt in widths]
        + [jax.ShapeDtypeStruct((ms, n), dt) for n, dt in widths],
        scratch_shapes=[
            pltpu.VMEM((D_MODEL, QKV_DIM), BF16),
            pltpu.VMEM((tm, D_MODEL), BF16),
            pltpu.VMEM((ms, D_MODEL), BF16),
        ],
        compiler_params=pltpu.CompilerParams(
            dimension_semantics=("arbitrary",), vmem_limit_bytes=VMEM_LIMIT),
        name="qkv",
    )(x, xs, g, w, gain, *tabs, *tabs_s, ones)


def _softmax_av(s, sink, v):
    mx = jnp.maximum(jnp.max(s, axis=-1, keepdims=True), sink)
    p = jnp.exp(s - mx)
    den = jnp.sum(p, axis=-1, keepdims=True) + jnp.exp(sink - mx)
    return _dot(p.astype(BF16), v) / den


SLOTS = 4
CHUNK = SLOTS * HEAD_DIM


def _attn_main_body(sink_ref, q_ref, k_ref, v_ref, km_ref, vm_ref, x_ref, wo_ref, o_ref, oh_ref,
                    *, nblk):
    rows2 = 2 * WINDOW
    r = lax.broadcasted_iota(jnp.int32, (rows2, WINDOW), 0) % WINDOW
    c = lax.broadcasted_iota(jnp.int32, (rows2, WINDOW), 1)
    prev = c > r
    top = lax.broadcasted_iota(jnp.int32, (rows2, 1), 0) < WINDOW
    lane_slot = lax.broadcasted_iota(jnp.int32, (rows2, CHUNK), 1) // HEAD_DIM
    slot_mask = [(lane_slot == s).astype(F32).astype(BF16) for s in range(SLOTS)]
    zero = jnp.zeros((rows2, WINDOW), F32)
    step = pl.program_id(1)
    seen = jnp.logical_not(prev) | (c >= WINDOW - N_META)

    def block(i, carry):
        row0 = pl.multiple_of(i * WINDOW, WINDOW)
        cur0 = pl.multiple_of(step * (nblk * WINDOW) + row0, WINDOW)
        past0 = pl.multiple_of(jnp.maximum(cur0 - WINDOW, 0), WINDOW)
        opens = cur0 == 0
        visible = seen | jnp.logical_not(opens)
        q_rows = pl.ds(row0, WINDOW)

        def keys_values(kv):
            lanes = slice(kv * LANES, (kv + 1) * LANES)
            k_past = jnp.where(opens, km_ref[0, :, lanes], k_ref[0, pl.ds(past0, WINDOW), lanes])
            v_past = jnp.where(opens, vm_ref[0, :, lanes], v_ref[0, pl.ds(past0, WINDOW), lanes])
            kd = jnp.concatenate([k_past, k_ref[0, pl.ds(cur0, WINDOW), lanes]], axis=0)
            vd = jnp.concatenate([v_past, v_ref[0, pl.ds(cur0, WINDOW), lanes]], axis=0)
            return jnp.concatenate([kd, kd], axis=1), jnp.concatenate([vd, vd], axis=1)

        def scores(kv, k4):
            g0 = kv * GROUP * HEAD_DIM
            qst = jnp.concatenate(
                [q_ref[q_rows, g0:g0 + CHUNK], q_ref[q_rows, g0 + CHUNK:g0 + 2 * CHUNK]], axis=0)
            return [_dot_nt(qst, k4 * slot_mask[s]) for s in range(SLOTS)]

        def weights(kv, s, sc):
            sf = jnp.where(prev, sc[:, :WINDOW], sc[:, WINDOW:])
            sf = jnp.where(visible, sf, NEG)
            sink = jnp.where(top, sink_ref[kv * GROUP + s], sink_ref[kv * GROUP + SLOTS + s])
            mx = jnp.maximum(jnp.max(sf, axis=-1, keepdims=True), sink)
            p = jnp.exp(sf - mx)
            den = jnp.sum(p, axis=-1, keepdims=True) + jnp.exp(sink - mx)
            p = p * (1.0 / den)
            p2 = jnp.concatenate([jnp.where(prev, p, zero), jnp.where(prev, zero, p)], axis=1)
            return p2.astype(BF16)

        def outputs(kv, ps, v4):
            acc = jnp.zeros((rows2, CHUNK), F32)
            for s in range(SLOTS):
                acc = acc + _dot(ps[s], v4 * slot_mask[s])
            g0 = kv * GROUP * HEAD_DIM
            oh_ref[q_rows, g0:g0 + CHUNK] = acc[:WINDOW].astype(BF16)
            oh_ref[q_rows, g0 + CHUNK:g0 + 2 * CHUNK] = acc[WINDOW:].astype(BF16)

        kvs = [keys_values(kv) for kv in range(N_KV_HEADS)]
        sc = scores(0, kvs[0][0])
        ps = None
        for kv in range(N_KV_HEADS):
            sc_next = scores(kv + 1, kvs[kv + 1][0]) if kv + 1 < N_KV_HEADS else None
            if ps is not None:
                outputs(kv - 1, ps, kvs[kv - 1][1])
            ps = [weights(kv, s, sc[s]) for s in range(SLOTS)]
            sc = sc_next
        outputs(N_KV_HEADS - 1, ps, kvs[N_KV_HEADS - 1][1])
        return carry

    lax.fori_loop(0, nblk, block, 0)
    o_ref[...] = x_ref[...] + _dot(oh_ref[...], wo_ref[...])


def _attn_main(sinks, q, kd, vd, km, vm, x, wo, layer, *, tq):
    nt = SEQ // tq
    row = lambda b, t: (b * nt + t, 0)
    batch = lambda b, t: (b, 0, 0)
    single = pl.Buffered(1)
    return pl.pallas_call(
        functools.partial(_attn_main_body, nblk=tq // WINDOW),
        grid=(BATCH, nt),
        in_specs=[
            pl.BlockSpec(memory_space=pltpu.SMEM),
            pl.BlockSpec((tq, D_MODEL), row),
            pl.BlockSpec((1, SEQ, 2 * KV_DIM), batch, pipeline_mode=single),
            pl.BlockSpec((1, SEQ, 2 * KV_DIM), batch, pipeline_mode=single),
            pl.BlockSpec((1, WINDOW, 2 * KV_DIM), batch),
            pl.BlockSpec((1, WINDOW, 2 * KV_DIM), batch),
            pl.BlockSpec((tq, D_MODEL), row),
            pl.BlockSpec((None, D_MODEL, D_MODEL), lambda b, t: (layer, 0, 0), pipeline_mode=single),
        ],
        out_specs=pl.BlockSpec((tq, D_MODEL), row),
        out_shape=jax.ShapeDtypeStruct((N_MAIN, D_MODEL), F32),
        scratch_shapes=[pltpu.VMEM((tq, D_MODEL), BF16)],
        compiler_params=pltpu.CompilerParams(
            dimension_semantics=("arbitrary", "arbitrary"), vmem_limit_bytes=VMEM_LIMIT),
        name="attn_main",
    )(sinks, q, kd, vd, km, vm, x, wo)


def _attn_small_body(sink_ref, q_ref, k_ref, v_ref, ck_ref, cv_ref, kn_ref, vn_ref, x_ref, wo_ref,
                     o_ref, nk_ref, nv_ref, oh_ref):
    nm = BATCH * N_META
    r = lax.broadcasted_iota(jnp.int32, (nm, nm), 0)
    c = lax.broadcasted_iota(jnp.int32, (nm, nm), 1)
    mmask = (c <= r) & ((r < N_META) == (c < N_META))
    km = k_ref[0:nm, :].astype(BF16)
    vm = v_ref[0:nm, :].astype(BF16)
    for h in range(N_HEADS):
        kv = h // GROUP
        hs = slice(h * HEAD_DIM, (h + 1) * HEAD_DIM)
        ks = slice(kv * HEAD_DIM, (kv + 1) * HEAD_DIM)
        s = jnp.where(mmask, _dot_nt(q_ref[0:nm, hs], km[:, ks]), NEG)
        oh_ref[0:nm, hs] = _softmax_av(s, sink_ref[h], vm[:, ks]).astype(BF16)

    nq = GROUP * DEC_BATCH
    wide = DEC_BATCH * HEAD_DIM
    own = (lax.broadcasted_iota(jnp.int32, (nq, wide), 0) % DEC_BATCH
           == lax.broadcasted_iota(jnp.int32, (nq, wide), 1) // HEAD_DIM).astype(F32)
    live = lax.broadcasted_iota(jnp.int32, (nq, WINDOW), 1) >= 1
    lower = lax.broadcasted_iota(jnp.int32, (1, LANES), 1) < HEAD_DIM
    for kv in range(N_KV_HEADS):
        ks = slice(kv * HEAD_DIM, (kv + 1) * HEAD_DIM)
        heads = range(kv * GROUP, (kv + 1) * GROUP)
        kts = ck_ref[:, kv].reshape(wide, WINDOW).astype(BF16)
        vts = cv_ref[:, kv].reshape(wide, WINDOW).astype(BF16)
        twice = []
        for h in heads[::2]:
            pair = q_ref[nm:, h * HEAD_DIM:(h + 2) * HEAD_DIM].astype(F32)
            swapped = pltpu.roll(pair, HEAD_DIM, 1)
            twice += [jnp.where(lower, pair, swapped), jnp.where(lower, swapped, pair)]
        qd = jnp.concatenate(twice, axis=0)
        qs = qd[:, :HEAD_DIM]
        qblk = (jnp.tile(qd, (1, wide // LANES)) * own).astype(BF16)
        sink = jnp.concatenate(
            [jnp.full((DEC_BATCH, 1), sink_ref[h], F32) for h in heads], axis=0)
        k_new = jnp.tile(k_ref[nm:, ks], (GROUP, 1))
        v_new = jnp.tile(v_ref[nm:, ks], (GROUP, 1))
        s = jnp.where(live, _dot(qblk, kts), NEG)
        s_new = jnp.sum(qs * k_new, axis=-1, keepdims=True)
        mx = jnp.maximum(jnp.maximum(jnp.max(s, axis=-1, keepdims=True), s_new), sink)
        p = jnp.exp(s - mx)
        p_new = jnp.exp(s_new - mx)
        den = jnp.sum(p, axis=-1, keepdims=True) + p_new + jnp.exp(sink - mx)
        spread = _dot_nt(p.astype(BF16), vts) * own
        folded = spread[:, :LANES]
        for t in range(1, wide // LANES):
            folded = folded + spread[:, t * LANES:(t + 1) * LANES]
        pv = (folded + pltpu.roll(folded, HEAD_DIM, 1))[:, :HEAD_DIM]
        o = (pv + p_new * v_new) / den
        for gi, h in enumerate(heads):
            oh_ref[nm:, h * HEAD_DIM:(h + 1) * HEAD_DIM] = (
                o[gi * DEC_BATCH:(gi + 1) * DEC_BATCH].astype(BF16))

    o_ref[...] = x_ref[...] + _dot(oh_ref[...], wo_ref[...])

    last = lax.broadcasted_iota(jnp.int32, (KV_DIM, WINDOW), 1) == WINDOW - 1

    def shift(b, carry):
        rows = pl.ds(pl.multiple_of(b * KV_DIM, KV_DIM), KV_DIM)
        for cache, new, out in ((ck_ref, kn_ref, nk_ref), (cv_ref, vn_ref, nv_ref)):
            rolled = pltpu.roll(cache[b].reshape(KV_DIM, WINDOW), WINDOW - 1, 1)
            out[b] = jnp.where(last, new[rows, :], rolled).reshape(N_KV_HEADS, HEAD_DIM, WINDOW)
        return carry

    lax.fori_loop(0, DEC_BATCH, shift, 0)


def _attn_small(sinks, q, k, v, ck, cv, kn, vn, x, wo, layer):
    single = pl.Buffered(1)
    cache_shape = (DEC_BATCH, N_KV_HEADS, HEAD_DIM, WINDOW)

    def whole(shape):
        return pl.BlockSpec(shape, lambda i: (0,) * len(shape), pipeline_mode=single)

    cache = pl.BlockSpec((None,) + cache_shape, lambda i: (layer, 0, 0, 0, 0), pipeline_mode=single)
    return pl.pallas_call(
        _attn_small_body,
        grid=(1,),
        in_specs=[pl.BlockSpec(memory_space=pltpu.SMEM)]
        + [whole(a.shape) for a in (q, k, v)] + [cache, cache] + [whole(a.shape) for a in (kn, vn, x)]
        + [pl.BlockSpec((None, D_MODEL, D_MODEL), lambda i: (layer, 0, 0), pipeline_mode=single)],
        out_specs=[whole((N_SMALL, D_MODEL)), whole(cache_shape), whole(cache_shape)],
        out_shape=[
            jax.ShapeDtypeStruct((N_SMALL, D_MODEL), F32),
            jax.ShapeDtypeStruct(cache_shape, F32),
            jax.ShapeDtypeStruct(cache_shape, F32),
        ],
        scratch_shapes=[pltpu.VMEM((N_SMALL, D_MODEL), BF16)],
        compiler_params=pltpu.CompilerParams(vmem_limit_bytes=VMEM_LIMIT),
        name="attn_small",
    )(sinks, q, k, v, ck, cv, kn, vn, x, wo)


def _rope_tables(pos):
    half = ROT_DIM // 2
    n = pos.shape[0]
    inv = jnp.float32(ROPE_THETA) ** (-jnp.arange(half, dtype=F32) * 2.0 / ROT_DIM)
    ang = pos.astype(F32)[:, None] * inv[None, :]
    cos, sin = jnp.cos(ang), jnp.sin(ang)
    rest = HEAD_DIM - ROT_DIM
    cos64 = jnp.concatenate([cos, cos, jnp.ones((n, rest), F32)], axis=1)
    sa64 = jnp.concatenate([-sin, jnp.zeros((n, half + rest), F32)], axis=1)
    sb64 = jnp.concatenate([jnp.zeros((n, half), F32), sin, jnp.zeros((n, rest), F32)], axis=1)
    rep = LANES // HEAD_DIM
    return tuple(jnp.tile(t, (1, rep)) for t in (cos64, sa64, sb64))


def kernel(x_prompt, x_sample, state_pool, cache_k, cache_v, meta_tokens, norm_mix, norm_ffn,
           pool_w, pool_scale, w_qkv, w_o, q_norm, k_norm, sinks, w_gate, w_up, w_down):
    assert x_prompt.shape == (BATCH, SEQ, D_MODEL) and x_sample.shape == (DEC_BATCH, 1, D_MODEL)
    nm = BATCH * N_META
    meta = meta_tokens.astype(F32)
    xm = x_prompt.reshape(N_MAIN, D_MODEL)
    xs = jnp.concatenate([meta, meta, x_sample.reshape(DEC_BATCH, D_MODEL)], axis=0)

    tabs_main = _rope_tables(N_META + jnp.arange(SEQ))
    tabs_small = _rope_tables(jnp.concatenate(
        [jnp.arange(N_META), jnp.arange(N_META), jnp.full((DEC_BATCH,), PAST_LEN)]))
    head_of_col = jnp.arange(QKV_CHUNK) // HEAD_DIM
    ones = (head_of_col[:, None] == head_of_col[None, :]).astype(BF16)

    wo = w_o.astype(BF16)
    ck = jnp.transpose(cache_k, (0, 1, 3, 4, 2))
    cv = jnp.transpose(cache_v, (0, 1, 3, 4, 2))

    pool_p, pool_s, kp_l, vp_l, ks_l, vs_l = [], [], [], [], [], []
    for i in range(DEPTH):
        j = i // 2
        g_mix = norm_mix[i].reshape(1, D_MODEL)
        if i % 2 == 0:
            w = pool_w[j].astype(BF16)
            sc = pool_scale[j].reshape(1, D_MODEL)
            st = jnp.swapaxes(state_pool[j], 0, 1)
            xm_new, h_last = _pool_main(xm, xs, g_mix, w, sc, tp=512)
            xs, ns = _pool_small(xs, st, g_mix, w, sc)
            xm = xm_new
            pool_p.append(h_last[:, N_META - POOL_STATE:])
            pool_s.append(jnp.swapaxes(ns, 0, 1))
        else:
            gain = jnp.concatenate(
                [jnp.tile(q_norm[j], N_HEADS), jnp.tile(k_norm[j], N_KV_HEADS)]).reshape(1, QK_DIM)
            (q_m, k_m, v_m, kd_m, vd_m, q_s, k_s, v_s, kd_s, vd_s) = _qkv(
                xm, xs, g_mix, w_qkv, j, gain, tabs_main, tabs_small, ones, tm=512)

            def meta_past(t):
                t = t[:nm].reshape(BATCH, N_META, 2 * KV_DIM)
                return jnp.pad(t, ((0, 0), (WINDOW - N_META, 0), (0, 0)))

            kn = k_s[nm:].reshape(DEC_BATCH * KV_DIM, 1)
            vn = v_s[nm:].reshape(DEC_BATCH * KV_DIM, 1)
            xm = _attn_main(sinks[j], q_m, kd_m.reshape(BATCH, SEQ, 2 * KV_DIM),
                            vd_m.reshape(BATCH, SEQ, 2 * KV_DIM), meta_past(kd_s), meta_past(vd_s),
                            xm, wo, j, tq=512)
            xs, nk, nv = _attn_small(sinks[j], q_s, k_s, v_s, ck, cv, kn, vn, xs, wo, j)
            last = lambda t: t.reshape(BATCH, SEQ, KV_DIM)[:, -WINDOW:].reshape(
                BATCH, WINDOW, N_KV_HEADS, HEAD_DIM)
            kp_l.append(last(k_m))
            vp_l.append(last(v_m))
            ks_l.append(jnp.transpose(nk, (0, 3, 1, 2)))
            vs_l.append(jnp.transpose(nv, (0, 3, 1, 2)))
        g_ffn = norm_ffn[i].reshape(1, D_MODEL)
        xm, xs = _ffn(xm, xs, g_ffn, w_gate, w_up, w_down, i, tm=1024, tf=512)

    return (xm.reshape(BATCH, SEQ, D_MODEL), xs[nm:].reshape(DEC_BATCH, 1, D_MODEL),
            jnp.stack(pool_p), jnp.stack(kp_l), jnp.stack(vp_l),
            jnp.stack(pool_s), jnp.stack(ks_l), jnp.stack(vs_l))
```

```python
import functools

import jax
import jax.numpy as jnp
from jax import lax
from jax.experimental import pallas as pl
from jax.experimental.pallas import tpu as pltpu

F32 = jnp.float32
BF16 = jnp.bfloat16

D_MODEL = 2048
BATCH = 2
SEQ = 4096
DEPTH = 4
DEC_BATCH = 32
PAST_LEN = 16384
N_META = 16
POOL_WINDOWS = (2, 4, 8, 16)
N_POOL_GROUPS = len(POOL_WINDOWS)
POOL_GROUP = D_MODEL // N_POOL_GROUPS
POOL_STATE = max(POOL_WINDOWS) - 1
HEAD_DIM = 64
N_HEADS = D_MODEL // HEAD_DIM
N_KV_HEADS = 4
GROUP = N_HEADS // N_KV_HEADS
WINDOW = 128
ROT_DIM = HEAD_DIM // 4
ROPE_THETA = 500000.0
D_FF = 5632
EPS = 1e-6
NEG = -1e30

N_MAIN = BATCH * SEQ
N_SMALL = BATCH * N_META + DEC_BATCH
KV_DIM = N_KV_HEADS * HEAD_DIM
QK_DIM = D_MODEL + KV_DIM
QKV_DIM = D_MODEL + 2 * KV_DIM
LANES = 128

VMEM_LIMIT = 60 * 1024 * 1024


def _rms(x, g):
    ms = jnp.mean(x * x, axis=-1, keepdims=True)
    return x * lax.rsqrt(ms + EPS) * g


def _dot(a, b):
    return jnp.dot(a, b, preferred_element_type=F32)


def _dot_nt(a, b):
    return lax.dot_general(a, b, (((1,), (1,)), ((), ())), preferred_element_type=F32)


def _split_bf16(x):
    hi = x.astype(BF16)
    lo = (x - hi.astype(F32)).astype(BF16)
    return hi, lo


def _ffn_body(x_ref, xs_ref, g_ref, wg_ref, wu_ref, wd_ref, o_hbm, os_ref, h_ref, acc_ref, sem):
    i, j = pl.program_id(0), pl.program_id(1)
    last_i, last_j = pl.num_programs(0) - 1, pl.num_programs(1) - 1
    tm, ms = x_ref.shape[0], xs_ref.shape[0]

    def writeback(tile):
        rows = pl.ds(pl.multiple_of(tile * tm, tm), tm)
        return pltpu.make_async_copy(acc_ref, o_hbm.at[rows, :], sem.at[0])

    def normalise(src_ref, row0):
        h_ref[row0:row0 + src_ref.shape[0], :] = _rms(src_ref[...], g_ref[...]).astype(BF16)

    def accumulate(rows, first):
        h = h_ref[0:rows, :]
        gate = _dot(h, wg_ref[...].astype(BF16))
        up = _dot(h, wu_ref[...].astype(BF16))
        act = (gate * jax.nn.sigmoid(gate) * up).astype(BF16)
        if first and rows == tm:
            writeback(i - 1).wait()
        y = _dot(act, wd_ref[...].astype(BF16))
        acc_ref[...] = (x_ref[...] if first else acc_ref[...]) + y[:tm]
        if rows > tm:
            os_ref[...] = (xs_ref[...] if first else os_ref[...]) + y[tm:]

    pl.when(j == 0)(lambda: normalise(x_ref, 0))
    pl.when((i == 0) & (j == 0))(lambda: normalise(xs_ref, tm))
    pl.when((i == 0) & (j == 0))(lambda: accumulate(tm + ms, True))
    pl.when((i == 0) & (j > 0))(lambda: accumulate(tm + ms, False))
    pl.when((i > 0) & (j == 0))(lambda: accumulate(tm, True))
    pl.when((i > 0) & (j > 0))(lambda: accumulate(tm, False))
    pl.when(j == last_j)(lambda: writeback(i).start())
    pl.when((i == last_i) & (j == last_j))(lambda: writeback(i).wait())


def _ffn(x, xs, g, wg, wu, wd, layer, *, tm, tf):
    m, ms = x.shape[0], xs.shape[0]
    return pl.pallas_call(
        _ffn_body,
        grid=(m // tm, D_FF // tf),
        in_specs=[
            pl.BlockSpec((tm, D_MODEL), lambda i, j: (i, 0)),
            pl.BlockSpec((ms, D_MODEL), lambda i, j: (0, 0)),
            pl.BlockSpec((1, D_MODEL), lambda i, j: (0, 0)),
            pl.BlockSpec((None, D_MODEL, tf), lambda i, j: (layer, 0, j)),
            pl.BlockSpec((None, D_MODEL, tf), lambda i, j: (layer, 0, j)),
            pl.BlockSpec((None, tf, D_MODEL), lambda i, j: (layer, j, 0)),
        ],
        out_specs=[
            pl.BlockSpec(memory_space=pl.ANY),
            pl.BlockSpec((ms, D_MODEL), lambda i, j: (0, 0)),
        ],
        out_shape=[
            jax.ShapeDtypeStruct((m, D_MODEL), F32),
            jax.ShapeDtypeStruct((ms, D_MODEL), F32),
        ],
        scratch_shapes=[
            pltpu.VMEM((tm + ms, D_MODEL), BF16),
            pltpu.VMEM((tm, D_MODEL), F32),
            pltpu.SemaphoreType.DMA((1,)),
        ],
        compiler_params=pltpu.CompilerParams(
            dimension_semantics=("arbitrary", "arbitrary"), vmem_limit_bytes=VMEM_LIMIT),
        name="ffn",
    )(x, xs, g, wg, wu, wd)


def _pool_project(diffs, w_ref, sc_ref):
    ys = [_dot(d.astype(BF16), w_ref[g]) for g, d in enumerate(diffs)]
    return jnp.concatenate(ys, axis=1) * sc_ref[...]


def _pool_main_body(x_ref, meta_ref, g_ref, w_ref, sc_ref, o_ref, hl_ref, ext_ref, *, tp):
    t = pl.program_id(1)

    @pl.when(t == 0)
    def _():
        ext_ref[0:N_META, :] = _rms(meta_ref[...], g_ref[...])

    x = x_ref[...]
    h = _rms(x, g_ref[...])
    ext_ref[N_META:N_META + tp, :] = h
    diffs = []
    for g, w in enumerate(POOL_WINDOWS):
        c0, c1 = g * POOL_GROUP, (g + 1) * POOL_GROUP
        s = ext_ref[:, c0:c1]
        shift = 1
        while shift < w:
            s = s + pltpu.roll(s, shift, 0)
            shift *= 2
        diffs.append(s[N_META:] * (1.0 / w) - h[:, c0:c1])
    o_ref[...] = x + _pool_project(diffs, w_ref, sc_ref)
    tail = ext_ref[tp:tp + N_META, :]
    ext_ref[0:N_META, :] = tail

    @pl.when(t == pl.num_programs(1) - 1)
    def _():
        hl_ref[0] = tail


def _pool_main(xm, xs, g, w, sc, *, tp):
    nt = SEQ // tp
    return pl.pallas_call(
        functools.partial(_pool_main_body, tp=tp),
        grid=(BATCH, nt),
        in_specs=[
            pl.BlockSpec((tp, D_MODEL), lambda b, t: (b * nt + t, 0)),
            pl.BlockSpec((N_META, D_MODEL), lambda b, t: (b, 0)),
            pl.BlockSpec((1, D_MODEL), lambda b, t: (0, 0)),
            pl.BlockSpec((N_POOL_GROUPS, POOL_GROUP, POOL_GROUP), lambda b, t: (0, 0, 0)),
            pl.BlockSpec((1, D_MODEL), lambda b, t: (0, 0)),
        ],
        out_specs=[
            pl.BlockSpec((tp, D_MODEL), lambda b, t: (b * nt + t, 0)),
            pl.BlockSpec((1, N_META, D_MODEL), lambda b, t: (b, 0, 0)),
        ],
        out_shape=[
            jax.ShapeDtypeStruct((N_MAIN, D_MODEL), F32),
            jax.ShapeDtypeStruct((BATCH, N_META, D_MODEL), F32),
        ],
        scratch_shapes=[pltpu.VMEM((tp + N_META, D_MODEL), F32)],
        compiler_params=pltpu.CompilerParams(
            dimension_semantics=("arbitrary", "arbitrary"), vmem_limit_bytes=VMEM_LIMIT),
        name="pool_main",
    )(xm, xs, g, w, sc)


def _pool_small_body(x_ref, st_ref, g_ref, w_ref, sc_ref, o_ref, ns_ref, ext_ref):
    nm = BATCH * N_META
    x = x_ref[...]
    h = _rms(x, g_ref[...])
    for b in range(BATCH):
        ext_ref[b, 0:N_META, :] = jnp.zeros((N_META, D_MODEL), F32)
        ext_ref[b, N_META:2 * N_META, :] = h[b * N_META:(b + 1) * N_META]
    hs = h[nm:]
    row = lax.broadcasted_iota(jnp.int32, (N_META, 1), 0)
    diffs = []
    for g, w in enumerate(POOL_WINDOWS):
        c0, c1 = g * POOL_GROUP, (g + 1) * POOL_GROUP
        cnt = jnp.minimum(row + 1, w).astype(F32)
        parts = []
        for b in range(BATCH):
            s = ext_ref[b, N_META:2 * N_META, c0:c1]
            for k in range(1, w):
                s = s + ext_ref[b, N_META - k:2 * N_META - k, c0:c1]
            parts.append(s / cnt)
        s = hs[:, c0:c1]
        for k in range(1, w):
            s = s + st_ref[POOL_STATE - k, :, c0:c1]
        parts.append(s / float(w))
        diffs.append(jnp.concatenate(parts, axis=0) - h[:, c0:c1])
    o_ref[...] = x + _pool_project(diffs, w_ref, sc_ref)
    ns_ref[0:POOL_STATE - 1] = st_ref[1:POOL_STATE]
    ns_ref[POOL_STATE - 1] = hs


def _pool_small(xs, st, g, w, sc):
    return pl.pallas_call(
        _pool_small_body,
        out_shape=[
            jax.ShapeDtypeStruct((N_SMALL, D_MODEL), F32),
            jax.ShapeDtypeStruct((POOL_STATE, DEC_BATCH, D_MODEL), F32),
        ],
        scratch_shapes=[pltpu.VMEM((BATCH, 2 * N_META, D_MODEL), F32)],
        compiler_params=pltpu.CompilerParams(vmem_limit_bytes=VMEM_LIMIT),
        name="pool_small",
    )(xs, st, g, w, sc)


QKV_CHUNK = 4 * HEAD_DIM


def _qkv_rows(x_ref, tab_refs, g_ref, w_ref, gain_ref, ones_ref, out_refs, h_ref):
    cos_ref, sa_ref, sb_ref = tab_refs
    q_ref, k_ref, v_ref, kd_ref, vd_ref = out_refs
    h_ref[...] = _rms(x_ref[...], g_ref[...]).astype(BF16)
    cos, sa, sb = cos_ref[...], sa_ref[...], sb_ref[...]
    half = ROT_DIM // 2

    def norm_rope(a, c0):
        sq_hi, sq_lo = _split_bf16(a * a)
        ss = _dot(sq_hi, ones_ref[...]) + _dot(sq_lo, ones_ref[...])
        y = a * lax.rsqrt(ss * (1.0 / HEAD_DIM) + EPS) * gain_ref[:, c0:c0 + QKV_CHUNK]
        tiles = []
        for u in range(QKV_CHUNK // LANES):
            yu = y[:, u * LANES:(u + 1) * LANES]
            tiles.append(yu * cos + pltpu.roll(yu, LANES - half, 1) * sa + pltpu.roll(yu, half, 1) * sb)
        return jnp.concatenate(tiles, axis=1)

    lower = lax.broadcasted_iota(jnp.int32, (1, LANES), 1) < HEAD_DIM

    def twice(t):
        tiles = []
        for u in range(KV_DIM // LANES):
            tu = t[:, u * LANES:(u + 1) * LANES]
            swapped = pltpu.roll(tu, HEAD_DIM, 1)
            tiles += [jnp.where(lower, tu, swapped), jnp.where(lower, swapped, tu)]
        return jnp.concatenate(tiles, axis=1).astype(BF16)

    wide = 2 * QKV_CHUNK
    nwide = QKV_DIM // wide
    project = lambda c: _dot(h_ref[...], w_ref[:, c * wide:(c + 1) * wide])
    a = project(0)
    for c in range(nwide):
        a_next = project(c + 1) if c + 1 < nwide else None
        c0 = c * wide
        if c0 < D_MODEL:
            for u in range(2):
                y = norm_rope(a[:, u * QKV_CHUNK:(u + 1) * QKV_CHUNK], c0 + u * QKV_CHUNK)
                q_ref[:, c0 + u * QKV_CHUNK:c0 + (u + 1) * QKV_CHUNK] = (
                    y * HEAD_DIM ** -0.5).astype(BF16)
        else:
            k = norm_rope(a[:, :KV_DIM], D_MODEL)
            v = a[:, KV_DIM:]
            k_ref[...] = k
            v_ref[...] = v
            kd_ref[...] = twice(k)
            vd_ref[...] = twice(v)
        a = a_next


def _qkv_body(*refs):
    x_ref, xs_ref, g_ref, w_ref, gain_ref = refs[:5]
    tabs, tabs_s, ones_ref = refs[5:8], refs[8:11], refs[11]
    outs, outs_s = refs[12:17], refs[17:22]
    wb_ref, h_ref, hs_ref = refs[22:]

    @pl.when(pl.program_id(0) == 0)
    def _():
        wb_ref[...] = w_ref[...].astype(BF16)
        _qkv_rows(xs_ref, tabs_s, g_ref, wb_ref, gain_ref, ones_ref, outs_s, hs_ref)

    _qkv_rows(x_ref, tabs, g_ref, wb_ref, gain_ref, ones_ref, outs, h_ref)


def _qkv(x, xs, g, w, layer, gain, tabs, tabs_s, ones, *, tm):
    m, ms = x.shape[0], xs.shape[0]
    nt = tabs[0].shape[0] // tm
    row = lambda i: (i, 0)
    tab = lambda i: (i % nt, 0)
    fixed = lambda i: (0, 0)
    widths = ((D_MODEL, BF16), (KV_DIM, F32), (KV_DIM, F32), (2 * KV_DIM, BF16), (2 * KV_DIM, BF16))
    return pl.pallas_call(
        _qkv_body,
        grid=(m // tm,),
        in_specs=[
            pl.BlockSpec((tm, D_MODEL), row),
            pl.BlockSpec((ms, D_MODEL), fixed),
            pl.BlockSpec((1, D_MODEL), fixed),
            pl.BlockSpec((None, D_MODEL, QKV_DIM), lambda i: (layer, 0, 0),
                         pipeline_mode=pl.Buffered(1)),
            pl.BlockSpec((1, QK_DIM), fixed),
        ]
        + [pl.BlockSpec((tm, LANES), tab)] * 3
        + [pl.BlockSpec((ms, LANES), fixed)] * 3
        + [pl.BlockSpec((QKV_CHUNK, QKV_CHUNK), fixed)],
        out_specs=[pl.BlockSpec((tm, n), row) for n, _ in widths]
        + [pl.BlockSpec((ms, n), fixed) for n, _ in widths],
        out_shape=[jax.ShapeDtypeStruct((m, n), dt) for n, dt in widths]
        + [jax.ShapeDtypeStruct((ms, n), dt) for n, dt in widths],
        scratch_shapes=[
            pltpu.VMEM((D_MODEL, QKV_DIM), BF16),
            pltpu.VMEM((tm, D_MODEL), BF16),
            pltpu.VMEM((ms, D_MODEL), BF16),
        ],
        compiler_params=pltpu.CompilerParams(
            dimension_semantics=("arbitrary",), vmem_limit_bytes=VMEM_LIMIT),
        name="qkv",
    )(x, xs, g, w, gain, *tabs, *tabs_s, ones)


def _softmax_av(s, sink, v):
    mx = jnp.maximum(jnp.max(s, axis=-1, keepdims=True), sink)
    p = jnp.exp(s - mx)
    den = jnp.sum(p, axis=-1, keepdims=True) + jnp.exp(sink - mx)
    return _dot(p.astype(BF16), v) / den


SLOTS = 4
CHUNK = SLOTS * HEAD_DIM


def _attn_main_body(sink_ref, q_ref, k_ref, v_ref, km_ref, vm_ref, x_ref, wo_ref, o_ref, oh_ref,
                    *, nblk):
    rows2 = 2 * WINDOW
    r = lax.broadcasted_iota(jnp.int32, (rows2, WINDOW), 0) % WINDOW
    c = lax.broadcasted_iota(jnp.int32, (rows2, WINDOW), 1)
    prev = c > r
    top = lax.broadcasted_iota(jnp.int32, (rows2, 1), 0) < WINDOW
    lane_slot = lax.broadcasted_iota(jnp.int32, (rows2, CHUNK), 1) // HEAD_DIM
    slot_mask = [(lane_slot == s).astype(F32).astype(BF16) for s in range(SLOTS)]
    zero = jnp.zeros((rows2, WINDOW), F32)
    step = pl.program_id(1)
    seen = jnp.logical_not(prev) | (c >= WINDOW - N_META)

    def block(i):
        row0 = i * WINDOW
        cur0 = pl.multiple_of(step * (nblk * WINDOW) + row0, WINDOW)
        past0 = pl.multiple_of(jnp.maximum(cur0 - WINDOW, 0), WINDOW)
        opens = cur0 == 0
        visible = seen | jnp.logical_not(opens)
        q_rows = pl.ds(row0, WINDOW)

        def keys_values(kv):
            lanes = slice(kv * LANES, (kv + 1) * LANES)
            k_past = jnp.where(opens, km_ref[0, :, lanes], k_ref[0, pl.ds(past0, WINDOW), lanes])
            v_past = jnp.where(opens, vm_ref[0, :, lanes], v_ref[0, pl.ds(past0, WINDOW), lanes])
            kd = jnp.concatenate([k_past, k_ref[0, pl.ds(cur0, WINDOW), lanes]], axis=0)
            vd = jnp.concatenate([v_past, v_ref[0, pl.ds(cur0, WINDOW), lanes]], axis=0)
            return jnp.concatenate([kd, kd], axis=1), jnp.concatenate([vd, vd], axis=1)

        def scores(kv, k4):
            g0 = kv * GROUP * HEAD_DIM
            qst = jnp.concatenate(
                [q_ref[q_rows, g0:g0 + CHUNK], q_ref[q_rows, g0 + CHUNK:g0 + 2 * CHUNK]], axis=0)
            return [_dot_nt(qst, k4 * slot_mask[s]) for s in range(SLOTS)]

        def weights(kv, s, sc):
            sf = jnp.where(prev, sc[:, :WINDOW], sc[:, WINDOW:])
            sf = jnp.where(visible, sf, NEG)
            sink = jnp.where(top, sink_ref[kv * GROUP + s], sink_ref[kv * GROUP + SLOTS + s])
            mx = jnp.maximum(jnp.max(sf, axis=-1, keepdims=True), sink)
            p = jnp.exp(sf - mx)
            den = jnp.sum(p, axis=-1, keepdims=True) + jnp.exp(sink - mx)
            p = p * (1.0 / den)
            p2 = jnp.concatenate([jnp.where(prev, p, zero), jnp.where(prev, zero, p)], axis=1)
            return p2.astype(BF16)

        def outputs(kv, ps, v4):
            acc = jnp.zeros((rows2, CHUNK), F32)
            for s in range(SLOTS):
                acc = acc + _dot(ps[s], v4 * slot_mask[s])
            g0 = kv * GROUP * HEAD_DIM
            oh_ref[q_rows, g0:g0 + CHUNK] = acc[:WINDOW].astype(BF16)
            oh_ref[q_rows, g0 + CHUNK:g0 + 2 * CHUNK] = acc[WINDOW:].astype(BF16)

        kvs = [keys_values(kv) for kv in range(N_KV_HEADS)]
        sc = scores(0, kvs[0][0])
        ps = None
        for kv in range(N_KV_HEADS):
            sc_next = scores(kv + 1, kvs[kv + 1][0]) if kv + 1 < N_KV_HEADS else None
            if ps is not None:
                outputs(kv - 1, ps, kvs[kv - 1][1])
            ps = [weights(kv, s, sc[s]) for s in range(SLOTS)]
            sc = sc_next
        outputs(N_KV_HEADS - 1, ps, kvs[N_KV_HEADS - 1][1])

    for i in range(nblk):
        block(i)
    o_ref[...] = x_ref[...] + _dot(oh_ref[...], wo_ref[...])


def _attn_main(sinks, q, kd, vd, km, vm, x, wo, layer, *, tq):
    nt = SEQ // tq
    row = lambda b, t: (b * nt + t, 0)
    batch = lambda b, t: (b, 0, 0)
    single = pl.Buffered(1)
    return pl.pallas_call(
        functools.partial(_attn_main_body, nblk=tq // WINDOW),
        grid=(BATCH, nt),
        in_specs=[
            pl.BlockSpec(memory_space=pltpu.SMEM),
            pl.BlockSpec((tq, D_MODEL), row),
            pl.BlockSpec((1, SEQ, 2 * KV_DIM), batch, pipeline_mode=single),
            pl.BlockSpec((1, SEQ, 2 * KV_DIM), batch, pipeline_mode=single),
            pl.BlockSpec((1, WINDOW, 2 * KV_DIM), batch),
            pl.BlockSpec((1, WINDOW, 2 * KV_DIM), batch),
            pl.BlockSpec((tq, D_MODEL), row),
            pl.BlockSpec((None, D_MODEL, D_MODEL), lambda b, t: (layer, 0, 0), pipeline_mode=single),
        ],
        out_specs=pl.BlockSpec((tq, D_MODEL), row),
        out_shape=jax.ShapeDtypeStruct((N_MAIN, D_MODEL), F32),
        scratch_shapes=[pltpu.VMEM((tq, D_MODEL), BF16)],
        compiler_params=pltpu.CompilerParams(
            dimension_semantics=("arbitrary", "arbitrary"), vmem_limit_bytes=VMEM_LIMIT),
        name="attn_main",
    )(sinks, q, kd, vd, km, vm, x, wo)


def _attn_small_body(sink_ref, q_ref, k_ref, v_ref, ck_ref, cv_ref, kn_ref, vn_ref, x_ref, wo_ref,
                     o_ref, nk_ref, nv_ref, oh_ref):
    nm = BATCH * N_META
    r = lax.broadcasted_iota(jnp.int32, (nm, nm), 0)
    c = lax.broadcasted_iota(jnp.int32, (nm, nm), 1)
    mmask = (c <= r) & ((r < N_META) == (c < N_META))
    km = k_ref[0:nm, :].astype(BF16)
    vm = v_ref[0:nm, :].astype(BF16)
    for h in range(N_HEADS):
        kv = h // GROUP
        hs = slice(h * HEAD_DIM, (h + 1) * HEAD_DIM)
        ks = slice(kv * HEAD_DIM, (kv + 1) * HEAD_DIM)
        s = jnp.where(mmask, _dot_nt(q_ref[0:nm, hs], km[:, ks]), NEG)
        oh_ref[0:nm, hs] = _softmax_av(s, sink_ref[h], vm[:, ks]).astype(BF16)

    nq = GROUP * DEC_BATCH
    wide = DEC_BATCH * HEAD_DIM
    own = (lax.broadcasted_iota(jnp.int32, (nq, wide), 0) % DEC_BATCH
           == lax.broadcasted_iota(jnp.int32, (nq, wide), 1) // HEAD_DIM).astype(F32)
    live = lax.broadcasted_iota(jnp.int32, (nq, WINDOW), 1) >= 1
    lower = lax.broadcasted_iota(jnp.int32, (1, LANES), 1) < HEAD_DIM
    for kv in range(N_KV_HEADS):
        ks = slice(kv * HEAD_DIM, (kv + 1) * HEAD_DIM)
        heads = range(kv * GROUP, (kv + 1) * GROUP)
        kts = ck_ref[:, kv].reshape(wide, WINDOW).astype(BF16)
        vts = cv_ref[:, kv].reshape(wide, WINDOW).astype(BF16)
        twice = []
        for h in heads[::2]:
            pair = q_ref[nm:, h * HEAD_DIM:(h + 2) * HEAD_DIM].astype(F32)
            swapped = pltpu.roll(pair, HEAD_DIM, 1)
            twice += [jnp.where(lower, pair, swapped), jnp.where(lower, swapped, pair)]
        qd = jnp.concatenate(twice, axis=0)
        qs = qd[:, :HEAD_DIM]
        qblk = (jnp.tile(qd, (1, wide // LANES)) * own).astype(BF16)
        sink = jnp.concatenate(
            [jnp.full((DEC_BATCH, 1), sink_ref[h], F32) for h in heads], axis=0)
        k_new = jnp.tile(k_ref[nm:, ks], (GROUP, 1))
        v_new = jnp.tile(v_ref[nm:, ks], (GROUP, 1))
        s = jnp.where(live, _dot(qblk, kts), NEG)
        s_new = jnp.sum(qs * k_new, axis=-1, keepdims=True)
        mx = jnp.maximum(jnp.maximum(jnp.max(s, axis=-1, keepdims=True), s_new), sink)
        p = jnp.exp(s - mx)
        p_new = jnp.exp(s_new - mx)
        den = jnp.sum(p, axis=-1, keepdims=True) + p_new + jnp.exp(sink - mx)
        spread = _dot_nt(p.astype(BF16), vts) * own
        folded = spread[:, :LANES]
        for t in range(1, wide // LANES):
            folded = folded + spread[:, t * LANES:(t + 1) * LANES]
        pv = (folded + pltpu.roll(folded, HEAD_DIM, 1))[:, :HEAD_DIM]
        o = (pv + p_new * v_new) / den
        for gi, h in enumerate(heads):
            oh_ref[nm:, h * HEAD_DIM:(h + 1) * HEAD_DIM] = (
                o[gi * DEC_BATCH:(gi + 1) * DEC_BATCH].astype(BF16))

    o_ref[...] = x_ref[...] + _dot(oh_ref[...], wo_ref[...])

    last = lax.broadcasted_iota(jnp.int32, (KV_DIM, WINDOW), 1) == WINDOW - 1

    def shift(b, carry):
        rows = pl.ds(pl.multiple_of(b * KV_DIM, KV_DIM), KV_DIM)
        for cache, new, out in ((ck_ref, kn_ref, nk_ref), (cv_ref, vn_ref, nv_ref)):
            rolled = pltpu.roll(cache[b].reshape(KV_DIM, WINDOW), WINDOW - 1, 1)
            out[b] = jnp.where(last, new[rows, :], rolled).reshape(N_KV_HEADS, HEAD_DIM, WINDOW)
        return carry

    lax.fori_loop(0, DEC_BATCH, shift, 0)


def _attn_small(sinks, q, k, v, ck, cv, kn, vn, x, wo, layer):
    single = pl.Buffered(1)
    cache_shape = (DEC_BATCH, N_KV_HEADS, HEAD_DIM, WINDOW)

    def whole(shape):
        return pl.BlockSpec(shape, lambda i: (0,) * len(shape), pipeline_mode=single)

    cache = pl.BlockSpec((None,) + cache_shape, lambda i: (layer, 0, 0, 0, 0), pipeline_mode=single)
    return pl.pallas_call(
        _attn_small_body,
        grid=(1,),
        in_specs=[pl.BlockSpec(memory_space=pltpu.SMEM)]
        + [whole(a.shape) for a in (q, k, v)] + [cache, cache] + [whole(a.shape) for a in (kn, vn, x)]
        + [pl.BlockSpec((None, D_MODEL, D_MODEL), lambda i: (layer, 0, 0), pipeline_mode=single)],
        out_specs=[whole((N_SMALL, D_MODEL)), whole(cache_shape), whole(cache_shape)],
        out_shape=[
            jax.ShapeDtypeStruct((N_SMALL, D_MODEL), F32),
            jax.ShapeDtypeStruct(cache_shape, F32),
            jax.ShapeDtypeStruct(cache_shape, F32),
        ],
        scratch_shapes=[pltpu.VMEM((N_SMALL, D_MODEL), BF16)],
        compiler_params=pltpu.CompilerParams(vmem_limit_bytes=VMEM_LIMIT),
        name="attn_small",
    )(sinks, q, k, v, ck, cv, kn, vn, x, wo)


def _rope_tables(pos):
    half = ROT_DIM // 2
    n = pos.shape[0]
    inv = jnp.float32(ROPE_THETA) ** (-jnp.arange(half, dtype=F32) * 2.0 / ROT_DIM)
    ang = pos.astype(F32)[:, None] * inv[None, :]
    cos, sin = jnp.cos(ang), jnp.sin(ang)
    rest = HEAD_DIM - ROT_DIM
    cos64 = jnp.concatenate([cos, cos, jnp.ones((n, rest), F32)], axis=1)
    sa64 = jnp.concatenate([-sin, jnp.zeros((n, half + rest), F32)], axis=1)
    sb64 = jnp.concatenate([jnp.zeros((n, half), F32), sin, jnp.zeros((n, rest), F32)], axis=1)
    rep = LANES // HEAD_DIM
    return tuple(jnp.tile(t, (1, rep)) for t in (cos64, sa64, sb64))


def kernel(x_prompt, x_sample, state_pool, cache_k, cache_v, meta_tokens, norm_mix, norm_ffn,
           pool_w, pool_scale, w_qkv, w_o, q_norm, k_norm, sinks, w_gate, w_up, w_down):
    assert x_prompt.shape == (BATCH, SEQ, D_MODEL) and x_sample.shape == (DEC_BATCH, 1, D_MODEL)
    nm = BATCH * N_META
    meta = meta_tokens.astype(F32)
    xm = x_prompt.reshape(N_MAIN, D_MODEL)
    xs = jnp.concatenate([meta, meta, x_sample.reshape(DEC_BATCH, D_MODEL)], axis=0)

    tabs_main = _rope_tables(N_META + jnp.arange(SEQ))
    tabs_small = _rope_tables(jnp.concatenate(
        [jnp.arange(N_META), jnp.arange(N_META), jnp.full((DEC_BATCH,), PAST_LEN)]))
    head_of_col = jnp.arange(QKV_CHUNK) // HEAD_DIM
    ones = (head_of_col[:, None] == head_of_col[None, :]).astype(BF16)

    wo = w_o.astype(BF16)
    ck = jnp.transpose(cache_k, (0, 1, 3, 4, 2))
    cv = jnp.transpose(cache_v, (0, 1, 3, 4, 2))

    pool_p, pool_s, kp_l, vp_l, ks_l, vs_l = [], [], [], [], [], []
    for i in range(DEPTH):
        j = i // 2
        g_mix = norm_mix[i].reshape(1, D_MODEL)
        if i % 2 == 0:
            w = pool_w[j].astype(BF16)
            sc = pool_scale[j].reshape(1, D_MODEL)
            st = jnp.swapaxes(state_pool[j], 0, 1)
            xm_new, h_last = _pool_main(xm, xs, g_mix, w, sc, tp=512)
            xs, ns = _pool_small(xs, st, g_mix, w, sc)
            xm = xm_new
            pool_p.append(h_last[:, N_META - POOL_STATE:])
            pool_s.append(jnp.swapaxes(ns, 0, 1))
        else:
            gain = jnp.concatenate(
                [jnp.tile(q_norm[j], N_HEADS), jnp.tile(k_norm[j], N_KV_HEADS)]).reshape(1, QK_DIM)
            (q_m, k_m, v_m, kd_m, vd_m, q_s, k_s, v_s, kd_s, vd_s) = _qkv(
                xm, xs, g_mix, w_qkv, j, gain, tabs_main, tabs_small, ones, tm=512)

            def meta_past(t):
                t = t[:nm].reshape(BATCH, N_META, 2 * KV_DIM)
                return jnp.pad(t, ((0, 0), (WINDOW - N_META, 0), (0, 0)))

            kn = k_s[nm:].reshape(DEC_BATCH * KV_DIM, 1)
            vn = v_s[nm:].reshape(DEC_BATCH * KV_DIM, 1)
            xm = _attn_main(sinks[j], q_m, kd_m.reshape(BATCH, SEQ, 2 * KV_DIM),
                            vd_m.reshape(BATCH, SEQ, 2 * KV_DIM), meta_past(kd_s), meta_past(vd_s),
                            xm, wo, j, tq=512)
            xs, nk, nv = _attn_small(sinks[j], q_s, k_s, v_s, ck, cv, kn, vn, xs, wo, j)
            last = lambda t: t.reshape(BATCH, SEQ, KV_DIM)[:, -WINDOW:].reshape(
                BATCH, WINDOW, N_KV_HEADS, HEAD_DIM)
            kp_l.append(last(k_m))
            vp_l.append(last(v_m))
            ks_l.append(jnp.transpose(nk, (0, 3, 1, 2)))
            vs_l.append(jnp.transpose(nv, (0, 3, 1, 2)))
        g_ffn = norm_ffn[i].reshape(1, D_MODEL)
        xm, xs = _ffn(xm, xs, g_ffn, w_gate, w_up, w_down, i, tm=1024, tf=512)

    return (xm.reshape(BATCH, SEQ, D_MODEL), xs[nm:].reshape(DEC_BATCH, 1, D_MODEL),
            jnp.stack(pool_p), jnp.stack(kp_l), jnp.stack(vp_l),
            jnp.stack(pool_s), jnp.stack(ks_l), jnp.stack(vs_l))
```

```python
import functools

import jax
import jax.numpy as jnp
from jax import lax
from jax.experimental import pallas as pl
from jax.experimental.pallas import tpu as pltpu

F32 = jnp.float32
BF16 = jnp.bfloat16

D_MODEL = 2048
BATCH = 2
SEQ = 4096
DEPTH = 4
DEC_BATCH = 32
PAST_LEN = 16384
N_META = 16
POOL_WINDOWS = (2, 4, 8, 16)
N_POOL_GROUPS = len(POOL_WINDOWS)
POOL_GROUP = D_MODEL // N_POOL_GROUPS
POOL_STATE = max(POOL_WINDOWS) - 1
HEAD_DIM = 64
N_HEADS = D_MODEL // HEAD_DIM
N_KV_HEADS = 4
GROUP = N_HEADS // N_KV_HEADS
WINDOW = 128
ROT_DIM = HEAD_DIM // 4
ROPE_THETA = 500000.0
D_FF = 5632
EPS = 1e-6
NEG = -1e30

N_MAIN = BATCH * SEQ
N_SMALL = BATCH * N_META + DEC_BATCH
KV_DIM = N_KV_HEADS * HEAD_DIM
QK_DIM = D_MODEL + KV_DIM
QKV_DIM = D_MODEL + 2 * KV_DIM
LANES = 128

VMEM_LIMIT = 60 * 1024 * 1024


def _rms(x, g):
    ms = jnp.mean(x * x, axis=-1, keepdims=True)
    return x * lax.rsqrt(ms + EPS) * g


def _dot(a, b):
    return jnp.dot(a, b, preferred_element_type=F32)


def _dot_nt(a, b):
    return lax.dot_general(a, b, (((1,), (1,)), ((), ())), preferred_element_type=F32)


def _split_bf16(x):
    hi = x.astype(BF16)
    lo = (x - hi.astype(F32)).astype(BF16)
    return hi, lo


def _ffn_body(x_ref, xs_ref, g_ref, wg_ref, wu_ref, wd_ref, o_hbm, os_ref, h_ref, acc_ref, sem):
    i, j = pl.program_id(0), pl.program_id(1)
    last_i, last_j = pl.num_programs(0) - 1, pl.num_programs(1) - 1
    tm, ms = x_ref.shape[0], xs_ref.shape[0]

    def writeback(tile):
        rows = pl.ds(pl.multiple_of(tile * tm, tm), tm)
        return pltpu.make_async_copy(acc_ref, o_hbm.at[rows, :], sem.at[0])

    def normalise(src_ref, row0):
        h_ref[row0:row0 + src_ref.shape[0], :] = _rms(src_ref[...], g_ref[...]).astype(BF16)

    def accumulate(rows, first):
        h = h_ref[0:rows, :]
        gate = _dot(h, wg_ref[...].astype(BF16))
        up = _dot(h, wu_ref[...].astype(BF16))
        act = (gate * jax.nn.sigmoid(gate) * up).astype(BF16)
        if first and rows == tm:
            writeback(i - 1).wait()
        y = _dot(act, wd_ref[...].astype(BF16))
        acc_ref[...] = (x_ref[...] if first else acc_ref[...]) + y[:tm]
        if rows > tm:
            os_ref[...] = (xs_ref[...] if first else os_ref[...]) + y[tm:]

    pl.when(j == 0)(lambda: normalise(x_ref, 0))
    pl.when((i == 0) & (j == 0))(lambda: normalise(xs_ref, tm))
    pl.when((i == 0) & (j == 0))(lambda: accumulate(tm + ms, True))
    pl.when((i == 0) & (j > 0))(lambda: accumulate(tm + ms, False))
    pl.when((i > 0) & (j == 0))(lambda: accumulate(tm, True))
    pl.when((i > 0) & (j > 0))(lambda: accumulate(tm, False))
    pl.when(j == last_j)(lambda: writeback(i).start())
    pl.when((i == last_i) & (j == last_j))(lambda: writeback(i).wait())


def _ffn(x, xs, g, wg, wu, wd, layer, *, tm, tf):
    m, ms = x.shape[0], xs.shape[0]
    return pl.pallas_call(
        _ffn_body,
        grid=(m // tm, D_FF // tf),
        in_specs=[
            pl.BlockSpec((tm, D_MODEL), lambda i, j: (i, 0)),
            pl.BlockSpec((ms, D_MODEL), lambda i, j: (0, 0)),
            pl.BlockSpec((1, D_MODEL), lambda i, j: (0, 0)),
            pl.BlockSpec((None, D_MODEL, tf), lambda i, j: (layer, 0, j)),
            pl.BlockSpec((None, D_MODEL, tf), lambda i, j: (layer, 0, j)),
            pl.BlockSpec((None, tf, D_MODEL), lambda i, j: (layer, j, 0)),
        ],
        out_specs=[
            pl.BlockSpec(memory_space=pl.ANY),
            pl.BlockSpec((ms, D_MODEL), lambda i, j: (0, 0)),
        ],
        out_shape=[
            jax.ShapeDtypeStruct((m, D_MODEL), F32),
            jax.ShapeDtypeStruct((ms, D_MODEL), F32),
        ],
        scratch_shapes=[
            pltpu.VMEM((tm + ms, D_MODEL), BF16),
            pltpu.VMEM((tm, D_MODEL), F32),
            pltpu.SemaphoreType.DMA((1,)),
        ],
        compiler_params=pltpu.CompilerParams(
            dimension_semantics=("arbitrary", "arbitrary"), vmem_limit_bytes=VMEM_LIMIT),
        name="ffn",
    )(x, xs, g, wg, wu, wd)


def _pool_project(diffs, w_ref, sc_ref):
    ys = [_dot(d.astype(BF16), w_ref[g]) for g, d in enumerate(diffs)]
    return jnp.concatenate(ys, axis=1) * sc_ref[...]


def _pool_main_body(x_ref, meta_ref, g_ref, w_ref, sc_ref, o_ref, hl_ref, ext_ref, *, tp):
    t = pl.program_id(1)

    @pl.when(t == 0)
    def _():
        ext_ref[0:N_META, :] = _rms(meta_ref[...], g_ref[...])

    x = x_ref[...]
    h = _rms(x, g_ref[...])
    ext_ref[N_META:N_META + tp, :] = h
    diffs = []
    for g, w in enumerate(POOL_WINDOWS):
        c0, c1 = g * POOL_GROUP, (g + 1) * POOL_GROUP
        s = ext_ref[:, c0:c1]
        shift = 1
        while shift < w:
            s = s + pltpu.roll(s, shift, 0)
            shift *= 2
        diffs.append(s[N_META:] * (1.0 / w) - h[:, c0:c1])
    o_ref[...] = x + _pool_project(diffs, w_ref, sc_ref)
    tail = ext_ref[tp:tp + N_META, :]
    ext_ref[0:N_META, :] = tail

    @pl.when(t == pl.num_programs(1) - 1)
    def _():
        hl_ref[0] = tail


def _pool_main(xm, xs, g, w, sc, *, tp):
    nt = SEQ // tp
    return pl.pallas_call(
        functools.partial(_pool_main_body, tp=tp),
        grid=(BATCH, nt),
        in_specs=[
            pl.BlockSpec((tp, D_MODEL), lambda b, t: (b * nt + t, 0)),
            pl.BlockSpec((N_META, D_MODEL), lambda b, t: (b, 0)),
            pl.BlockSpec((1, D_MODEL), lambda b, t: (0, 0)),
            pl.BlockSpec((N_POOL_GROUPS, POOL_GROUP, POOL_GROUP), lambda b, t: (0, 0, 0)),
            pl.BlockSpec((1, D_MODEL), lambda b, t: (0, 0)),
        ],
        out_specs=[
            pl.BlockSpec((tp, D_MODEL), lambda b, t: (b * nt + t, 0)),
            pl.BlockSpec((1, N_META, D_MODEL), lambda b, t: (b, 0, 0)),
        ],
        out_shape=[
            jax.ShapeDtypeStruct((N_MAIN, D_MODEL), F32),
            jax.ShapeDtypeStruct((BATCH, N_META, D_MODEL), F32),
        ],
        scratch_shapes=[pltpu.VMEM((tp + N_META, D_MODEL), F32)],
        compiler_params=pltpu.CompilerParams(
            dimension_semantics=("arbitrary", "arbitrary"), vmem_limit_bytes=VMEM_LIMIT),
        name="pool_main",
    )(xm, xs, g, w, sc)


def _pool_small_body(x_ref, st_ref, g_ref, w_ref, sc_ref, o_ref, ns_ref, ext_ref):
    nm = BATCH * N_META
    x = x_ref[...]
    h = _rms(x, g_ref[...])
    for b in range(BATCH):
        ext_ref[b, 0:N_META, :] = jnp.zeros((N_META, D_MODEL), F32)
        ext_ref[b, N_META:2 * N_META, :] = h[b * N_META:(b + 1) * N_META]
    hs = h[nm:]
    row = lax.broadcasted_iota(jnp.int32, (N_META, 1), 0)
    diffs = []
    for g, w in enumerate(POOL_WINDOWS):
        c0, c1 = g * POOL_GROUP, (g + 1) * POOL_GROUP
        cnt = jnp.minimum(row + 1, w).astype(F32)
        parts = []
        for b in range(BATCH):
            s = ext_ref[b, N_META:2 * N_META, c0:c1]
            for k in range(1, w):
                s = s + ext_ref[b, N_META - k:2 * N_META - k, c0:c1]
            parts.append(s / cnt)
        s = hs[:, c0:c1]
        for k in range(1, w):
            s = s + st_ref[POOL_STATE - k, :, c0:c1]
        parts.append(s / float(w))
        diffs.append(jnp.concatenate(parts, axis=0) - h[:, c0:c1])
    o_ref[...] = x + _pool_project(diffs, w_ref, sc_ref)
    ns_ref[0:POOL_STATE - 1] = st_ref[1:POOL_STATE]
    ns_ref[POOL_STATE - 1] = hs


def _pool_small(xs, st, g, w, sc):
    return pl.pallas_call(
        _pool_small_body,
        out_shape=[
            jax.ShapeDtypeStruct((N_SMALL, D_MODEL), F32),
            jax.ShapeDtypeStruct((POOL_STATE, DEC_BATCH, D_MODEL), F32),
        ],
        scratch_shapes=[pltpu.VMEM((BATCH, 2 * N_META, D_MODEL), F32)],
        compiler_params=pltpu.CompilerParams(vmem_limit_bytes=VMEM_LIMIT),
        name="pool_small",
    )(xs, st, g, w, sc)


QKV_CHUNK = 4 * HEAD_DIM


def _qkv_rows(x_ref, tab_refs, g_ref, w_ref, gain_ref, ones_ref, out_refs, h_ref):
    cos_ref, sa_ref, sb_ref = tab_refs
    q_ref, k_ref, v_ref, kd_ref, vd_ref = out_refs
    h_ref[...] = _rms(x_ref[...], g_ref[...]).astype(BF16)
    cos, sa, sb = cos_ref[...], sa_ref[...], sb_ref[...]
    half = ROT_DIM // 2

    def norm_rope(a, c0):
        sq_hi, sq_lo = _split_bf16(a * a)
        ss = _dot(sq_hi, ones_ref[...]) + _dot(sq_lo, ones_ref[...])
        y = a * lax.rsqrt(ss * (1.0 / HEAD_DIM) + EPS) * gain_ref[:, c0:c0 + QKV_CHUNK]
        tiles = []
        for u in range(QKV_CHUNK // LANES):
            yu = y[:, u * LANES:(u + 1) * LANES]
            tiles.append(yu * cos + pltpu.roll(yu, LANES - half, 1) * sa + pltpu.roll(yu, half, 1) * sb)
        return jnp.concatenate(tiles, axis=1)

    lower = lax.broadcasted_iota(jnp.int32, (1, LANES), 1) < HEAD_DIM

    def twice(t):
        tiles = []
        for u in range(KV_DIM // LANES):
            tu = t[:, u * LANES:(u + 1) * LANES]
            swapped = pltpu.roll(tu, HEAD_DIM, 1)
            tiles += [jnp.where(lower, tu, swapped), jnp.where(lower, swapped, tu)]
        return jnp.concatenate(tiles, axis=1).astype(BF16)

    wide = 2 * QKV_CHUNK
    nwide = QKV_DIM // wide
    project = lambda c: _dot(h_ref[...], w_ref[:, c * wide:(c + 1) * wide])
    a = project(0)
    for c in range(nwide):
        a_next = project(c + 1) if c + 1 < nwide else None
        c0 = c * wide
        if c0 < D_MODEL:
            for u in range(2):
                y = norm_rope(a[:, u * QKV_CHUNK:(u + 1) * QKV_CHUNK], c0 + u * QKV_CHUNK)
                q_ref[:, c0 + u * QKV_CHUNK:c0 + (u + 1) * QKV_CHUNK] = (
                    y * HEAD_DIM ** -0.5).astype(BF16)
        else:
            k = norm_rope(a[:, :KV_DIM], D_MODEL)
            v = a[:, KV_DIM:]
            k_ref[...] = k
            v_ref[...] = v
            kd_ref[...] = twice(k)
            vd_ref[...] = twice(v)
        a = a_next


def _qkv_body(*refs):
    x_ref, xs_ref, g_ref, w_ref, gain_ref = refs[:5]
    tabs, tabs_s, ones_ref = refs[5:8], refs[8:11], refs[11]
    outs, outs_s = refs[12:17], refs[17:22]
    wb_ref, h_ref, hs_ref = refs[22:]

    @pl.when(pl.program_id(0) == 0)
    def _():
        wb_ref[...] = w_ref[...].astype(BF16)
        _qkv_rows(xs_ref, tabs_s, g_ref, wb_ref, gain_ref, ones_ref, outs_s, hs_ref)

    half = x_ref.shape[0] // 2
    for r in range(2):
        rows = pl.ds(r * half, half)
        _qkv_rows(x_ref.at[rows], [t.at[rows] for t in tabs], g_ref, wb_ref, gain_ref, ones_ref,
                  [o.at[rows] for o in outs], h_ref.at[rows])


def _qkv(x, xs, g, w, layer, gain, tabs, tabs_s, ones, *, tm):
    m, ms = x.shape[0], xs.shape[0]
    nt = tabs[0].shape[0] // tm
    row = lambda i: (i, 0)
    tab = lambda i: (i % nt, 0)
    fixed = lambda i: (0, 0)
    widths = ((D_MODEL, BF16), (KV_DIM, F32), (KV_DIM, F32), (2 * KV_DIM, BF16), (2 * KV_DIM, BF16))
    return pl.pallas_call(
        _qkv_body,
        grid=(m // tm,),
        in_specs=[
            pl.BlockSpec((tm, D_MODEL), row),
            pl.BlockSpec((ms, D_MODEL), fixed),
            pl.BlockSpec((1, D_MODEL), fixed),
            pl.BlockSpec((None, D_MODEL, QKV_DIM), lambda i: (layer, 0, 0),
                         pipeline_mode=pl.Buffered(1)),
            pl.BlockSpec((1, QK_DIM), fixed),
        ]
        + [pl.BlockSpec((tm, LANES), tab)] * 3
        + [pl.BlockSpec((ms, LANES), fixed)] * 3
        + [pl.BlockSpec((QKV_CHUNK, QKV_CHUNK), fixed)],
        out_specs=[pl.BlockSpec((tm, n), row) for n, _ in widths]
        + [pl.BlockSpec((ms, n), fixed) for n, _ in widths],
        out_shape=[jax.ShapeDtypeStruct((m, n), dt) for n, dt in widths]
        + [jax.ShapeDtypeStruct((ms, n), dt) for n, dt in widths],
        scratch_shapes=[
            pltpu.VMEM((D_MODEL, QKV_DIM), BF16),
            pltpu.VMEM((tm, D_MODEL), BF16),
            pltpu.VMEM((ms, D_MODEL), BF16),
        ],
        compiler_params=pltpu.CompilerParams(
            dimension_semantics=("arbitrary",), vmem_limit_bytes=VMEM_LIMIT),
        name="qkv",
    )(x, xs, g, w, gain, *tabs, *tabs_s, ones)


def _softmax_av(s, sink, v):
    mx = jnp.maximum(jnp.max(s, axis=-1, keepdims=True), sink)
    p = jnp.exp(s - mx)
    den = jnp.sum(p, axis=-1, keepdims=True) + jnp.exp(sink - mx)
    return _dot(p.astype(BF16), v) / den


SLOTS = 4
CHUNK = SLOTS * HEAD_DIM


def _attn_main_body(sink_ref, q_ref, k_ref, v_ref, km_ref, vm_ref, x_ref, wo_ref, o_ref, oh_ref,
                    *, nblk):
    rows2 = 2 * WINDOW
    r = lax.broadcasted_iota(jnp.int32, (rows2, WINDOW), 0) % WINDOW
    c = lax.broadcasted_iota(jnp.int32, (rows2, WINDOW), 1)
    prev = c > r
    top = lax.broadcasted_iota(jnp.int32, (rows2, 1), 0) < WINDOW
    lane_slot = lax.broadcasted_iota(jnp.int32, (rows2, CHUNK), 1) // HEAD_DIM
    slot_mask = [(lane_slot == s).astype(F32).astype(BF16) for s in range(SLOTS)]
    zero = jnp.zeros((rows2, WINDOW), F32)
    step = pl.program_id(1)
    seen = jnp.logical_not(prev) | (c >= WINDOW - N_META)

    def block(i):
        row0 = i * WINDOW
        cur0 = pl.multiple_of(step * (nblk * WINDOW) + row0, WINDOW)
        past0 = pl.multiple_of(jnp.maximum(cur0 - WINDOW, 0), WINDOW)
        opens = cur0 == 0
        visible = seen | jnp.logical_not(opens)
        q_rows = pl.ds(row0, WINDOW)

        def keys_values(kv):
            lanes = slice(kv * LANES, (kv + 1) * LANES)
            k_past = jnp.where(opens, km_ref[0, :, lanes], k_ref[0, pl.ds(past0, WINDOW), lanes])
            v_past = jnp.where(opens, vm_ref[0, :, lanes], v_ref[0, pl.ds(past0, WINDOW), lanes])
            kd = jnp.concatenate([k_past, k_ref[0, pl.ds(cur0, WINDOW), lanes]], axis=0)
            vd = jnp.concatenate([v_past, v_ref[0, pl.ds(cur0, WINDOW), lanes]], axis=0)
            return jnp.concatenate([kd, kd], axis=1), jnp.concatenate([vd, vd], axis=1)

        def scores(kv, k4):
            g0 = kv * GROUP * HEAD_DIM
            qst = jnp.concatenate(
                [q_ref[q_rows, g0:g0 + CHUNK], q_ref[q_rows, g0 + CHUNK:g0 + 2 * CHUNK]], axis=0)
            return [_dot_nt(qst, k4 * slot_mask[s]) for s in range(SLOTS)]

        def weights(kv, s, sc):
            sf = jnp.where(prev, sc[:, :WINDOW], sc[:, WINDOW:])
            sf = jnp.where(visible, sf, NEG)
            sink = jnp.where(top, sink_ref[kv * GROUP + s], sink_ref[kv * GROUP + SLOTS + s])
            mx = jnp.maximum(jnp.max(sf, axis=-1, keepdims=True), sink)
            p = jnp.exp(sf - mx)
            den = jnp.sum(p, axis=-1, keepdims=True) + jnp.exp(sink - mx)
            p = p * (1.0 / den)
            p2 = jnp.concatenate([jnp.where(prev, p, zero), jnp.where(prev, zero, p)], axis=1)
            return p2.astype(BF16)

        def outputs(kv, ps, v4):
            acc = jnp.zeros((rows2, CHUNK), F32)
            for s in range(SLOTS):
                acc = acc + _dot(ps[s], v4 * slot_mask[s])
            g0 = kv * GROUP * HEAD_DIM
            oh_ref[q_rows, g0:g0 + CHUNK] = acc[:WINDOW].astype(BF16)
            oh_ref[q_rows, g0 + CHUNK:g0 + 2 * CHUNK] = acc[WINDOW:].astype(BF16)

        kvs = [keys_values(kv) for kv in range(N_KV_HEADS)]
        sc = scores(0, kvs[0][0])
        ps = None
        for kv in range(N_KV_HEADS):
            sc_next = scores(kv + 1, kvs[kv + 1][0]) if kv + 1 < N_KV_HEADS else None
            if ps is not None:
                outputs(kv - 1, ps, kvs[kv - 1][1])
            ps = [weights(kv, s, sc[s]) for s in range(SLOTS)]
            sc = sc_next
        outputs(N_KV_HEADS - 1, ps, kvs[N_KV_HEADS - 1][1])

    for i in range(nblk):
        block(i)
    o_ref[...] = x_ref[...] + _dot(oh_ref[...], wo_ref[...])


def _attn_main(sinks, q, kd, vd, km, vm, x, wo, layer, *, tq):
    nt = SEQ // tq
    row = lambda b, t: (b * nt + t, 0)
    batch = lambda b, t: (b, 0, 0)
    single = pl.Buffered(1)
    return pl.pallas_call(
        functools.partial(_attn_main_body, nblk=tq // WINDOW),
        grid=(BATCH, nt),
        in_specs=[
            pl.BlockSpec(memory_space=pltpu.SMEM),
            pl.BlockSpec((tq, D_MODEL), row),
            pl.BlockSpec((1, SEQ, 2 * KV_DIM), batch, pipeline_mode=single),
            pl.BlockSpec((1, SEQ, 2 * KV_DIM), batch, pipeline_mode=single),
            pl.BlockSpec((1, WINDOW, 2 * KV_DIM), batch),
            pl.BlockSpec((1, WINDOW, 2 * KV_DIM), batch),
            pl.BlockSpec((tq, D_MODEL), row),
            pl.BlockSpec((None, D_MODEL, D_MODEL), lambda b, t: (layer, 0, 0), pipeline_mode=single),
        ],
        out_specs=pl.BlockSpec((tq, D_MODEL), row),
        out_shape=jax.ShapeDtypeStruct((N_MAIN, D_MODEL), F32),
        scratch_shapes=[pltpu.VMEM((tq, D_MODEL), BF16)],
        compiler_params=pltpu.CompilerParams(
            dimension_semantics=("arbitrary", "arbitrary"), vmem_limit_bytes=VMEM_LIMIT),
        name="attn_main",
    )(sinks, q, kd, vd, km, vm, x, wo)


def _attn_small_body(sink_ref, q_ref, k_ref, v_ref, ck_ref, cv_ref, kn_ref, vn_ref, x_ref, wo_ref,
                     o_ref, nk_ref, nv_ref, oh_ref):
    nm = BATCH * N_META
    assert nm == DEC_BATCH
    nq = GROUP * DEC_BATCH
    lower = lax.broadcasted_iota(jnp.int32, (1, LANES), 1) < HEAD_DIM

    def stacked(row0, kv):
        tiles = []
        for h in range(kv * GROUP, (kv + 1) * GROUP, 2):
            pair = q_ref[row0:row0 + DEC_BATCH, h * HEAD_DIM:(h + 2) * HEAD_DIM].astype(F32)
            swapped = pltpu.roll(pair, HEAD_DIM, 1)
            tiles += [jnp.where(lower, pair, swapped), jnp.where(lower, swapped, pair)]
        return jnp.concatenate(tiles, axis=0)

    def sink_column(kv):
        return jnp.concatenate([jnp.full((DEC_BATCH, 1), sink_ref[h], F32)
                                for h in range(kv * GROUP, (kv + 1) * GROUP)], axis=0)

    def unstack(o, row0, kv):
        for gi, h in enumerate(range(kv * GROUP, (kv + 1) * GROUP)):
            oh_ref[row0:row0 + DEC_BATCH, h * HEAD_DIM:(h + 1) * HEAD_DIM] = (
                o[gi * DEC_BATCH:(gi + 1) * DEC_BATCH].astype(BF16))

    r = lax.broadcasted_iota(jnp.int32, (nq, nm), 0) % nm
    c = lax.broadcasted_iota(jnp.int32, (nq, nm), 1)
    mmask = (c <= r) & ((r < N_META) == (c < N_META))
    km = k_ref[0:nm, :].astype(BF16)
    vm = v_ref[0:nm, :].astype(BF16)
    for kv in range(N_KV_HEADS):
        ks = slice(kv * HEAD_DIM, (kv + 1) * HEAD_DIM)
        qm = stacked(0, kv)[:, :HEAD_DIM].astype(BF16)
        s = jnp.where(mmask, _dot_nt(qm, km[:, ks]), NEG)
        unstack(_softmax_av(s, sink_column(kv), vm[:, ks]), 0, kv)

    wide = DEC_BATCH * HEAD_DIM
    own = (lax.broadcasted_iota(jnp.int32, (nq, wide), 0) % DEC_BATCH
           == lax.broadcasted_iota(jnp.int32, (nq, wide), 1) // HEAD_DIM).astype(F32)
    live = lax.broadcasted_iota(jnp.int32, (nq, WINDOW), 1) >= 1
    for kv in range(N_KV_HEADS):
        ks = slice(kv * HEAD_DIM, (kv + 1) * HEAD_DIM)
        kts = ck_ref[:, kv].reshape(wide, WINDOW).astype(BF16)
        vts = cv_ref[:, kv].reshape(wide, WINDOW).astype(BF16)
        qd = stacked(nm, kv)
        qs = qd[:, :HEAD_DIM]
        qblk = (jnp.tile(qd, (1, wide // LANES)) * own).astype(BF16)
        sink = sink_column(kv)
        k_new = jnp.tile(k_ref[nm:, ks], (GROUP, 1))
        v_new = jnp.tile(v_ref[nm:, ks], (GROUP, 1))
        s = jnp.where(live, _dot(qblk, kts), NEG)
        s_new = jnp.sum(qs * k_new, axis=-1, keepdims=True)
        mx = jnp.maximum(jnp.maximum(jnp.max(s, axis=-1, keepdims=True), s_new), sink)
        p = jnp.exp(s - mx)
        p_new = jnp.exp(s_new - mx)
        den = jnp.sum(p, axis=-1, keepdims=True) + p_new + jnp.exp(sink - mx)
        spread = _dot_nt(p.astype(BF16), vts) * own
        folded = spread[:, :LANES]
        for t in range(1, wide // LANES):
            folded = folded + spread[:, t * LANES:(t + 1) * LANES]
        pv = (folded + pltpu.roll(folded, HEAD_DIM, 1))[:, :HEAD_DIM]
        unstack((pv + p_new * v_new) / den, nm, kv)

    o_ref[...] = x_ref[...] + _dot(oh_ref[...], wo_ref[...])

    last = lax.broadcasted_iota(jnp.int32, (KV_DIM, WINDOW), 1) == WINDOW - 1
    for b in range(DEC_BATCH):
        for cache, new, out in ((ck_ref, kn_ref, nk_ref), (cv_ref, vn_ref, nv_ref)):
            rolled = pltpu.roll(cache[b].reshape(KV_DIM, WINDOW), WINDOW - 1, 1)
            out[b] = jnp.where(last, new[:, b:b + 1], rolled).reshape(N_KV_HEADS, HEAD_DIM, WINDOW)


def _attn_small(sinks, q, k, v, ck, cv, kn, vn, x, wo, layer):
    single = pl.Buffered(1)
    cache_shape = (DEC_BATCH, N_KV_HEADS, HEAD_DIM, WINDOW)

    def whole(shape):
        return pl.BlockSpec(shape, lambda i: (0,) * len(shape), pipeline_mode=single)

    cache = pl.BlockSpec((None,) + cache_shape, lambda i: (layer, 0, 0, 0, 0), pipeline_mode=single)
    return pl.pallas_call(
        _attn_small_body,
        grid=(1,),
        in_specs=[pl.BlockSpec(memory_space=pltpu.SMEM)]
        + [whole(a.shape) for a in (q, k, v)] + [cache, cache] + [whole(a.shape) for a in (kn, vn, x)]
        + [pl.BlockSpec((None, D_MODEL, D_MODEL), lambda i: (layer, 0, 0), pipeline_mode=single)],
        out_specs=[whole((N_SMALL, D_MODEL)), whole(cache_shape), whole(cache_shape)],
        out_shape=[
            jax.ShapeDtypeStruct((N_SMALL, D_MODEL), F32),
            jax.ShapeDtypeStruct(cache_shape, F32),
            jax.ShapeDtypeStruct(cache_shape, F32),
        ],
        scratch_shapes=[pltpu.VMEM((N_SMALL, D_MODEL), BF16)],
        compiler_params=pltpu.CompilerParams(vmem_limit_bytes=VMEM_LIMIT),
        name="attn_small",
    )(sinks, q, k, v, ck, cv, kn, vn, x, wo)


def _rope_tables(pos):
    half = ROT_DIM // 2
    n = pos.shape[0]
    inv = jnp.float32(ROPE_THETA) ** (-jnp.arange(half, dtype=F32) * 2.0 / ROT_DIM)
    ang = pos.astype(F32)[:, None] * inv[None, :]
    cos, sin = jnp.cos(ang), jnp.sin(ang)
    rest = HEAD_DIM - ROT_DIM
    cos64 = jnp.concatenate([cos, cos, jnp.ones((n, rest), F32)], axis=1)
    sa64 = jnp.concatenate([-sin, jnp.zeros((n, half + rest), F32)], axis=1)
    sb64 = jnp.concatenate([jnp.zeros((n, half), F32), sin, jnp.zeros((n, rest), F32)], axis=1)
    rep = LANES // HEAD_DIM
    return tuple(jnp.tile(t, (1, rep)) for t in (cos64, sa64, sb64))


def kernel(x_prompt, x_sample, state_pool, cache_k, cache_v, meta_tokens, norm_mix, norm_ffn,
           pool_w, pool_scale, w_qkv, w_o, q_norm, k_norm, sinks, w_gate, w_up, w_down):
    assert x_prompt.shape == (BATCH, SEQ, D_MODEL) and x_sample.shape == (DEC_BATCH, 1, D_MODEL)
    nm = BATCH * N_META
    meta = meta_tokens.astype(F32)
    xm = x_prompt.reshape(N_MAIN, D_MODEL)
    xs = jnp.concatenate([meta, meta, x_sample.reshape(DEC_BATCH, D_MODEL)], axis=0)

    tabs_main = _rope_tables(N_META + jnp.arange(SEQ))
    tabs_small = _rope_tables(jnp.concatenate(
        [jnp.arange(N_META), jnp.arange(N_META), jnp.full((DEC_BATCH,), PAST_LEN)]))
    head_of_col = jnp.arange(QKV_CHUNK) // HEAD_DIM
    ones = (head_of_col[:, None] == head_of_col[None, :]).astype(BF16)

    wo = w_o.astype(BF16)
    ck = jnp.transpose(cache_k, (0, 1, 3, 4, 2))
    cv = jnp.transpose(cache_v, (0, 1, 3, 4, 2))

    pool_p, pool_s, kp_l, vp_l, ks_l, vs_l = [], [], [], [], [], []
    for i in range(DEPTH):
        j = i // 2
        g_mix = norm_mix[i].reshape(1, D_MODEL)
        if i % 2 == 0:
            w = pool_w[j].astype(BF16)
            sc = pool_scale[j].reshape(1, D_MODEL)
            st = jnp.swapaxes(state_pool[j], 0, 1)
            xm_new, h_last = _pool_main(xm, xs, g_mix, w, sc, tp=512)
            xs, ns = _pool_small(xs, st, g_mix, w, sc)
            xm = xm_new
            pool_p.append(h_last[:, N_META - POOL_STATE:])
            pool_s.append(jnp.swapaxes(ns, 0, 1))
        else:
            gain = jnp.concatenate(
                [jnp.tile(q_norm[j], N_HEADS), jnp.tile(k_norm[j], N_KV_HEADS)]).reshape(1, QK_DIM)
            (q_m, k_m, v_m, kd_m, vd_m, q_s, k_s, v_s, kd_s, vd_s) = _qkv(
                xm, xs, g_mix, w_qkv, j, gain, tabs_main, tabs_small, ones, tm=512)

            def meta_past(t):
                t = t[:nm].reshape(BATCH, N_META, 2 * KV_DIM)
                return jnp.pad(t, ((0, 0), (WINDOW - N_META, 0), (0, 0)))

            kn = k_s[nm:].T
            vn = v_s[nm:].T
            xm = _attn_main(sinks[j], q_m, kd_m.reshape(BATCH, SEQ, 2 * KV_DIM),
                            vd_m.reshape(BATCH, SEQ, 2 * KV_DIM), meta_past(kd_s), meta_past(vd_s),
                            xm, wo, j, tq=512)
            xs, nk, nv = _attn_small(sinks[j], q_s, k_s, v_s, ck, cv, kn, vn, xs, wo, j)
            last = lambda t: t.reshape(BATCH, SEQ, KV_DIM)[:, -WINDOW:].reshape(
                BATCH, WINDOW, N_KV_HEADS, HEAD_DIM)
            kp_l.append(last(k_m))
            vp_l.append(last(v_m))
            ks_l.append(jnp.transpose(nk, (0, 3, 1, 2)))
            vs_l.append(jnp.transpose(nv, (0, 3, 1, 2)))
        g_ffn = norm_ffn[i].reshape(1, D_MODEL)
        xm, xs = _ffn(xm, xs, g_ffn, w_gate, w_up, w_down, i, tm=1024, tf=512)

    return (xm.reshape(BATCH, SEQ, D_MODEL), xs[nm:].reshape(DEC_BATCH, 1, D_MODEL),
            jnp.stack(pool_p), jnp.stack(kp_l), jnp.stack(vp_l),
            jnp.stack(pool_s), jnp.stack(ks_l), jnp.stack(vs_l))
```

```python
import functools

import jax
import jax.numpy as jnp
from jax import lax
from jax.experimental import pallas as pl
from jax.experimental.pallas import tpu as pltpu

F32 = jnp.float32
BF16 = jnp.bfloat16

D_MODEL = 2048
BATCH = 2
SEQ = 4096
DEPTH = 4
DEC_BATCH = 32
PAST_LEN = 16384
N_META = 16
POOL_WINDOWS = (2, 4, 8, 16)
N_POOL_GROUPS = len(POOL_WINDOWS)
POOL_GROUP = D_MODEL // N_POOL_GROUPS
POOL_STATE = max(POOL_WINDOWS) - 1
HEAD_DIM = 64
N_HEADS = D_MODEL // HEAD_DIM
N_KV_HEADS = 4
GROUP = N_HEADS // N_KV_HEADS
WINDOW = 128
ROT_DIM = HEAD_DIM // 4
ROPE_THETA = 500000.0
D_FF = 5632
EPS = 1e-6
NEG = -1e30

N_MAIN = BATCH * SEQ
N_SMALL = BATCH * N_META + DEC_BATCH
KV_DIM = N_KV_HEADS * HEAD_DIM
QK_DIM = D_MODEL + KV_DIM
QKV_DIM = D_MODEL + 2 * KV_DIM
LANES = 128

VMEM_LIMIT = 60 * 1024 * 1024


def _rms(x, g):
    ms = jnp.mean(x * x, axis=-1, keepdims=True)
    return x * lax.rsqrt(ms + EPS) * g


def _dot(a, b):
    return jnp.dot(a, b, preferred_element_type=F32)


def _dot_nt(a, b):
    return lax.dot_general(a, b, (((1,), (1,)), ((), ())), preferred_element_type=F32)


def _split_bf16(x):
    hi = x.astype(BF16)
    lo = (x - hi.astype(F32)).astype(BF16)
    return hi, lo


def _ffn_body(x_ref, xs_ref, g_ref, wg_ref, wu_ref, wd_ref, o_hbm, os_ref, h_ref, acc_ref, sem):
    i, j = pl.program_id(0), pl.program_id(1)
    last_i, last_j = pl.num_programs(0) - 1, pl.num_programs(1) - 1
    tm, ms = x_ref.shape[0], xs_ref.shape[0]

    def writeback(tile):
        rows = pl.ds(pl.multiple_of(tile * tm, tm), tm)
        return pltpu.make_async_copy(acc_ref, o_hbm.at[rows, :], sem.at[0])

    def normalise(src_ref, row0):
        h_ref[row0:row0 + src_ref.shape[0], :] = _rms(src_ref[...], g_ref[...]).astype(BF16)

    def accumulate(rows, first):
        h = h_ref[0:rows, :]
        gate = _dot(h, wg_ref[...].astype(BF16))
        up = _dot(h, wu_ref[...].astype(BF16))
        act = (gate * jax.nn.sigmoid(gate) * up).astype(BF16)
        if first and rows == tm:
            writeback(i - 1).wait()
        y = _dot(act, wd_ref[...].astype(BF16))
        acc_ref[...] = (x_ref[...] if first else acc_ref[...]) + y[:tm]
        if rows > tm:
            os_ref[...] = (xs_ref[...] if first else os_ref[...]) + y[tm:]

    pl.when(j == 0)(lambda: normalise(x_ref, 0))
    pl.when((i == 0) & (j == 0))(lambda: normalise(xs_ref, tm))
    pl.when((i == 0) & (j == 0))(lambda: accumulate(tm + ms, True))
    pl.when((i == 0) & (j > 0))(lambda: accumulate(tm + ms, False))
    pl.when((i > 0) & (j == 0))(lambda: accumulate(tm, True))
    pl.when((i > 0) & (j > 0))(lambda: accumulate(tm, False))
    pl.when(j == last_j)(lambda: writeback(i).start())
    pl.when((i == last_i) & (j == last_j))(lambda: writeback(i).wait())


def _ffn(x, xs, g, wg, wu, wd, layer, *, tm, tf):
    m, ms = x.shape[0], xs.shape[0]
    return pl.pallas_call(
        _ffn_body,
        grid=(m // tm, D_FF // tf),
        in_specs=[
            pl.BlockSpec((tm, D_MODEL), lambda i, j: (i, 0)),
            pl.BlockSpec((ms, D_MODEL), lambda i, j: (0, 0)),
            pl.BlockSpec((1, D_MODEL), lambda i, j: (0, 0)),
            pl.BlockSpec((None, D_MODEL, tf), lambda i, j: (layer, 0, j)),
            pl.BlockSpec((None, D_MODEL, tf), lambda i, j: (layer, 0, j)),
            pl.BlockSpec((None, tf, D_MODEL), lambda i, j: (layer, j, 0)),
        ],
        out_specs=[
            pl.BlockSpec(memory_space=pl.ANY),
            pl.BlockSpec((ms, D_MODEL), lambda i, j: (0, 0)),
        ],
        out_shape=[
            jax.ShapeDtypeStruct((m, D_MODEL), F32),
            jax.ShapeDtypeStruct((ms, D_MODEL), F32),
        ],
        scratch_shapes=[
            pltpu.VMEM((tm + ms, D_MODEL), BF16),
            pltpu.VMEM((tm, D_MODEL), F32),
            pltpu.SemaphoreType.DMA((1,)),
        ],
        compiler_params=pltpu.CompilerParams(
            dimension_semantics=("arbitrary", "arbitrary"), vmem_limit_bytes=VMEM_LIMIT),
        name="ffn",
    )(x, xs, g, wg, wu, wd)


def _pool_project(diffs, w_ref, sc_ref):
    ys = [_dot(d.astype(BF16), w_ref[g]) for g, d in enumerate(diffs)]
    return jnp.concatenate(ys, axis=1) * sc_ref[...]


def _pool_main_body(x_ref, meta_ref, g_ref, w_ref, sc_ref, o_ref, hl_ref, ext_ref, *, tp):
    t = pl.program_id(1)

    @pl.when(t == 0)
    def _():
        ext_ref[0:N_META, :] = _rms(meta_ref[...], g_ref[...])

    x = x_ref[...]
    h = _rms(x, g_ref[...])
    ext_ref[N_META:N_META + tp, :] = h
    diffs = []
    for g, w in enumerate(POOL_WINDOWS):
        c0, c1 = g * POOL_GROUP, (g + 1) * POOL_GROUP
        s = ext_ref[:, c0:c1]
        shift = 1
        while shift < w:
            s = s + pltpu.roll(s, shift, 0)
            shift *= 2
        diffs.append(s[N_META:] * (1.0 / w) - h[:, c0:c1])
    o_ref[...] = x + _pool_project(diffs, w_ref, sc_ref)
    tail = ext_ref[tp:tp + N_META, :]
    ext_ref[0:N_META, :] = tail

    @pl.when(t == pl.num_programs(1) - 1)
    def _():
        hl_ref[0] = tail


def _pool_main(xm, xs, g, w, sc, *, tp):
    nt = SEQ // tp
    return pl.pallas_call(
        functools.partial(_pool_main_body, tp=tp),
        grid=(BATCH, nt),
        in_specs=[
            pl.BlockSpec((tp, D_MODEL), lambda b, t: (b * nt + t, 0)),
            pl.BlockSpec((N_META, D_MODEL), lambda b, t: (b, 0)),
            pl.BlockSpec((1, D_MODEL), lambda b, t: (0, 0)),
            pl.BlockSpec((N_POOL_GROUPS, POOL_GROUP, POOL_GROUP), lambda b, t: (0, 0, 0)),
            pl.BlockSpec((1, D_MODEL), lambda b, t: (0, 0)),
        ],
        out_specs=[
            pl.BlockSpec((tp, D_MODEL), lambda b, t: (b * nt + t, 0)),
            pl.BlockSpec((1, N_META, D_MODEL), lambda b, t: (b, 0, 0)),
        ],
        out_shape=[
            jax.ShapeDtypeStruct((N_MAIN, D_MODEL), F32),
            jax.ShapeDtypeStruct((BATCH, N_META, D_MODEL), F32),
        ],
        scratch_shapes=[pltpu.VMEM((tp + N_META, D_MODEL), F32)],
        compiler_params=pltpu.CompilerParams(
            dimension_semantics=("arbitrary", "arbitrary"), vmem_limit_bytes=VMEM_LIMIT),
        name="pool_main",
    )(xm, xs, g, w, sc)


def _pool_small_body(x_ref, st_ref, g_ref, w_ref, sc_ref, o_ref, ns_ref, ext_ref):
    nm = BATCH * N_META
    x = x_ref[...]
    h = _rms(x, g_ref[...])
    for b in range(BATCH):
        ext_ref[b, 0:N_META, :] = jnp.zeros((N_META, D_MODEL), F32)
        ext_ref[b, N_META:2 * N_META, :] = h[b * N_META:(b + 1) * N_META]
    hs = h[nm:]
    row = lax.broadcasted_iota(jnp.int32, (N_META, 1), 0)
    diffs = []
    for g, w in enumerate(POOL_WINDOWS):
        c0, c1 = g * POOL_GROUP, (g + 1) * POOL_GROUP
        cnt = jnp.minimum(row + 1, w).astype(F32)
        parts = []
        for b in range(BATCH):
            s = ext_ref[b, N_META:2 * N_META, c0:c1]
            for k in range(1, w):
                s = s + ext_ref[b, N_META - k:2 * N_META - k, c0:c1]
            parts.append(s / cnt)
        s = hs[:, c0:c1]
        for k in range(1, w):
            s = s + st_ref[POOL_STATE - k, :, c0:c1]
        parts.append(s / float(w))
        diffs.append(jnp.concatenate(parts, axis=0) - h[:, c0:c1])
    o_ref[...] = x + _pool_project(diffs, w_ref, sc_ref)
    ns_ref[0:POOL_STATE - 1] = st_ref[1:POOL_STATE]
    ns_ref[POOL_STATE - 1] = hs


def _pool_small(xs, st, layer, g, w, sc):
    single = pl.Buffered(1)

    def whole(shape):
        return pl.BlockSpec(shape, lambda i: (0,) * len(shape), pipeline_mode=single)

    state_shape = (POOL_STATE, DEC_BATCH, D_MODEL)
    return pl.pallas_call(
        _pool_small_body,
        grid=(1,),
        in_specs=[
            whole(xs.shape),
            pl.BlockSpec((None,) + state_shape, lambda i: (layer, 0, 0, 0), pipeline_mode=single),
            whole(g.shape), whole(w.shape), whole(sc.shape),
        ],
        out_specs=[whole((N_SMALL, D_MODEL)), whole(state_shape)],
        out_shape=[
            jax.ShapeDtypeStruct((N_SMALL, D_MODEL), F32),
            jax.ShapeDtypeStruct((POOL_STATE, DEC_BATCH, D_MODEL), F32),
        ],
        scratch_shapes=[pltpu.VMEM((BATCH, 2 * N_META, D_MODEL), F32)],
        compiler_params=pltpu.CompilerParams(vmem_limit_bytes=VMEM_LIMIT),
        name="pool_small",
    )(xs, st, g, w, sc)


QKV_CHUNK = 4 * HEAD_DIM


def _qkv_rows(x_ref, tab_refs, g_ref, w_ref, gain_ref, ones_ref, out_refs, h_ref):
    cos_ref, sa_ref, sb_ref = tab_refs
    q_ref, k_ref, v_ref, kd_ref, vd_ref = out_refs
    h_ref[...] = _rms(x_ref[...], g_ref[...]).astype(BF16)
    cos, sa, sb = cos_ref[...], sa_ref[...], sb_ref[...]
    half = ROT_DIM // 2

    def norm_rope(a, c0):
        sq_hi, sq_lo = _split_bf16(a * a)
        ss = _dot(sq_hi, ones_ref[...]) + _dot(sq_lo, ones_ref[...])
        y = a * lax.rsqrt(ss * (1.0 / HEAD_DIM) + EPS) * gain_ref[:, c0:c0 + QKV_CHUNK]
        tiles = []
        for u in range(QKV_CHUNK // LANES):
            yu = y[:, u * LANES:(u + 1) * LANES]
            tiles.append(yu * cos + pltpu.roll(yu, LANES - half, 1) * sa + pltpu.roll(yu, half, 1) * sb)
        return jnp.concatenate(tiles, axis=1)

    lower = lax.broadcasted_iota(jnp.int32, (1, LANES), 1) < HEAD_DIM

    def twice(t):
        tiles = []
        for u in range(KV_DIM // LANES):
            tu = t[:, u * LANES:(u + 1) * LANES]
            swapped = pltpu.roll(tu, HEAD_DIM, 1)
            tiles += [jnp.where(lower, tu, swapped), jnp.where(lower, swapped, tu)]
        return jnp.concatenate(tiles, axis=1).astype(BF16)

    wide = 2 * QKV_CHUNK
    nwide = QKV_DIM // wide
    project = lambda c: _dot(h_ref[...], w_ref[:, c * wide:(c + 1) * wide])
    a = project(0)
    for c in range(nwide):
        a_next = project(c + 1) if c + 1 < nwide else None
        c0 = c * wide
        if c0 < D_MODEL:
            for u in range(2):
                y = norm_rope(a[:, u * QKV_CHUNK:(u + 1) * QKV_CHUNK], c0 + u * QKV_CHUNK)
                q_ref[:, c0 + u * QKV_CHUNK:c0 + (u + 1) * QKV_CHUNK] = (
                    y * HEAD_DIM ** -0.5).astype(BF16)
        else:
            k = norm_rope(a[:, :KV_DIM], D_MODEL)
            v = a[:, KV_DIM:]
            k_ref[...] = k
            v_ref[...] = v
            kd_ref[...] = twice(k)
            vd_ref[...] = twice(v)
        a = a_next


def _qkv_body(*refs):
    x_ref, xs_ref, g_ref, w_ref, gain_ref = refs[:5]
    tabs, tabs_s, ones_ref = refs[5:8], refs[8:11], refs[11]
    outs, outs_s = refs[12:17], refs[17:22]
    wb_ref, h_ref, hs_ref = refs[22:]

    @pl.when(pl.program_id(0) == 0)
    def _():
        wb_ref[...] = w_ref[...].astype(BF16)
        _qkv_rows(xs_ref, tabs_s, g_ref, wb_ref, gain_ref, ones_ref, outs_s, hs_ref)

    half = x_ref.shape[0] // 2
    for r in range(2):
        rows = pl.ds(r * half, half)
        _qkv_rows(x_ref.at[rows], [t.at[rows] for t in tabs], g_ref, wb_ref, gain_ref, ones_ref,
                  [o.at[rows] for o in outs], h_ref.at[rows])


def _qkv(x, xs, g, w, layer, gain, tabs, tabs_s, ones, *, tm):
    m, ms = x.shape[0], xs.shape[0]
    nt = tabs[0].shape[0] // tm
    row = lambda i: (i, 0)
    tab = lambda i: (i % nt, 0)
    fixed = lambda i: (0, 0)
    widths = ((D_MODEL, BF16), (KV_DIM, F32), (KV_DIM, F32), (2 * KV_DIM, BF16), (2 * KV_DIM, BF16))
    return pl.pallas_call(
        _qkv_body,
        grid=(m // tm,),
        in_specs=[
            pl.BlockSpec((tm, D_MODEL), row),
            pl.BlockSpec((ms, D_MODEL), fixed),
            pl.BlockSpec((1, D_MODEL), fixed),
            pl.BlockSpec((None, D_MODEL, QKV_DIM), lambda i: (layer, 0, 0),
                         pipeline_mode=pl.Buffered(1)),
            pl.BlockSpec((1, QK_DIM), fixed),
        ]
        + [pl.BlockSpec((tm, LANES), tab)] * 3
        + [pl.BlockSpec((ms, LANES), fixed)] * 3
        + [pl.BlockSpec((QKV_CHUNK, QKV_CHUNK), fixed)],
        out_specs=[pl.BlockSpec((tm, n), row) for n, _ in widths]
        + [pl.BlockSpec((ms, n), fixed) for n, _ in widths],
        out_shape=[jax.ShapeDtypeStruct((m, n), dt) for n, dt in widths]
        + [jax.ShapeDtypeStruct((ms, n), dt) for n, dt in widths],
        scratch_shapes=[
            pltpu.VMEM((D_MODEL, QKV_DIM), BF16),
            pltpu.VMEM((tm, D_MODEL), BF16),
            pltpu.VMEM((ms, D_MODEL), BF16),
        ],
        compiler_params=pltpu.CompilerParams(
            dimension_semantics=("arbitrary",), vmem_limit_bytes=VMEM_LIMIT),
        name="qkv",
    )(x, xs, g, w, gain, *tabs, *tabs_s, ones)


def _softmax_av(s, sink, v):
    mx = jnp.maximum(jnp.max(s, axis=-1, keepdims=True), sink)
    p = jnp.exp(s - mx)
    den = jnp.sum(p, axis=-1, keepdims=True) + jnp.exp(sink - mx)
    return _dot(p.astype(BF16), v) / den


SLOTS = 4
CHUNK = SLOTS * HEAD_DIM


def _attn_main_body(sink_ref, q_ref, k_ref, v_ref, km_ref, vm_ref, x_ref, wo_ref, o_ref, oh_ref,
                    *, nblk):
    rows2 = 2 * WINDOW
    r = lax.broadcasted_iota(jnp.int32, (rows2, WINDOW), 0) % WINDOW
    c = lax.broadcasted_iota(jnp.int32, (rows2, WINDOW), 1)
    prev = c > r
    top = lax.broadcasted_iota(jnp.int32, (rows2, 1), 0) < WINDOW
    lane_slot = lax.broadcasted_iota(jnp.int32, (rows2, CHUNK), 1) // HEAD_DIM
    slot_mask = [(lane_slot == s).astype(F32).astype(BF16) for s in range(SLOTS)]
    zero = jnp.zeros((rows2, WINDOW), F32)
    step = pl.program_id(1)
    seen = jnp.logical_not(prev) | (c >= WINDOW - N_META)

    def block(i):
        row0 = i * WINDOW
        cur0 = pl.multiple_of(step * (nblk * WINDOW) + row0, WINDOW)
        past0 = pl.multiple_of(jnp.maximum(cur0 - WINDOW, 0), WINDOW)
        opens = cur0 == 0
        visible = seen | jnp.logical_not(opens)
        q_rows = pl.ds(row0, WINDOW)

        def keys_values(kv):
            lanes = slice(kv * LANES, (kv + 1) * LANES)
            k_past = jnp.where(opens, km_ref[0, :, lanes], k_ref[0, pl.ds(past0, WINDOW), lanes])
            v_past = jnp.where(opens, vm_ref[0, :, lanes], v_ref[0, pl.ds(past0, WINDOW), lanes])
            kd = jnp.concatenate([k_past, k_ref[0, pl.ds(cur0, WINDOW), lanes]], axis=0)
            vd = jnp.concatenate([v_past, v_ref[0, pl.ds(cur0, WINDOW), lanes]], axis=0)
            return jnp.concatenate([kd, kd], axis=1), jnp.concatenate([vd, vd], axis=1)

        def scores(kv, k4):
            g0 = kv * GROUP * HEAD_DIM
            qst = jnp.concatenate(
                [q_ref[q_rows, g0:g0 + CHUNK], q_ref[q_rows, g0 + CHUNK:g0 + 2 * CHUNK]], axis=0)
            return [_dot_nt(qst, k4 * slot_mask[s]) for s in range(SLOTS)]

        def weights(kv, s, sc):
            sf = jnp.where(prev, sc[:, :WINDOW], sc[:, WINDOW:])
            sf = jnp.where(visible, sf, NEG)
            sink = jnp.where(top, sink_ref[kv * GROUP + s], sink_ref[kv * GROUP + SLOTS + s])
            mx = jnp.maximum(jnp.max(sf, axis=-1, keepdims=True), sink)
            p = jnp.exp(sf - mx)
            den = jnp.sum(p, axis=-1, keepdims=True) + jnp.exp(sink - mx)
            p = p * (1.0 / den)
            p2 = jnp.concatenate([jnp.where(prev, p, zero), jnp.where(prev, zero, p)], axis=1)
            return p2.astype(BF16)

        def outputs(kv, ps, v4):
            acc = jnp.zeros((rows2, CHUNK), F32)
            for s in range(SLOTS):
                acc = acc + _dot(ps[s], v4 * slot_mask[s])
            g0 = kv * GROUP * HEAD_DIM
            oh_ref[q_rows, g0:g0 + CHUNK] = acc[:WINDOW].astype(BF16)
            oh_ref[q_rows, g0 + CHUNK:g0 + 2 * CHUNK] = acc[WINDOW:].astype(BF16)

        kvs = [keys_values(kv) for kv in range(N_KV_HEADS)]
        sc = scores(0, kvs[0][0])
        ps = None
        for kv in range(N_KV_HEADS):
            sc_next = scores(kv + 1, kvs[kv + 1][0]) if kv + 1 < N_KV_HEADS else None
            if ps is not None:
                outputs(kv - 1, ps, kvs[kv - 1][1])
            ps = [weights(kv, s, sc[s]) for s in range(SLOTS)]
            sc = sc_next
        outputs(N_KV_HEADS - 1, ps, kvs[N_KV_HEADS - 1][1])

    for i in range(nblk):
        block(i)
    o_ref[...] = x_ref[...] + _dot(oh_ref[...], wo_ref[...])


def _attn_main(sinks, q, kd, vd, km, vm, x, wo, layer, *, tq):
    nt = SEQ // tq
    row = lambda b, t: (b * nt + t, 0)
    batch = lambda b, t: (b, 0, 0)
    single = pl.Buffered(1)
    return pl.pallas_call(
        functools.partial(_attn_main_body, nblk=tq // WINDOW),
        grid=(BATCH, nt),
        in_specs=[
            pl.BlockSpec(memory_space=pltpu.SMEM),
            pl.BlockSpec((tq, D_MODEL), row),
            pl.BlockSpec((1, SEQ, 2 * KV_DIM), batch),
            pl.BlockSpec((1, SEQ, 2 * KV_DIM), batch),
            pl.BlockSpec((1, WINDOW, 2 * KV_DIM), batch),
            pl.BlockSpec((1, WINDOW, 2 * KV_DIM), batch),
            pl.BlockSpec((tq, D_MODEL), row),
            pl.BlockSpec((None, D_MODEL, D_MODEL), lambda b, t: (layer, 0, 0), pipeline_mode=single),
        ],
        out_specs=pl.BlockSpec((tq, D_MODEL), row),
        out_shape=jax.ShapeDtypeStruct((N_MAIN, D_MODEL), F32),
        scratch_shapes=[pltpu.VMEM((tq, D_MODEL), BF16)],
        compiler_params=pltpu.CompilerParams(
            dimension_semantics=("arbitrary", "arbitrary"), vmem_limit_bytes=VMEM_LIMIT),
        name="attn_main",
    )(sinks, q, kd, vd, km, vm, x, wo)


def _attn_small_body(sink_ref, q_ref, k_ref, v_ref, ck_hbm, cv_hbm, kn_ref, vn_ref, x_ref, wo_hbm,
                     o_ref, nk_ref, nv_ref, oh_ref, ck_ref, cv_ref, wo_ref, sems, *, layer):
    fetches = [pltpu.make_async_copy(src.at[layer], dst, sems.at[n])
               for n, (src, dst) in enumerate(((ck_hbm, ck_ref), (cv_hbm, cv_ref), (wo_hbm, wo_ref)))]
    for fetch in fetches:
        fetch.start()

    nm = BATCH * N_META
    assert nm == DEC_BATCH
    nq = GROUP * DEC_BATCH
    lower = lax.broadcasted_iota(jnp.int32, (1, LANES), 1) < HEAD_DIM

    def stacked(row0, kv):
        tiles = []
        for h in range(kv * GROUP, (kv + 1) * GROUP, 2):
            pair = q_ref[row0:row0 + DEC_BATCH, h * HEAD_DIM:(h + 2) * HEAD_DIM].astype(F32)
            swapped = pltpu.roll(pair, HEAD_DIM, 1)
            tiles += [jnp.where(lower, pair, swapped), jnp.where(lower, swapped, pair)]
        return jnp.concatenate(tiles, axis=0)

    def sink_column(kv):
        return jnp.concatenate([jnp.full((DEC_BATCH, 1), sink_ref[h], F32)
                                for h in range(kv * GROUP, (kv + 1) * GROUP)], axis=0)

    def unstack(o, row0, kv):
        for gi, h in enumerate(range(kv * GROUP, (kv + 1) * GROUP)):
            oh_ref[row0:row0 + DEC_BATCH, h * HEAD_DIM:(h + 1) * HEAD_DIM] = (
                o[gi * DEC_BATCH:(gi + 1) * DEC_BATCH].astype(BF16))

    r = lax.broadcasted_iota(jnp.int32, (nq, nm), 0) % nm
    c = lax.broadcasted_iota(jnp.int32, (nq, nm), 1)
    mmask = (c <= r) & ((r < N_META) == (c < N_META))
    km = k_ref[0:nm, :].astype(BF16)
    vm = v_ref[0:nm, :].astype(BF16)
    for kv in range(N_KV_HEADS):
        ks = slice(kv * HEAD_DIM, (kv + 1) * HEAD_DIM)
        qm = stacked(0, kv)[:, :HEAD_DIM].astype(BF16)
        s = jnp.where(mmask, _dot_nt(qm, km[:, ks]), NEG)
        unstack(_softmax_av(s, sink_column(kv), vm[:, ks]), 0, kv)

    wide = DEC_BATCH * HEAD_DIM
    own = (lax.broadcasted_iota(jnp.int32, (nq, wide), 0) % DEC_BATCH
           == lax.broadcasted_iota(jnp.int32, (nq, wide), 1) // HEAD_DIM).astype(F32)
    live = lax.broadcasted_iota(jnp.int32, (nq, WINDOW), 1) >= 1
    fetches[0].wait()
    fetches[1].wait()
    for kv in range(N_KV_HEADS):
        ks = slice(kv * HEAD_DIM, (kv + 1) * HEAD_DIM)
        kts = ck_ref[:, kv].reshape(wide, WINDOW).astype(BF16)
        vts = cv_ref[:, kv].reshape(wide, WINDOW).astype(BF16)
        qd = stacked(nm, kv)
        qs = qd[:, :HEAD_DIM]
        qblk = (jnp.tile(qd, (1, wide // LANES)) * own).astype(BF16)
        sink = sink_column(kv)
        k_new = jnp.tile(k_ref[nm:, ks], (GROUP, 1))
        v_new = jnp.tile(v_ref[nm:, ks], (GROUP, 1))
        s = jnp.where(live, _dot(qblk, kts), NEG)
        s_new = jnp.sum(qs * k_new, axis=-1, keepdims=True)
        mx = jnp.maximum(jnp.maximum(jnp.max(s, axis=-1, keepdims=True), s_new), sink)
        p = jnp.exp(s - mx)
        p_new = jnp.exp(s_new - mx)
        den = jnp.sum(p, axis=-1, keepdims=True) + p_new + jnp.exp(sink - mx)
        spread = _dot_nt(p.astype(BF16), vts) * own
        folded = spread[:, :LANES]
        for t in range(1, wide // LANES):
            folded = folded + spread[:, t * LANES:(t + 1) * LANES]
        pv = (folded + pltpu.roll(folded, HEAD_DIM, 1))[:, :HEAD_DIM]
        unstack((pv + p_new * v_new) / den, nm, kv)

    last = lax.broadcasted_iota(jnp.int32, (KV_DIM, WINDOW), 1) == WINDOW - 1
    for b in range(DEC_BATCH):
        for cache, new, out in ((ck_ref, kn_ref, nk_ref), (cv_ref, vn_ref, nv_ref)):
            rolled = pltpu.roll(cache[b].reshape(KV_DIM, WINDOW), WINDOW - 1, 1)
            out[b] = jnp.where(last, new[:, b:b + 1], rolled).reshape(N_KV_HEADS, HEAD_DIM, WINDOW)

    fetches[2].wait()
    o_ref[...] = x_ref[...] + _dot(oh_ref[...], wo_ref[...])


def _attn_small(sinks, q, k, v, ck, cv, kn, vn, x, wo, layer):
    single = pl.Buffered(1)
    cache_shape = (DEC_BATCH, N_KV_HEADS, HEAD_DIM, WINDOW)

    def whole(shape):
        return pl.BlockSpec(shape, lambda i: (0,) * len(shape), pipeline_mode=single)

    anywhere = pl.BlockSpec(memory_space=pl.ANY)
    return pl.pallas_call(
        functools.partial(_attn_small_body, layer=layer),
        grid=(1,),
        in_specs=[pl.BlockSpec(memory_space=pltpu.SMEM)]
        + [whole(a.shape) for a in (q, k, v)] + [anywhere, anywhere]
        + [whole(a.shape) for a in (kn, vn, x)] + [anywhere],
        out_specs=[whole((N_SMALL, D_MODEL)), whole(cache_shape), whole(cache_shape)],
        out_shape=[
            jax.ShapeDtypeStruct((N_SMALL, D_MODEL), F32),
            jax.ShapeDtypeStruct(cache_shape, F32),
            jax.ShapeDtypeStruct(cache_shape, F32),
        ],
        scratch_shapes=[
            pltpu.VMEM((N_SMALL, D_MODEL), BF16),
            pltpu.VMEM(cache_shape, F32),
            pltpu.VMEM(cache_shape, F32),
            pltpu.VMEM((D_MODEL, D_MODEL), BF16),
            pltpu.SemaphoreType.DMA((3,)),
        ],
        compiler_params=pltpu.CompilerParams(vmem_limit_bytes=VMEM_LIMIT),
        name="attn_small",
    )(sinks, q, k, v, ck, cv, kn, vn, x, wo)


def _rope_tables(pos):
    half = ROT_DIM // 2
    n = pos.shape[0]
    inv = jnp.float32(ROPE_THETA) ** (-jnp.arange(half, dtype=F32) * 2.0 / ROT_DIM)
    ang = pos.astype(F32)[:, None] * inv[None, :]
    cos, sin = jnp.cos(ang), jnp.sin(ang)
    rest = HEAD_DIM - ROT_DIM
    cos64 = jnp.concatenate([cos, cos, jnp.ones((n, rest), F32)], axis=1)
    sa64 = jnp.concatenate([-sin, jnp.zeros((n, half + rest), F32)], axis=1)
    sb64 = jnp.concatenate([jnp.zeros((n, half), F32), sin, jnp.zeros((n, rest), F32)], axis=1)
    rep = LANES // HEAD_DIM
    return tuple(jnp.tile(t, (1, rep)) for t in (cos64, sa64, sb64))


def kernel(x_prompt, x_sample, state_pool, cache_k, cache_v, meta_tokens, norm_mix, norm_ffn,
           pool_w, pool_scale, w_qkv, w_o, q_norm, k_norm, sinks, w_gate, w_up, w_down):
    assert x_prompt.shape == (BATCH, SEQ, D_MODEL) and x_sample.shape == (DEC_BATCH, 1, D_MODEL)
    nm = BATCH * N_META
    meta = meta_tokens.astype(F32)
    xm = x_prompt.reshape(N_MAIN, D_MODEL)
    xs = jnp.concatenate([meta, meta, x_sample.reshape(DEC_BATCH, D_MODEL)], axis=0)

    tabs_main = _rope_tables(N_META + jnp.arange(SEQ))
    tabs_small = _rope_tables(jnp.concatenate(
        [jnp.arange(N_META), jnp.arange(N_META), jnp.full((DEC_BATCH,), PAST_LEN)]))
    head_of_col = jnp.arange(QKV_CHUNK) // HEAD_DIM
    ones = (head_of_col[:, None] == head_of_col[None, :]).astype(BF16)

    wo = w_o.astype(BF16)
    st = jnp.swapaxes(state_pool, 1, 2)
    ck = jnp.transpose(cache_k, (0, 1, 3, 4, 2))
    cv = jnp.transpose(cache_v, (0, 1, 3, 4, 2))

    pool_p, pool_s, kp_l, vp_l, ks_l, vs_l = [], [], [], [], [], []
    for i in range(DEPTH):
        j = i // 2
        g_mix = norm_mix[i].reshape(1, D_MODEL)
        if i % 2 == 0:
            w = pool_w[j].astype(BF16)
            sc = pool_scale[j].reshape(1, D_MODEL)
            xm_new, h_last = _pool_main(xm, xs, g_mix, w, sc, tp=512)
            xs, ns = _pool_small(xs, st, j, g_mix, w, sc)
            xm = xm_new
            pool_p.append(h_last[:, N_META - POOL_STATE:])
            pool_s.append(jnp.swapaxes(ns, 0, 1))
        else:
            gain = jnp.concatenate(
                [jnp.tile(q_norm[j], N_HEADS), jnp.tile(k_norm[j], N_KV_HEADS)]).reshape(1, QK_DIM)
            (q_m, k_m, v_m, kd_m, vd_m, q_s, k_s, v_s, kd_s, vd_s) = _qkv(
                xm, xs, g_mix, w_qkv, j, gain, tabs_main, tabs_small, ones, tm=512)

            def meta_past(t):
                t = t[:nm].reshape(BATCH, N_META, 2 * KV_DIM)
                return jnp.pad(t, ((0, 0), (WINDOW - N_META, 0), (0, 0)))

            kn = k_s[nm:].T
            vn = v_s[nm:].T
            xm = _attn_main(sinks[j], q_m, kd_m.reshape(BATCH, SEQ, 2 * KV_DIM),
                            vd_m.reshape(BATCH, SEQ, 2 * KV_DIM), meta_past(kd_s), meta_past(vd_s),
                            xm, wo, j, tq=512)
            xs, nk, nv = _attn_small(sinks[j], q_s, k_s, v_s, ck, cv, kn, vn, xs, wo, j)
            last = lambda t: t.reshape(BATCH, SEQ, KV_DIM)[:, -WINDOW:].reshape(
                BATCH, WINDOW, N_KV_HEADS, HEAD_DIM)
            kp_l.append(last(k_m))
            vp_l.append(last(v_m))
            ks_l.append(jnp.transpose(nk, (0, 3, 1, 2)))
            vs_l.append(jnp.transpose(nv, (0, 3, 1, 2)))
        g_ffn = norm_ffn[i].reshape(1, D_MODEL)
        xm, xs = _ffn(xm, xs, g_ffn, w_gate, w_up, w_down, i, tm=1024, tf=512)

    return (xm.reshape(BATCH, SEQ, D_MODEL), xs[nm:].reshape(DEC_BATCH, 1, D_MODEL),
            jnp.stack(pool_p), jnp.stack(kp_l), jnp.stack(vp_l),
            jnp.stack(pool_s), jnp.stack(ks_l), jnp.stack(vs_l))
```

```python
import functools

import jax
import jax.numpy as jnp
from jax import lax
from jax.experimental import pallas as pl
from jax.experimental.pallas import tpu as pltpu

F32 = jnp.float32
BF16 = jnp.bfloat16

D_MODEL = 2048
BATCH = 2
SEQ = 4096
DEPTH = 4
DEC_BATCH = 32
PAST_LEN = 16384
N_META = 16
POOL_WINDOWS = (2, 4, 8, 16)
N_POOL_GROUPS = len(POOL_WINDOWS)
POOL_GROUP = D_MODEL // N_POOL_GROUPS
POOL_STATE = max(POOL_WINDOWS) - 1
HEAD_DIM = 64
N_HEADS = D_MODEL // HEAD_DIM
N_KV_HEADS = 4
GROUP = N_HEADS // N_KV_HEADS
WINDOW = 128
ROT_DIM = HEAD_DIM // 4
ROPE_THETA = 500000.0
D_FF = 5632
EPS = 1e-6
NEG = -1e30

N_MAIN = BATCH * SEQ
N_SMALL = BATCH * N_META + DEC_BATCH
KV_DIM = N_KV_HEADS * HEAD_DIM
QK_DIM = D_MODEL + KV_DIM
QKV_DIM = D_MODEL + 2 * KV_DIM
LANES = 128

VMEM_LIMIT = 60 * 1024 * 1024


def _rms(x, g):
    ms = jnp.mean(x * x, axis=-1, keepdims=True)
    return x * lax.rsqrt(ms + EPS) * g


def _dot(a, b):
    return jnp.dot(a, b, preferred_element_type=F32)


def _dot_nt(a, b):
    return lax.dot_general(a, b, (((1,), (1,)), ((), ())), preferred_element_type=F32)


def _split_bf16(x):
    hi = x.astype(BF16)
    lo = (x - hi.astype(F32)).astype(BF16)
    return hi, lo


def _ffn_body(x_ref, xs_ref, g_ref, wg_ref, wu_ref, wd_ref, o_hbm, os_ref, h_ref, acc_ref, sem):
    i, j = pl.program_id(0), pl.program_id(1)
    last_i, last_j = pl.num_programs(0) - 1, pl.num_programs(1) - 1
    tm, ms = x_ref.shape[0], xs_ref.shape[0]

    def writeback(tile):
        rows = pl.ds(pl.multiple_of(tile * tm, tm), tm)
        return pltpu.make_async_copy(acc_ref, o_hbm.at[rows, :], sem.at[0])

    def normalise(src_ref, row0):
        h_ref[row0:row0 + src_ref.shape[0], :] = _rms(src_ref[...], g_ref[...]).astype(BF16)

    def accumulate(rows, first):
        h = h_ref[0:rows, :]
        gate = _dot(h, wg_ref[...].astype(BF16))
        up = _dot(h, wu_ref[...].astype(BF16))
        act = (gate * jax.nn.sigmoid(gate) * up).astype(BF16)
        if first and rows == tm:
            writeback(i - 1).wait()
        y = _dot(act, wd_ref[...].astype(BF16))
        acc_ref[...] = (x_ref[...] if first else acc_ref[...]) + y[:tm]
        if rows > tm:
            os_ref[...] = (xs_ref[...] if first else os_ref[...]) + y[tm:]

    pl.when(j == 0)(lambda: normalise(x_ref, 0))
    pl.when((i == 0) & (j == 0))(lambda: normalise(xs_ref, tm))
    pl.when((i == 0) & (j == 0))(lambda: accumulate(tm + ms, True))
    pl.when((i == 0) & (j > 0))(lambda: accumulate(tm + ms, False))
    pl.when((i > 0) & (j == 0))(lambda: accumulate(tm, True))
    pl.when((i > 0) & (j > 0))(lambda: accumulate(tm, False))
    pl.when(j == last_j)(lambda: writeback(i).start())
    pl.when((i == last_i) & (j == last_j))(lambda: writeback(i).wait())


def _ffn(x, xs, g, wg, wu, wd, layer, *, tm, tf):
    m, ms = x.shape[0], xs.shape[0]
    return pl.pallas_call(
        _ffn_body,
        grid=(m // tm, D_FF // tf),
        in_specs=[
            pl.BlockSpec((tm, D_MODEL), lambda i, j: (i, 0)),
            pl.BlockSpec((ms, D_MODEL), lambda i, j: (0, 0)),
            pl.BlockSpec((1, D_MODEL), lambda i, j: (0, 0)),
            pl.BlockSpec((None, D_MODEL, tf), lambda i, j: (layer, 0, j)),
            pl.BlockSpec((None, D_MODEL, tf), lambda i, j: (layer, 0, j)),
            pl.BlockSpec((None, tf, D_MODEL), lambda i, j: (layer, j, 0)),
        ],
        out_specs=[
            pl.BlockSpec(memory_space=pl.ANY),
            pl.BlockSpec((ms, D_MODEL), lambda i, j: (0, 0)),
        ],
        out_shape=[
            jax.ShapeDtypeStruct((m, D_MODEL), F32),
            jax.ShapeDtypeStruct((ms, D_MODEL), F32),
        ],
        scratch_shapes=[
            pltpu.VMEM((tm + ms, D_MODEL), BF16),
            pltpu.VMEM((tm, D_MODEL), F32),
            pltpu.SemaphoreType.DMA((1,)),
        ],
        compiler_params=pltpu.CompilerParams(
            dimension_semantics=("arbitrary", "arbitrary"), vmem_limit_bytes=VMEM_LIMIT),
        name="ffn",
    )(x, xs, g, wg, wu, wd)


def _pool_project(diffs, w_ref, sc_ref):
    ys = [_dot(d.astype(BF16), w_ref[g]) for g, d in enumerate(diffs)]
    return jnp.concatenate(ys, axis=1) * sc_ref[...]


def _pool_main_body(x_ref, meta_ref, g_ref, w_ref, sc_ref, o_ref, hl_ref, ext_ref, *, tp):
    t = pl.program_id(1)

    @pl.when(t == 0)
    def _():
        ext_ref[0:N_META, :] = _rms(meta_ref[...], g_ref[...])

    x = x_ref[...]
    h = _rms(x, g_ref[...])
    ext_ref[N_META:N_META + tp, :] = h
    diffs = []
    for g, w in enumerate(POOL_WINDOWS):
        c0, c1 = g * POOL_GROUP, (g + 1) * POOL_GROUP
        s = ext_ref[:, c0:c1]
        shift = 1
        while shift < w:
            s = s + pltpu.roll(s, shift, 0)
            shift *= 2
        diffs.append(s[N_META:] * (1.0 / w) - h[:, c0:c1])
    o_ref[...] = x + _pool_project(diffs, w_ref, sc_ref)
    tail = ext_ref[tp:tp + N_META, :]
    ext_ref[0:N_META, :] = tail

    @pl.when(t == pl.num_programs(1) - 1)
    def _():
        hl_ref[0] = tail


def _pool_main(xm, xs, g, w, sc, *, tp):
    nt = SEQ // tp
    return pl.pallas_call(
        functools.partial(_pool_main_body, tp=tp),
        grid=(BATCH, nt),
        in_specs=[
            pl.BlockSpec((tp, D_MODEL), lambda b, t: (b * nt + t, 0)),
            pl.BlockSpec((N_META, D_MODEL), lambda b, t: (b, 0)),
            pl.BlockSpec((1, D_MODEL), lambda b, t: (0, 0)),
            pl.BlockSpec((N_POOL_GROUPS, POOL_GROUP, POOL_GROUP), lambda b, t: (0, 0, 0)),
            pl.BlockSpec((1, D_MODEL), lambda b, t: (0, 0)),
        ],
        out_specs=[
            pl.BlockSpec((tp, D_MODEL), lambda b, t: (b * nt + t, 0)),
            pl.BlockSpec((1, N_META, D_MODEL), lambda b, t: (b, 0, 0)),
        ],
        out_shape=[
            jax.ShapeDtypeStruct((N_MAIN, D_MODEL), F32),
            jax.ShapeDtypeStruct((BATCH, N_META, D_MODEL), F32),
        ],
        scratch_shapes=[pltpu.VMEM((tp + N_META, D_MODEL), F32)],
        compiler_params=pltpu.CompilerParams(
            dimension_semantics=("arbitrary", "arbitrary"), vmem_limit_bytes=VMEM_LIMIT),
        name="pool_main",
    )(xm, xs, g, w, sc)


def _pool_small_body(x_ref, st_ref, g_ref, w_ref, sc_ref, o_ref, ns_ref, ext_ref):
    nm = BATCH * N_META
    x = x_ref[...]
    h = _rms(x, g_ref[...])
    for b in range(BATCH):
        ext_ref[b, 0:N_META, :] = jnp.zeros((N_META, D_MODEL), F32)
        ext_ref[b, N_META:2 * N_META, :] = h[b * N_META:(b + 1) * N_META]
    hs = h[nm:]
    row = lax.broadcasted_iota(jnp.int32, (N_META, 1), 0)
    diffs = []
    for g, w in enumerate(POOL_WINDOWS):
        c0, c1 = g * POOL_GROUP, (g + 1) * POOL_GROUP
        cnt = jnp.minimum(row + 1, w).astype(F32)
        parts = []
        for b in range(BATCH):
            s = ext_ref[b, N_META:2 * N_META, c0:c1]
            for k in range(1, w):
                s = s + ext_ref[b, N_META - k:2 * N_META - k, c0:c1]
            parts.append(s / cnt)
        s = hs[:, c0:c1]
        for k in range(1, w):
            s = s + st_ref[POOL_STATE - k, :, c0:c1]
        parts.append(s / float(w))
        diffs.append(jnp.concatenate(parts, axis=0) - h[:, c0:c1])
    o_ref[...] = x + _pool_project(diffs, w_ref, sc_ref)
    ns_ref[0:POOL_STATE - 1] = st_ref[1:POOL_STATE]
    ns_ref[POOL_STATE - 1] = hs


def _pool_small(xs, st, layer, g, w, sc):
    single = pl.Buffered(1)

    def whole(shape):
        return pl.BlockSpec(shape, lambda i: (0,) * len(shape), pipeline_mode=single)

    state_shape = (POOL_STATE, DEC_BATCH, D_MODEL)
    return pl.pallas_call(
        _pool_small_body,
        grid=(1,),
        in_specs=[
            whole(xs.shape),
            pl.BlockSpec((None,) + state_shape, lambda i: (layer, 0, 0, 0), pipeline_mode=single),
            whole(g.shape), whole(w.shape), whole(sc.shape),
        ],
        out_specs=[whole((N_SMALL, D_MODEL)), whole(state_shape)],
        out_shape=[
            jax.ShapeDtypeStruct((N_SMALL, D_MODEL), F32),
            jax.ShapeDtypeStruct((POOL_STATE, DEC_BATCH, D_MODEL), F32),
        ],
        scratch_shapes=[pltpu.VMEM((BATCH, 2 * N_META, D_MODEL), F32)],
        compiler_params=pltpu.CompilerParams(vmem_limit_bytes=VMEM_LIMIT),
        name="pool_small",
    )(xs, st, g, w, sc)


QKV_CHUNK = 4 * HEAD_DIM


def _qkv_rows(x_ref, tab_refs, g_ref, w_ref, gain_ref, ones_ref, out_refs, h_ref):
    cos_ref, sa_ref, sb_ref = tab_refs
    q_ref, k_ref, v_ref, kd_ref, vd_ref = out_refs
    h_ref[...] = _rms(x_ref[...], g_ref[...]).astype(BF16)
    cos, sa, sb = cos_ref[...], sa_ref[...], sb_ref[...]
    half = ROT_DIM // 2

    def norm_rope(a, c0):
        sq_hi, sq_lo = _split_bf16(a * a)
        ss = _dot(sq_hi, ones_ref[...]) + _dot(sq_lo, ones_ref[...])
        y = a * lax.rsqrt(ss * (1.0 / HEAD_DIM) + EPS) * gain_ref[:, c0:c0 + QKV_CHUNK]
        tiles = []
        for u in range(QKV_CHUNK // LANES):
            yu = y[:, u * LANES:(u + 1) * LANES]
            tiles.append(yu * cos + pltpu.roll(yu, LANES - half, 1) * sa + pltpu.roll(yu, half, 1) * sb)
        return jnp.concatenate(tiles, axis=1)

    lower = lax.broadcasted_iota(jnp.int32, (1, LANES), 1) < HEAD_DIM

    def twice(t):
        tiles = []
        for u in range(KV_DIM // LANES):
            tu = t[:, u * LANES:(u + 1) * LANES]
            swapped = pltpu.roll(tu, HEAD_DIM, 1)
            tiles += [jnp.where(lower, tu, swapped), jnp.where(lower, swapped, tu)]
        return jnp.concatenate(tiles, axis=1).astype(BF16)

    wide = 2 * QKV_CHUNK
    nwide = QKV_DIM // wide
    project = lambda c: _dot(h_ref[...], w_ref[:, c * wide:(c + 1) * wide])
    a = project(0)
    for c in range(nwide):
        a_next = project(c + 1) if c + 1 < nwide else None
        c0 = c * wide
        if c0 < D_MODEL:
            for u in range(2):
                y = norm_rope(a[:, u * QKV_CHUNK:(u + 1) * QKV_CHUNK], c0 + u * QKV_CHUNK)
                q_ref[:, c0 + u * QKV_CHUNK:c0 + (u + 1) * QKV_CHUNK] = (
                    y * HEAD_DIM ** -0.5).astype(BF16)
        else:
            k = norm_rope(a[:, :KV_DIM], D_MODEL)
            v = a[:, KV_DIM:]
            k_ref[...] = k
            v_ref[...] = v
            kd_ref[...] = twice(k)
            vd_ref[...] = twice(v)
        a = a_next


def _qkv_body(*refs):
    x_ref, xs_ref, g_ref, w_ref, gain_ref = refs[:5]
    tabs, tabs_s, ones_ref = refs[5:8], refs[8:11], refs[11]
    outs, outs_s = refs[12:17], refs[17:22]
    wb_ref, h_ref, hs_ref = refs[22:]

    @pl.when(pl.program_id(0) == 0)
    def _():
        wb_ref[...] = w_ref[...].astype(BF16)
        _qkv_rows(xs_ref, tabs_s, g_ref, wb_ref, gain_ref, ones_ref, outs_s, hs_ref)

    half = x_ref.shape[0] // 2
    for r in range(2):
        rows = pl.ds(r * half, half)
        _qkv_rows(x_ref.at[rows], [t.at[rows] for t in tabs], g_ref, wb_ref, gain_ref, ones_ref,
                  [o.at[rows] for o in outs], h_ref.at[rows])


def _qkv(x, xs, g, w, layer, gain, tabs, tabs_s, ones, *, tm):
    m, ms = x.shape[0], xs.shape[0]
    nt = tabs[0].shape[0] // tm
    row = lambda i: (i, 0)
    tab = lambda i: (i % nt, 0)
    fixed = lambda i: (0, 0)
    widths = ((D_MODEL, BF16), (KV_DIM, F32), (KV_DIM, F32), (2 * KV_DIM, BF16), (2 * KV_DIM, BF16))
    return pl.pallas_call(
        _qkv_body,
        grid=(m // tm,),
        in_specs=[
            pl.BlockSpec((tm, D_MODEL), row),
            pl.BlockSpec((ms, D_MODEL), fixed),
            pl.BlockSpec((1, D_MODEL), fixed),
            pl.BlockSpec((None, D_MODEL, QKV_DIM), lambda i: (layer, 0, 0),
                         pipeline_mode=pl.Buffered(1)),
            pl.BlockSpec((1, QK_DIM), fixed),
        ]
        + [pl.BlockSpec((tm, LANES), tab)] * 3
        + [pl.BlockSpec((ms, LANES), fixed)] * 3
        + [pl.BlockSpec((QKV_CHUNK, QKV_CHUNK), fixed)],
        out_specs=[pl.BlockSpec((tm, n), row) for n, _ in widths]
        + [pl.BlockSpec((ms, n), fixed) for n, _ in widths],
        out_shape=[jax.ShapeDtypeStruct((m, n), dt) for n, dt in widths]
        + [jax.ShapeDtypeStruct((ms, n), dt) for n, dt in widths],
        scratch_shapes=[
            pltpu.VMEM((D_MODEL, QKV_DIM), BF16),
            pltpu.VMEM((tm, D_MODEL), BF16),
            pltpu.VMEM((ms, D_MODEL), BF16),
        ],
        compiler_params=pltpu.CompilerParams(
            dimension_semantics=("arbitrary",), vmem_limit_bytes=VMEM_LIMIT),
        name="qkv",
    )(x, xs, g, w, gain, *tabs, *tabs_s, ones)


def _softmax_av(s, sink, v):
    mx = jnp.maximum(jnp.max(s, axis=-1, keepdims=True), sink)
    p = jnp.exp(s - mx)
    den = jnp.sum(p, axis=-1, keepdims=True) + jnp.exp(sink - mx)
    return _dot(p.astype(BF16), v) / den


SLOTS = 4
CHUNK = SLOTS * HEAD_DIM


def _attn_main_body(sink_ref, q_ref, k_ref, v_ref, km_ref, vm_ref, x_ref, wo_ref, o_ref, oh_ref,
                    *, nblk):
    rows2 = 2 * WINDOW
    r = lax.broadcasted_iota(jnp.int32, (rows2, WINDOW), 0) % WINDOW
    c = lax.broadcasted_iota(jnp.int32, (rows2, WINDOW), 1)
    prev = c > r
    top = lax.broadcasted_iota(jnp.int32, (rows2, 1), 0) < WINDOW
    lane_slot = lax.broadcasted_iota(jnp.int32, (rows2, CHUNK), 1) // HEAD_DIM
    slot_mask = [(lane_slot == s).astype(F32).astype(BF16) for s in range(SLOTS)]
    zero = jnp.zeros((rows2, WINDOW), F32)
    step = pl.program_id(1)
    seen = jnp.logical_not(prev) | (c >= WINDOW - N_META)

    def block(i):
        row0 = i * WINDOW
        cur0 = pl.multiple_of(step * (nblk * WINDOW) + row0, WINDOW)
        past0 = pl.multiple_of(jnp.maximum(cur0 - WINDOW, 0), WINDOW)
        opens = cur0 == 0
        visible = seen | jnp.logical_not(opens)
        q_rows = pl.ds(row0, WINDOW)

        def keys_values(kv):
            lanes = slice(kv * LANES, (kv + 1) * LANES)
            k_past = jnp.where(opens, km_ref[0, :, lanes], k_ref[0, pl.ds(past0, WINDOW), lanes])
            v_past = jnp.where(opens, vm_ref[0, :, lanes], v_ref[0, pl.ds(past0, WINDOW), lanes])
            kd = jnp.concatenate([k_past, k_ref[0, pl.ds(cur0, WINDOW), lanes]], axis=0)
            vd = jnp.concatenate([v_past, v_ref[0, pl.ds(cur0, WINDOW), lanes]], axis=0)
            return jnp.concatenate([kd, kd], axis=1), jnp.concatenate([vd, vd], axis=1)

        def scores(kv, k4):
            g0 = kv * GROUP * HEAD_DIM
            qst = jnp.concatenate(
                [q_ref[q_rows, g0:g0 + CHUNK], q_ref[q_rows, g0 + CHUNK:g0 + 2 * CHUNK]], axis=0)
            return [_dot_nt(qst, k4 * slot_mask[s]) for s in range(SLOTS)]

        def weights(kv, s, sc):
            sf = jnp.where(prev, sc[:, :WINDOW], sc[:, WINDOW:])
            sf = jnp.where(visible, sf, NEG)
            sink = jnp.where(top, sink_ref[kv * GROUP + s], sink_ref[kv * GROUP + SLOTS + s])
            mx = jnp.maximum(jnp.max(sf, axis=-1, keepdims=True), sink)
            p = jnp.exp(sf - mx)
            den = jnp.sum(p, axis=-1, keepdims=True) + jnp.exp(sink - mx)
            p = p * (1.0 / den)
            p2 = jnp.concatenate([jnp.where(prev, p, zero), jnp.where(prev, zero, p)], axis=1)
            return p2.astype(BF16)

        def outputs(kv, ps, v4):
            acc = jnp.zeros((rows2, CHUNK), F32)
            for s in range(SLOTS):
                acc = acc + _dot(ps[s], v4 * slot_mask[s])
            g0 = kv * GROUP * HEAD_DIM
            oh_ref[q_rows, g0:g0 + CHUNK] = acc[:WINDOW].astype(BF16)
            oh_ref[q_rows, g0 + CHUNK:g0 + 2 * CHUNK] = acc[WINDOW:].astype(BF16)

        kvs = [keys_values(kv) for kv in range(N_KV_HEADS)]
        sc = scores(0, kvs[0][0])
        ps = None
        for kv in range(N_KV_HEADS):
            sc_next = scores(kv + 1, kvs[kv + 1][0]) if kv + 1 < N_KV_HEADS else None
            if ps is not None:
                outputs(kv - 1, ps, kvs[kv - 1][1])
            ps = [weights(kv, s, sc[s]) for s in range(SLOTS)]
            sc = sc_next
        outputs(N_KV_HEADS - 1, ps, kvs[N_KV_HEADS - 1][1])

    for i in range(nblk):
        block(i)
    o_ref[...] = x_ref[...] + _dot(oh_ref[...], wo_ref[...])


def _attn_main(sinks, q, kd, vd, km, vm, x, wo, layer, *, tq):
    nt = SEQ // tq
    row = lambda b, t: (b * nt + t, 0)
    batch = lambda b, t: (b, 0, 0)
    single = pl.Buffered(1)
    return pl.pallas_call(
        functools.partial(_attn_main_body, nblk=tq // WINDOW),
        grid=(BATCH, nt),
        in_specs=[
            pl.BlockSpec(memory_space=pltpu.SMEM),
            pl.BlockSpec((tq, D_MODEL), row),
            pl.BlockSpec((1, SEQ, 2 * KV_DIM), batch),
            pl.BlockSpec((1, SEQ, 2 * KV_DIM), batch),
            pl.BlockSpec((1, WINDOW, 2 * KV_DIM), batch),
            pl.BlockSpec((1, WINDOW, 2 * KV_DIM), batch),
            pl.BlockSpec((tq, D_MODEL), row),
            pl.BlockSpec((None, D_MODEL, D_MODEL), lambda b, t: (layer, 0, 0), pipeline_mode=single),
        ],
        out_specs=pl.BlockSpec((tq, D_MODEL), row),
        out_shape=jax.ShapeDtypeStruct((N_MAIN, D_MODEL), F32),
        scratch_shapes=[pltpu.VMEM((tq, D_MODEL), BF16)],
        compiler_params=pltpu.CompilerParams(
            dimension_semantics=("arbitrary", "arbitrary"), vmem_limit_bytes=VMEM_LIMIT),
        name="attn_main",
    )(sinks, q, kd, vd, km, vm, x, wo)


def _attn_small_body(sink_ref, q_ref, k_ref, v_ref, ck_hbm, cv_hbm, kn_ref, vn_ref, x_ref, wo_hbm,
                     o_ref, nk_ref, nv_ref, oh_ref, ck_ref, cv_ref, wo_ref, sems, *, layer):
    fetches = [pltpu.make_async_copy(src.at[layer], dst, sems.at[n])
               for n, (src, dst) in enumerate(((ck_hbm, ck_ref), (cv_hbm, cv_ref), (wo_hbm, wo_ref)))]
    for fetch in fetches:
        fetch.start()

    nm = BATCH * N_META
    assert nm == DEC_BATCH
    nq = GROUP * DEC_BATCH
    lower = lax.broadcasted_iota(jnp.int32, (1, LANES), 1) < HEAD_DIM

    def stacked(row0, kv):
        tiles = []
        for h in range(kv * GROUP, (kv + 1) * GROUP, 2):
            pair = q_ref[row0:row0 + DEC_BATCH, h * HEAD_DIM:(h + 2) * HEAD_DIM].astype(F32)
            swapped = pltpu.roll(pair, HEAD_DIM, 1)
            tiles += [jnp.where(lower, pair, swapped), jnp.where(lower, swapped, pair)]
        return jnp.concatenate(tiles, axis=0)

    def sink_column(kv):
        return jnp.concatenate([jnp.full((DEC_BATCH, 1), sink_ref[h], F32)
                                for h in range(kv * GROUP, (kv + 1) * GROUP)], axis=0)

    def unstack(o, row0, kv):
        for gi, h in enumerate(range(kv * GROUP, (kv + 1) * GROUP)):
            oh_ref[row0:row0 + DEC_BATCH, h * HEAD_DIM:(h + 1) * HEAD_DIM] = (
                o[gi * DEC_BATCH:(gi + 1) * DEC_BATCH].astype(BF16))

    r = lax.broadcasted_iota(jnp.int32, (nq, nm), 0) % nm
    c = lax.broadcasted_iota(jnp.int32, (nq, nm), 1)
    mmask = (c <= r) & ((r < N_META) == (c < N_META))
    km = k_ref[0:nm, :].astype(BF16)
    vm = v_ref[0:nm, :].astype(BF16)
    for kv in range(N_KV_HEADS):
        ks = slice(kv * HEAD_DIM, (kv + 1) * HEAD_DIM)
        qm = stacked(0, kv)[:, :HEAD_DIM].astype(BF16)
        s = jnp.where(mmask, _dot_nt(qm, km[:, ks]), NEG)
        unstack(_softmax_av(s, sink_column(kv), vm[:, ks]), 0, kv)

    wide = DEC_BATCH * HEAD_DIM
    own = (lax.broadcasted_iota(jnp.int32, (nq, wide), 0) % DEC_BATCH
           == lax.broadcasted_iota(jnp.int32, (nq, wide), 1) // HEAD_DIM).astype(F32)
    live = lax.broadcasted_iota(jnp.int32, (nq, WINDOW), 1) >= 1
    fetches[0].wait()
    fetches[1].wait()
    for kv in range(N_KV_HEADS):
        ks = slice(kv * HEAD_DIM, (kv + 1) * HEAD_DIM)
        kts = ck_ref[:, kv].reshape(wide, WINDOW).astype(BF16)
        vts = cv_ref[:, kv].reshape(wide, WINDOW).astype(BF16)
        qd = stacked(nm, kv)
        qs = qd[:, :HEAD_DIM]
        qblk = (jnp.tile(qd, (1, wide // LANES)) * own).astype(BF16)
        sink = sink_column(kv)
        k_new = jnp.tile(k_ref[nm:, ks], (GROUP, 1))
        v_new = jnp.tile(v_ref[nm:, ks], (GROUP, 1))
        s = jnp.where(live, _dot(qblk, kts), NEG)
        s_new = jnp.sum(qs * k_new, axis=-1, keepdims=True)
        mx = jnp.maximum(jnp.maximum(jnp.max(s, axis=-1, keepdims=True), s_new), sink)
        p = jnp.exp(s - mx)
        p_new = jnp.exp(s_new - mx)
        den = jnp.sum(p, axis=-1, keepdims=True) + p_new + jnp.exp(sink - mx)
        spread = _dot_nt(p.astype(BF16), vts) * own
        folded = spread[:, :LANES]
        for t in range(1, wide // LANES):
            folded = folded + spread[:, t * LANES:(t + 1) * LANES]
        pv = (folded + pltpu.roll(folded, HEAD_DIM, 1))[:, :HEAD_DIM]
        unstack((pv + p_new * v_new) / den, nm, kv)

    last = lax.broadcasted_iota(jnp.int32, (KV_DIM, WINDOW), 1) == WINDOW - 1
    for b in range(DEC_BATCH):
        for cache, new, out in ((ck_ref, kn_ref, nk_ref), (cv_ref, vn_ref, nv_ref)):
            rolled = pltpu.roll(cache[b].reshape(KV_DIM, WINDOW), WINDOW - 1, 1)
            out[b] = jnp.where(last, new[:, b:b + 1], rolled).reshape(N_KV_HEADS, HEAD_DIM, WINDOW)

    fetches[2].wait()
    o_ref[...] = x_ref[...] + _dot(oh_ref[...], wo_ref[...])


def _attn_small(sinks, q, k, v, ck, cv, kn, vn, x, wo, layer):
    single = pl.Buffered(1)
    cache_shape = (DEC_BATCH, N_KV_HEADS, HEAD_DIM, WINDOW)

    def whole(shape):
        return pl.BlockSpec(shape, lambda i: (0,) * len(shape), pipeline_mode=single)

    anywhere = pl.BlockSpec(memory_space=pl.ANY)
    return pl.pallas_call(
        functools.partial(_attn_small_body, layer=layer),
        grid=(1,),
        in_specs=[pl.BlockSpec(memory_space=pltpu.SMEM)]
        + [whole(a.shape) for a in (q, k, v)] + [anywhere, anywhere]
        + [whole(a.shape) for a in (kn, vn, x)] + [anywhere],
        out_specs=[whole((N_SMALL, D_MODEL)), whole(cache_shape), whole(cache_shape)],
        out_shape=[
            jax.ShapeDtypeStruct((N_SMALL, D_MODEL), F32),
            jax.ShapeDtypeStruct(cache_shape, F32),
            jax.ShapeDtypeStruct(cache_shape, F32),
        ],
        scratch_shapes=[
            pltpu.VMEM((N_SMALL, D_MODEL), BF16),
            pltpu.VMEM(cache_shape, F32),
            pltpu.VMEM(cache_shape, F32),
            pltpu.VMEM((D_MODEL, D_MODEL), BF16),
            pltpu.SemaphoreType.DMA((3,)),
        ],
        compiler_params=pltpu.CompilerParams(vmem_limit_bytes=VMEM_LIMIT),
        name="attn_small",
    )(sinks, q, k, v, ck, cv, kn, vn, x, wo)


def _rope_tables(pos):
    half = ROT_DIM // 2
    n = pos.shape[0]
    inv = jnp.float32(ROPE_THETA) ** (-jnp.arange(half, dtype=F32) * 2.0 / ROT_DIM)
    ang = pos.astype(F32)[:, None] * inv[None, :]
    cos, sin = jnp.cos(ang), jnp.sin(ang)
    rest = HEAD_DIM - ROT_DIM
    cos64 = jnp.concatenate([cos, cos, jnp.ones((n, rest), F32)], axis=1)
    sa64 = jnp.concatenate([-sin, jnp.zeros((n, half + rest), F32)], axis=1)
    sb64 = jnp.concatenate([jnp.zeros((n, half), F32), sin, jnp.zeros((n, rest), F32)], axis=1)
    rep = LANES // HEAD_DIM
    return tuple(jnp.tile(t, (1, rep)) for t in (cos64, sa64, sb64))


def kernel(x_prompt, x_sample, state_pool, cache_k, cache_v, meta_tokens, norm_mix, norm_ffn,
           pool_w, pool_scale, w_qkv, w_o, q_norm, k_norm, sinks, w_gate, w_up, w_down):
    assert x_prompt.shape == (BATCH, SEQ, D_MODEL) and x_sample.shape == (DEC_BATCH, 1, D_MODEL)
    nm = BATCH * N_META
    meta = meta_tokens.astype(F32)
    xm = x_prompt.reshape(N_MAIN, D_MODEL)
    xs = jnp.concatenate([meta, meta, x_sample.reshape(DEC_BATCH, D_MODEL)], axis=0)

    tabs_main = _rope_tables(N_META + jnp.arange(SEQ))
    tabs_small = _rope_tables(jnp.concatenate(
        [jnp.arange(N_META), jnp.arange(N_META), jnp.full((DEC_BATCH,), PAST_LEN)]))
    head_of_col = jnp.arange(QKV_CHUNK) // HEAD_DIM
    ones = (head_of_col[:, None] == head_of_col[None, :]).astype(BF16)

    wo = w_o.astype(BF16)
    st = jnp.swapaxes(state_pool, 1, 2)
    ck = jnp.transpose(cache_k, (0, 1, 3, 4, 2))
    cv = jnp.transpose(cache_v, (0, 1, 3, 4, 2))

    pool_p, pool_s, kp_l, vp_l, ks_l, vs_l = [], [], [], [], [], []
    for i in range(DEPTH):
        j = i // 2
        g_mix = norm_mix[i].reshape(1, D_MODEL)
        if i % 2 == 0:
            w = pool_w[j].astype(BF16)
            sc = pool_scale[j].reshape(1, D_MODEL)
            xm_new, h_last = _pool_main(xm, xs, g_mix, w, sc, tp=1024)
            xs, ns = _pool_small(xs, st, j, g_mix, w, sc)
            xm = xm_new
            pool_p.append(h_last[:, N_META - POOL_STATE:])
            pool_s.append(jnp.swapaxes(ns, 0, 1))
        else:
            gain = jnp.concatenate(
                [jnp.tile(q_norm[j], N_HEADS), jnp.tile(k_norm[j], N_KV_HEADS)]).reshape(1, QK_DIM)
            (q_m, k_m, v_m, kd_m, vd_m, q_s, k_s, v_s, kd_s, vd_s) = _qkv(
                xm, xs, g_mix, w_qkv, j, gain, tabs_main, tabs_small, ones, tm=512)

            def meta_past(t):
                t = t[:nm].reshape(BATCH, N_META, 2 * KV_DIM)
                return jnp.pad(t, ((0, 0), (WINDOW - N_META, 0), (0, 0)))

            kn = k_s[nm:].T
            vn = v_s[nm:].T
            xm = _attn_main(sinks[j], q_m, kd_m.reshape(BATCH, SEQ, 2 * KV_DIM),
                            vd_m.reshape(BATCH, SEQ, 2 * KV_DIM), meta_past(kd_s), meta_past(vd_s),
                            xm, wo, j, tq=512)
            xs, nk, nv = _attn_small(sinks[j], q_s, k_s, v_s, ck, cv, kn, vn, xs, wo, j)
            last = lambda t: t.reshape(BATCH, SEQ, KV_DIM)[:, -WINDOW:].reshape(
                BATCH, WINDOW, N_KV_HEADS, HEAD_DIM)
            kp_l.append(last(k_m))
            vp_l.append(last(v_m))
            ks_l.append(jnp.transpose(nk, (0, 3, 1, 2)))
            vs_l.append(jnp.transpose(nv, (0, 3, 1, 2)))
        g_ffn = norm_ffn[i].reshape(1, D_MODEL)
        xm, xs = _ffn(xm, xs, g_ffn, w_gate, w_up, w_down, i, tm=1024, tf=512)

    return (xm.reshape(BATCH, SEQ, D_MODEL), xs[nm:].reshape(DEC_BATCH, 1, D_MODEL),
            jnp.stack(pool_p), jnp.stack(kp_l), jnp.stack(vp_l),
            jnp.stack(pool_s), jnp.stack(ks_l), jnp.stack(vs_l))
```

```python
import functools

import jax
import jax.numpy as jnp
from jax import lax
from jax.experimental import pallas as pl
from jax.experimental.pallas import tpu as pltpu

F32 = jnp.float32
BF16 = jnp.bfloat16

D_MODEL = 2048
BATCH = 2
SEQ = 4096
DEPTH = 4
DEC_BATCH = 32
PAST_LEN = 16384
N_META = 16
POOL_WINDOWS = (2, 4, 8, 16)
N_POOL_GROUPS = len(POOL_WINDOWS)
POOL_GROUP = D_MODEL // N_POOL_GROUPS
POOL_STATE = max(POOL_WINDOWS) - 1
HEAD_DIM = 64
N_HEADS = D_MODEL // HEAD_DIM
N_KV_HEADS = 4
GROUP = N_HEADS // N_KV_HEADS
WINDOW = 128
ROT_DIM = HEAD_DIM // 4
ROPE_THETA = 500000.0
D_FF = 5632
EPS = 1e-6
NEG = -1e30

N_MAIN = BATCH * SEQ
N_SMALL = BATCH * N_META + DEC_BATCH
KV_DIM = N_KV_HEADS * HEAD_DIM
QK_DIM = D_MODEL + KV_DIM
QKV_DIM = D_MODEL + 2 * KV_DIM
LANES = 128

VMEM_LIMIT = 60 * 1024 * 1024


def _rms(x, g):
    ms = jnp.mean(x * x, axis=-1, keepdims=True)
    return x * lax.rsqrt(ms + EPS) * g


def _dot(a, b):
    return jnp.dot(a, b, preferred_element_type=F32)


def _dot_nt(a, b):
    return lax.dot_general(a, b, (((1,), (1,)), ((), ())), preferred_element_type=F32)


def _split_bf16(x):
    hi = x.astype(BF16)
    lo = (x - hi.astype(F32)).astype(BF16)
    return hi, lo


def _ffn_body(x_ref, xs_ref, g_ref, wg_ref, wu_ref, wd_ref, o_hbm, os_ref, h_ref, acc_ref, sem):
    i, j = pl.program_id(0), pl.program_id(1)
    last_i, last_j = pl.num_programs(0) - 1, pl.num_programs(1) - 1
    tm, ms = x_ref.shape[0], xs_ref.shape[0]

    def writeback(tile):
        rows = pl.ds(pl.multiple_of(tile * tm, tm), tm)
        return pltpu.make_async_copy(acc_ref, o_hbm.at[rows, :], sem.at[0])

    def normalise(src_ref, row0):
        h_ref[row0:row0 + src_ref.shape[0], :] = _rms(src_ref[...], g_ref[...]).astype(BF16)

    def accumulate(rows, first):
        h = h_ref[0:rows, :]
        gate = _dot(h, wg_ref[...].astype(BF16))
        up = _dot(h, wu_ref[...].astype(BF16))
        act = (gate * jax.nn.sigmoid(gate) * up).astype(BF16)
        if first and rows == tm:
            writeback(i - 1).wait()
        y = _dot(act, wd_ref[...].astype(BF16))
        acc_ref[...] = (x_ref[...] if first else acc_ref[...]) + y[:tm]
        if rows > tm:
            os_ref[...] = (xs_ref[...] if first else os_ref[...]) + y[tm:]

    pl.when(j == 0)(lambda: normalise(x_ref, 0))
    pl.when((i == 0) & (j == 0))(lambda: normalise(xs_ref, tm))
    pl.when((i == 0) & (j == 0))(lambda: accumulate(tm + ms, True))
    pl.when((i == 0) & (j > 0))(lambda: accumulate(tm + ms, False))
    pl.when((i > 0) & (j == 0))(lambda: accumulate(tm, True))
    pl.when((i > 0) & (j > 0))(lambda: accumulate(tm, False))
    pl.when(j == last_j)(lambda: writeback(i).start())
    pl.when((i == last_i) & (j == last_j))(lambda: writeback(i).wait())


def _ffn(x, xs, g, wg, wu, wd, layer, *, tm, tf):
    m, ms = x.shape[0], xs.shape[0]
    return pl.pallas_call(
        _ffn_body,
        grid=(m // tm, D_FF // tf),
        in_specs=[
            pl.BlockSpec((tm, D_MODEL), lambda i, j: (i, 0)),
            pl.BlockSpec((ms, D_MODEL), lambda i, j: (0, 0)),
            pl.BlockSpec((1, D_MODEL), lambda i, j: (0, 0)),
            pl.BlockSpec((None, D_MODEL, tf), lambda i, j: (layer, 0, j)),
            pl.BlockSpec((None, D_MODEL, tf), lambda i, j: (layer, 0, j)),
            pl.BlockSpec((None, tf, D_MODEL), lambda i, j: (layer, j, 0)),
        ],
        out_specs=[
            pl.BlockSpec(memory_space=pl.ANY),
            pl.BlockSpec((ms, D_MODEL), lambda i, j: (0, 0)),
        ],
        out_shape=[
            jax.ShapeDtypeStruct((m, D_MODEL), F32),
            jax.ShapeDtypeStruct((ms, D_MODEL), F32),
        ],
        scratch_shapes=[
            pltpu.VMEM((tm + ms, D_MODEL), BF16),
            pltpu.VMEM((tm, D_MODEL), F32),
            pltpu.SemaphoreType.DMA((1,)),
        ],
        compiler_params=pltpu.CompilerParams(
            dimension_semantics=("arbitrary", "arbitrary"), vmem_limit_bytes=VMEM_LIMIT),
        name="ffn",
    )(x, xs, g, wg, wu, wd)


def _pool_project(diffs, w_ref, sc_ref):
    ys = [_dot(d.astype(BF16), w_ref[g]) for g, d in enumerate(diffs)]
    return jnp.concatenate(ys, axis=1) * sc_ref[...]


def _pool_main_body(x_ref, meta_ref, g_ref, w_ref, sc_ref, o_ref, hl_ref, ext_ref, *, tp):
    t = pl.program_id(1)

    @pl.when(t == 0)
    def _():
        ext_ref[0:N_META, :] = _rms(meta_ref[...], g_ref[...])

    x = x_ref[...]
    h = _rms(x, g_ref[...])
    ext_ref[N_META:N_META + tp, :] = h
    diffs = []
    for g, w in enumerate(POOL_WINDOWS):
        c0, c1 = g * POOL_GROUP, (g + 1) * POOL_GROUP
        s = ext_ref[:, c0:c1]
        shift = 1
        while shift < w:
            s = s + pltpu.roll(s, shift, 0)
            shift *= 2
        diffs.append(s[N_META:] * (1.0 / w) - h[:, c0:c1])
    o_ref[...] = x + _pool_project(diffs, w_ref, sc_ref)
    tail = ext_ref[tp:tp + N_META, :]
    ext_ref[0:N_META, :] = tail

    @pl.when(t == pl.num_programs(1) - 1)
    def _():
        hl_ref[0] = tail


def _pool_main(xm, xs, g, w, sc, *, tp):
    nt = SEQ // tp
    return pl.pallas_call(
        functools.partial(_pool_main_body, tp=tp),
        grid=(BATCH, nt),
        in_specs=[
            pl.BlockSpec((tp, D_MODEL), lambda b, t: (b * nt + t, 0)),
            pl.BlockSpec((N_META, D_MODEL), lambda b, t: (b, 0)),
            pl.BlockSpec((1, D_MODEL), lambda b, t: (0, 0)),
            pl.BlockSpec((N_POOL_GROUPS, POOL_GROUP, POOL_GROUP), lambda b, t: (0, 0, 0)),
            pl.BlockSpec((1, D_MODEL), lambda b, t: (0, 0)),
        ],
        out_specs=[
            pl.BlockSpec((tp, D_MODEL), lambda b, t: (b * nt + t, 0)),
            pl.BlockSpec((1, N_META, D_MODEL), lambda b, t: (b, 0, 0)),
        ],
        out_shape=[
            jax.ShapeDtypeStruct((N_MAIN, D_MODEL), F32),
            jax.ShapeDtypeStruct((BATCH, N_META, D_MODEL), F32),
        ],
        scratch_shapes=[pltpu.VMEM((tp + N_META, D_MODEL), F32)],
        compiler_params=pltpu.CompilerParams(
            dimension_semantics=("arbitrary", "arbitrary"), vmem_limit_bytes=VMEM_LIMIT),
        name="pool_main",
    )(xm, xs, g, w, sc)


def _pool_small_body(x_ref, st_ref, g_ref, w_ref, sc_ref, o_ref, ns_ref, ext_ref):
    nm = BATCH * N_META
    x = x_ref[...]
    h = _rms(x, g_ref[...])
    for b in range(BATCH):
        ext_ref[b, 0:N_META, :] = jnp.zeros((N_META, D_MODEL), F32)
        ext_ref[b, N_META:2 * N_META, :] = h[b * N_META:(b + 1) * N_META]
    hs = h[nm:]
    row = lax.broadcasted_iota(jnp.int32, (N_META, 1), 0)
    diffs = []
    for g, w in enumerate(POOL_WINDOWS):
        c0, c1 = g * POOL_GROUP, (g + 1) * POOL_GROUP
        cnt = jnp.minimum(row + 1, w).astype(F32)
        parts = []
        for b in range(BATCH):
            s = ext_ref[b, N_META:2 * N_META, c0:c1]
            for k in range(1, w):
                s = s + ext_ref[b, N_META - k:2 * N_META - k, c0:c1]
            parts.append(s / cnt)
        s = hs[:, c0:c1]
        for k in range(1, w):
            s = s + st_ref[POOL_STATE - k, :, c0:c1]
        parts.append(s / float(w))
        diffs.append(jnp.concatenate(parts, axis=0) - h[:, c0:c1])
    o_ref[...] = x + _pool_project(diffs, w_ref, sc_ref)
    ns_ref[0:POOL_STATE - 1] = st_ref[1:POOL_STATE]
    ns_ref[POOL_STATE - 1] = hs


def _pool_small(xs, st, layer, g, w, sc):
    single = pl.Buffered(1)

    def whole(shape):
        return pl.BlockSpec(shape, lambda i: (0,) * len(shape), pipeline_mode=single)

    state_shape = (POOL_STATE, DEC_BATCH, D_MODEL)
    return pl.pallas_call(
        _pool_small_body,
        grid=(1,),
        in_specs=[
            whole(xs.shape),
            pl.BlockSpec((None,) + state_shape, lambda i: (layer, 0, 0, 0), pipeline_mode=single),
            whole(g.shape), whole(w.shape), whole(sc.shape),
        ],
        out_specs=[whole((N_SMALL, D_MODEL)), whole(state_shape)],
        out_shape=[
            jax.ShapeDtypeStruct((N_SMALL, D_MODEL), F32),
            jax.ShapeDtypeStruct((POOL_STATE, DEC_BATCH, D_MODEL), F32),
        ],
        scratch_shapes=[pltpu.VMEM((BATCH, 2 * N_META, D_MODEL), F32)],
        compiler_params=pltpu.CompilerParams(vmem_limit_bytes=VMEM_LIMIT),
        name="pool_small",
    )(xs, st, g, w, sc)


QKV_CHUNK = 4 * HEAD_DIM


def _qkv_rows(x_ref, tab_refs, g_ref, w_ref, gain_ref, ones_ref, out_refs, h_ref):
    cos_ref, sa_ref, sb_ref = tab_refs
    q_ref, k_ref, v_ref, kd_ref, vd_ref = out_refs
    h_ref[...] = _rms(x_ref[...], g_ref[...]).astype(BF16)
    cos, sa, sb = cos_ref[...], sa_ref[...], sb_ref[...]
    half = ROT_DIM // 2

    def norm_rope(a, c0):
        sq_hi, sq_lo = _split_bf16(a * a)
        ss = _dot(sq_hi, ones_ref[...]) + _dot(sq_lo, ones_ref[...])
        y = a * lax.rsqrt(ss * (1.0 / HEAD_DIM) + EPS) * gain_ref[:, c0:c0 + QKV_CHUNK]
        tiles = []
        for u in range(QKV_CHUNK // LANES):
            yu = y[:, u * LANES:(u + 1) * LANES]
            tiles.append(yu * cos + pltpu.roll(yu, LANES - half, 1) * sa + pltpu.roll(yu, half, 1) * sb)
        return jnp.concatenate(tiles, axis=1)

    lower = lax.broadcasted_iota(jnp.int32, (1, LANES), 1) < HEAD_DIM

    def twice(t):
        tiles = []
        for u in range(KV_DIM // LANES):
            tu = t[:, u * LANES:(u + 1) * LANES]
            swapped = pltpu.roll(tu, HEAD_DIM, 1)
            tiles += [jnp.where(lower, tu, swapped), jnp.where(lower, swapped, tu)]
        return jnp.concatenate(tiles, axis=1).astype(BF16)

    wide = 2 * QKV_CHUNK
    nwide = QKV_DIM // wide
    project = lambda c: _dot(h_ref[...], w_ref[:, c * wide:(c + 1) * wide])
    a = project(0)
    for c in range(nwide):
        a_next = project(c + 1) if c + 1 < nwide else None
        c0 = c * wide
        if c0 < D_MODEL:
            for u in range(2):
                y = norm_rope(a[:, u * QKV_CHUNK:(u + 1) * QKV_CHUNK], c0 + u * QKV_CHUNK)
                q_ref[:, c0 + u * QKV_CHUNK:c0 + (u + 1) * QKV_CHUNK] = (
                    y * HEAD_DIM ** -0.5).astype(BF16)
        else:
            k = norm_rope(a[:, :KV_DIM], D_MODEL)
            v = a[:, KV_DIM:]
            k_ref[...] = k
            v_ref[...] = v
            kd_ref[...] = twice(k)
            vd_ref[...] = twice(v)
        a = a_next


def _qkv_body(*refs):
    x_ref, xs_ref, g_ref, w_ref, gain_ref = refs[:5]
    tabs, tabs_s, ones_ref = refs[5:8], refs[8:11], refs[11]
    outs, outs_s = refs[12:17], refs[17:22]
    wb_ref, h_ref, hs_ref = refs[22:]

    @pl.when(pl.program_id(0) == 0)
    def _():
        wb_ref[...] = w_ref[...].astype(BF16)
        _qkv_rows(xs_ref, tabs_s, g_ref, wb_ref, gain_ref, ones_ref, outs_s, hs_ref)

    half = x_ref.shape[0] // 2
    for r in range(2):
        rows = pl.ds(r * half, half)
        _qkv_rows(x_ref.at[rows], [t.at[rows] for t in tabs], g_ref, wb_ref, gain_ref, ones_ref,
                  [o.at[rows] for o in outs], h_ref.at[rows])


def _qkv(x, xs, g, w, layer, gain, tabs, tabs_s, ones, *, tm):
    m, ms = x.shape[0], xs.shape[0]
    nt = tabs[0].shape[0] // tm
    row = lambda i: (i, 0)
    tab = lambda i: (i % nt, 0)
    fixed = lambda i: (0, 0)
    widths = ((D_MODEL, BF16), (KV_DIM, F32), (KV_DIM, F32), (2 * KV_DIM, BF16), (2 * KV_DIM, BF16))
    return pl.pallas_call(
        _qkv_body,
        grid=(m // tm,),
        in_specs=[
            pl.BlockSpec((tm, D_MODEL), row),
            pl.BlockSpec((ms, D_MODEL), fixed),
            pl.BlockSpec((1, D_MODEL), fixed),
            pl.BlockSpec((None, D_MODEL, QKV_DIM), lambda i: (layer, 0, 0),
                         pipeline_mode=pl.Buffered(1)),
            pl.BlockSpec((1, QK_DIM), fixed),
        ]
        + [pl.BlockSpec((tm, LANES), tab)] * 3
        + [pl.BlockSpec((ms, LANES), fixed)] * 3
        + [pl.BlockSpec((QKV_CHUNK, QKV_CHUNK), fixed)],
        out_specs=[pl.BlockSpec((tm, n), row) for n, _ in widths]
        + [pl.BlockSpec((ms, n), fixed) for n, _ in widths],
        out_shape=[jax.ShapeDtypeStruct((m, n), dt) for n, dt in widths]
        + [jax.ShapeDtypeStruct((ms, n), dt) for n, dt in widths],
        scratch_shapes=[
            pltpu.VMEM((D_MODEL, QKV_DIM), BF16),
            pltpu.VMEM((tm, D_MODEL), BF16),
            pltpu.VMEM((ms, D_MODEL), BF16),
        ],
        compiler_params=pltpu.CompilerParams(
            dimension_semantics=("arbitrary",), vmem_limit_bytes=VMEM_LIMIT),
        name="qkv",
    )(x, xs, g, w, gain, *tabs, *tabs_s, ones)


def _softmax_av(s, sink, v):
    mx = jnp.maximum(jnp.max(s, axis=-1, keepdims=True), sink)
    p = jnp.exp(s - mx)
    den = jnp.sum(p, axis=-1, keepdims=True) + jnp.exp(sink - mx)
    return _dot(p.astype(BF16), v) / den


SLOTS = 4
CHUNK = SLOTS * HEAD_DIM


def _attn_main_body(sink_ref, q_ref, k_ref, v_ref, km_ref, vm_ref, x_ref, wo_ref, o_ref, oh_ref,
                    *, nblk):
    rows2 = 2 * WINDOW
    r = lax.broadcasted_iota(jnp.int32, (rows2, WINDOW), 0) % WINDOW
    c = lax.broadcasted_iota(jnp.int32, (rows2, WINDOW), 1)
    prev = c > r
    top = lax.broadcasted_iota(jnp.int32, (rows2, 1), 0) < WINDOW
    lane_slot = lax.broadcasted_iota(jnp.int32, (rows2, CHUNK), 1) // HEAD_DIM
    slot_mask = [(lane_slot == s).astype(F32).astype(BF16) for s in range(SLOTS)]
    zero = jnp.zeros((rows2, WINDOW), F32)
    step = pl.program_id(1)
    seen = jnp.logical_not(prev) | (c >= WINDOW - N_META)

    def block(i):
        row0 = i * WINDOW
        cur0 = pl.multiple_of(step * (nblk * WINDOW) + row0, WINDOW)
        past0 = pl.multiple_of(jnp.maximum(cur0 - WINDOW, 0), WINDOW)
        may_open = i == 0
        opens = cur0 == 0
        visible = seen | jnp.logical_not(opens)
        q_rows = pl.ds(row0, WINDOW)

        def keys_values(kv):
            lanes = slice(kv * LANES, (kv + 1) * LANES)
            k_past = k_ref[0, pl.ds(past0, WINDOW), lanes]
            v_past = v_ref[0, pl.ds(past0, WINDOW), lanes]
            if may_open:
                k_past = jnp.where(opens, km_ref[0, :, lanes], k_past)
                v_past = jnp.where(opens, vm_ref[0, :, lanes], v_past)
            kd = jnp.concatenate([k_past, k_ref[0, pl.ds(cur0, WINDOW), lanes]], axis=0)
            vd = jnp.concatenate([v_past, v_ref[0, pl.ds(cur0, WINDOW), lanes]], axis=0)
            return jnp.concatenate([kd, kd], axis=1), jnp.concatenate([vd, vd], axis=1)

        def scores(kv, k4):
            g0 = kv * GROUP * HEAD_DIM
            qst = jnp.concatenate(
                [q_ref[q_rows, g0:g0 + CHUNK], q_ref[q_rows, g0 + CHUNK:g0 + 2 * CHUNK]], axis=0)
            return [_dot_nt(qst, k4 * slot_mask[s]) for s in range(SLOTS)]

        def weights(kv, s, sc):
            sf = jnp.where(prev, sc[:, :WINDOW], sc[:, WINDOW:])
            if may_open:
                sf = jnp.where(visible, sf, NEG)
            sink = jnp.where(top, sink_ref[kv * GROUP + s], sink_ref[kv * GROUP + SLOTS + s])
            mx = jnp.maximum(jnp.max(sf, axis=-1, keepdims=True), sink)
            p = jnp.exp(sf - mx)
            den = jnp.sum(p, axis=-1, keepdims=True) + jnp.exp(sink - mx)
            p = p * (1.0 / den)
            p2 = jnp.concatenate([jnp.where(prev, p, zero), jnp.where(prev, zero, p)], axis=1)
            return p2.astype(BF16)

        def outputs(kv, ps, v4):
            acc = jnp.zeros((rows2, CHUNK), F32)
            for s in range(SLOTS):
                acc = acc + _dot(ps[s], v4 * slot_mask[s])
            g0 = kv * GROUP * HEAD_DIM
            oh_ref[q_rows, g0:g0 + CHUNK] = acc[:WINDOW].astype(BF16)
            oh_ref[q_rows, g0 + CHUNK:g0 + 2 * CHUNK] = acc[WINDOW:].astype(BF16)

        kvs = [keys_values(kv) for kv in range(N_KV_HEADS)]
        sc = scores(0, kvs[0][0])
        ps = None
        for kv in range(N_KV_HEADS):
            sc_next = scores(kv + 1, kvs[kv + 1][0]) if kv + 1 < N_KV_HEADS else None
            if ps is not None:
                outputs(kv - 1, ps, kvs[kv - 1][1])
            ps = [weights(kv, s, sc[s]) for s in range(SLOTS)]
            sc = sc_next
        outputs(N_KV_HEADS - 1, ps, kvs[N_KV_HEADS - 1][1])

    for i in range(nblk):
        block(i)
    o_ref[...] = x_ref[...] + _dot(oh_ref[...], wo_ref[...])


def _attn_main(sinks, q, kd, vd, km, vm, x, wo, layer, *, tq):
    nt = SEQ // tq
    row = lambda b, t: (b * nt + t, 0)
    batch = lambda b, t: (b, 0, 0)
    single = pl.Buffered(1)
    return pl.pallas_call(
        functools.partial(_attn_main_body, nblk=tq // WINDOW),
        grid=(BATCH, nt),
        in_specs=[
            pl.BlockSpec(memory_space=pltpu.SMEM),
            pl.BlockSpec((tq, D_MODEL), row),
            pl.BlockSpec((1, SEQ, 2 * KV_DIM), batch),
            pl.BlockSpec((1, SEQ, 2 * KV_DIM), batch),
            pl.BlockSpec((1, WINDOW, 2 * KV_DIM), batch),
            pl.BlockSpec((1, WINDOW, 2 * KV_DIM), batch),
            pl.BlockSpec((tq, D_MODEL), row),
            pl.BlockSpec((None, D_MODEL, D_MODEL), lambda b, t: (layer, 0, 0), pipeline_mode=single),
        ],
        out_specs=pl.BlockSpec((tq, D_MODEL), row),
        out_shape=jax.ShapeDtypeStruct((N_MAIN, D_MODEL), F32),
        scratch_shapes=[pltpu.VMEM((tq, D_MODEL), BF16)],
        compiler_params=pltpu.CompilerParams(
            dimension_semantics=("arbitrary", "arbitrary"), vmem_limit_bytes=VMEM_LIMIT),
        name="attn_main",
    )(sinks, q, kd, vd, km, vm, x, wo)


def _attn_small_body(sink_ref, q_ref, k_ref, v_ref, ck_hbm, cv_hbm, kn_ref, vn_ref, x_ref, wo_hbm,
                     o_ref, nk_ref, nv_ref, oh_ref, ck_ref, cv_ref, wo_ref, sems, *, layer):
    fetches = [pltpu.make_async_copy(src.at[layer], dst, sems.at[n])
               for n, (src, dst) in enumerate(((ck_hbm, ck_ref), (cv_hbm, cv_ref), (wo_hbm, wo_ref)))]
    for fetch in fetches:
        fetch.start()

    nm = BATCH * N_META
    assert nm == DEC_BATCH
    nq = GROUP * DEC_BATCH
    lower = lax.broadcasted_iota(jnp.int32, (1, LANES), 1) < HEAD_DIM

    def stacked(row0, kv):
        tiles = []
        for h in range(kv * GROUP, (kv + 1) * GROUP, 2):
            pair = q_ref[row0:row0 + DEC_BATCH, h * HEAD_DIM:(h + 2) * HEAD_DIM].astype(F32)
            swapped = pltpu.roll(pair, HEAD_DIM, 1)
            tiles += [jnp.where(lower, pair, swapped), jnp.where(lower, swapped, pair)]
        return jnp.concatenate(tiles, axis=0)

    def sink_column(kv):
        return jnp.concatenate([jnp.full((DEC_BATCH, 1), sink_ref[h], F32)
                                for h in range(kv * GROUP, (kv + 1) * GROUP)], axis=0)

    def unstack(o, row0, kv):
        for gi, h in enumerate(range(kv * GROUP, (kv + 1) * GROUP)):
            oh_ref[row0:row0 + DEC_BATCH, h * HEAD_DIM:(h + 1) * HEAD_DIM] = (
                o[gi * DEC_BATCH:(gi + 1) * DEC_BATCH].astype(BF16))

    r = lax.broadcasted_iota(jnp.int32, (nq, nm), 0) % nm
    c = lax.broadcasted_iota(jnp.int32, (nq, nm), 1)
    mmask = (c <= r) & ((r < N_META) == (c < N_META))
    km = k_ref[0:nm, :].astype(BF16)
    vm = v_ref[0:nm, :].astype(BF16)
    for kv in range(N_KV_HEADS):
        ks = slice(kv * HEAD_DIM, (kv + 1) * HEAD_DIM)
        qm = stacked(0, kv)[:, :HEAD_DIM].astype(BF16)
        s = jnp.where(mmask, _dot_nt(qm, km[:, ks]), NEG)
        unstack(_softmax_av(s, sink_column(kv), vm[:, ks]), 0, kv)

    wide = DEC_BATCH * HEAD_DIM
    own = (lax.broadcasted_iota(jnp.int32, (nq, wide), 0) % DEC_BATCH
           == lax.broadcasted_iota(jnp.int32, (nq, wide), 1) // HEAD_DIM).astype(F32)
    live = lax.broadcasted_iota(jnp.int32, (nq, WINDOW), 1) >= 1
    fetches[0].wait()
    fetches[1].wait()
    for kv in range(N_KV_HEADS):
        ks = slice(kv * HEAD_DIM, (kv + 1) * HEAD_DIM)
        kts = ck_ref[:, kv].reshape(wide, WINDOW).astype(BF16)
        vts = cv_ref[:, kv].reshape(wide, WINDOW).astype(BF16)
        qd = stacked(nm, kv)
        qs = qd[:, :HEAD_DIM]
        qblk = (jnp.tile(qd, (1, wide // LANES)) * own).astype(BF16)
        sink = sink_column(kv)
        k_new = jnp.tile(k_ref[nm:, ks], (GROUP, 1))
        v_new = jnp.tile(v_ref[nm:, ks], (GROUP, 1))
        s = jnp.where(live, _dot(qblk, kts), NEG)
        s_new = jnp.sum(qs * k_new, axis=-1, keepdims=True)
        mx = jnp.maximum(jnp.maximum(jnp.max(s, axis=-1, keepdims=True), s_new), sink)
        p = jnp.exp(s - mx)
        p_new = jnp.exp(s_new - mx)
        den = jnp.sum(p, axis=-1, keepdims=True) + p_new + jnp.exp(sink - mx)
        spread = _dot_nt(p.astype(BF16), vts) * own
        folded = spread[:, :LANES]
        for t in range(1, wide // LANES):
            folded = folded + spread[:, t * LANES:(t + 1) * LANES]
        pv = (folded + pltpu.roll(folded, HEAD_DIM, 1))[:, :HEAD_DIM]
        unstack((pv + p_new * v_new) / den, nm, kv)

    last = lax.broadcasted_iota(jnp.int32, (KV_DIM, WINDOW), 1) == WINDOW - 1
    for b in range(DEC_BATCH):
        for cache, new, out in ((ck_ref, kn_ref, nk_ref), (cv_ref, vn_ref, nv_ref)):
            rolled = pltpu.roll(cache[b].reshape(KV_DIM, WINDOW), WINDOW - 1, 1)
            out[b] = jnp.where(last, new[:, b:b + 1], rolled).reshape(N_KV_HEADS, HEAD_DIM, WINDOW)

    fetches[2].wait()
    o_ref[...] = x_ref[...] + _dot(oh_ref[...], wo_ref[...])


def _attn_small(sinks, q, k, v, ck, cv, kn, vn, x, wo, layer):
    single = pl.Buffered(1)
    cache_shape = (DEC_BATCH, N_KV_HEADS, HEAD_DIM, WINDOW)

    def whole(shape):
        return pl.BlockSpec(shape, lambda i: (0,) * len(shape), pipeline_mode=single)

    anywhere = pl.BlockSpec(memory_space=pl.ANY)
    return pl.pallas_call(
        functools.partial(_attn_small_body, layer=layer),
        grid=(1,),
        in_specs=[pl.BlockSpec(memory_space=pltpu.SMEM)]
        + [whole(a.shape) for a in (q, k, v)] + [anywhere, anywhere]
        + [whole(a.shape) for a in (kn, vn, x)] + [anywhere],
        out_specs=[whole((N_SMALL, D_MODEL)), whole(cache_shape), whole(cache_shape)],
        out_shape=[
            jax.ShapeDtypeStruct((N_SMALL, D_MODEL), F32),
            jax.ShapeDtypeStruct(cache_shape, F32),
            jax.ShapeDtypeStruct(cache_shape, F32),
        ],
        scratch_shapes=[
            pltpu.VMEM((N_SMALL, D_MODEL), BF16),
            pltpu.VMEM(cache_shape, F32),
            pltpu.VMEM(cache_shape, F32),
            pltpu.VMEM((D_MODEL, D_MODEL), BF16),
            pltpu.SemaphoreType.DMA((3,)),
        ],
        compiler_params=pltpu.CompilerParams(vmem_limit_bytes=VMEM_LIMIT),
        name="attn_small",
    )(sinks, q, k, v, ck, cv, kn, vn, x, wo)


def _rope_tables(pos):
    half = ROT_DIM // 2
    n = pos.shape[0]
    inv = jnp.float32(ROPE_THETA) ** (-jnp.arange(half, dtype=F32) * 2.0 / ROT_DIM)
    ang = pos.astype(F32)[:, None] * inv[None, :]
    cos, sin = jnp.cos(ang), jnp.sin(ang)
    rest = HEAD_DIM - ROT_DIM
    cos64 = jnp.concatenate([cos, cos, jnp.ones((n, rest), F32)], axis=1)
    sa64 = jnp.concatenate([-sin, jnp.zeros((n, half + rest), F32)], axis=1)
    sb64 = jnp.concatenate([jnp.zeros((n, half), F32), sin, jnp.zeros((n, rest), F32)], axis=1)
    rep = LANES // HEAD_DIM
    return tuple(jnp.tile(t, (1, rep)) for t in (cos64, sa64, sb64))


def kernel(x_prompt, x_sample, state_pool, cache_k, cache_v, meta_tokens, norm_mix, norm_ffn,
           pool_w, pool_scale, w_qkv, w_o, q_norm, k_norm, sinks, w_gate, w_up, w_down):
    assert x_prompt.shape == (BATCH, SEQ, D_MODEL) and x_sample.shape == (DEC_BATCH, 1, D_MODEL)
    nm = BATCH * N_META
    meta = meta_tokens.astype(F32)
    xm = x_prompt.reshape(N_MAIN, D_MODEL)
    xs = jnp.concatenate([meta, meta, x_sample.reshape(DEC_BATCH, D_MODEL)], axis=0)

    tabs_main = _rope_tables(N_META + jnp.arange(SEQ))
    tabs_small = _rope_tables(jnp.concatenate(
        [jnp.arange(N_META), jnp.arange(N_META), jnp.full((DEC_BATCH,), PAST_LEN)]))
    head_of_col = jnp.arange(QKV_CHUNK) // HEAD_DIM
    ones = (head_of_col[:, None] == head_of_col[None, :]).astype(BF16)

    wo = w_o.astype(BF16)
    st = jnp.swapaxes(state_pool, 1, 2)
    ck = jnp.transpose(cache_k, (0, 1, 3, 4, 2))
    cv = jnp.transpose(cache_v, (0, 1, 3, 4, 2))

    pool_p, pool_s, kp_l, vp_l, ks_l, vs_l = [], [], [], [], [], []
    for i in range(DEPTH):
        j = i // 2
        g_mix = norm_mix[i].reshape(1, D_MODEL)
        if i % 2 == 0:
            w = pool_w[j].astype(BF16)
            sc = pool_scale[j].reshape(1, D_MODEL)
            xm_new, h_last = _pool_main(xm, xs, g_mix, w, sc, tp=1024)
            xs, ns = _pool_small(xs, st, j, g_mix, w, sc)
            xm = xm_new
            pool_p.append(h_last[:, N_META - POOL_STATE:])
            pool_s.append(jnp.swapaxes(ns, 0, 1))
        else:
            gain = jnp.concatenate(
                [jnp.tile(q_norm[j], N_HEADS), jnp.tile(k_norm[j], N_KV_HEADS)]).reshape(1, QK_DIM)
            (q_m, k_m, v_m, kd_m, vd_m, q_s, k_s, v_s, kd_s, vd_s) = _qkv(
                xm, xs, g_mix, w_qkv, j, gain, tabs_main, tabs_small, ones, tm=512)

            def meta_past(t):
                t = t[:nm].reshape(BATCH, N_META, 2 * KV_DIM)
                return jnp.pad(t, ((0, 0), (WINDOW - N_META, 0), (0, 0)))

            kn = k_s[nm:].T
            vn = v_s[nm:].T
            xm = _attn_main(sinks[j], q_m, kd_m.reshape(BATCH, SEQ, 2 * KV_DIM),
                            vd_m.reshape(BATCH, SEQ, 2 * KV_DIM), meta_past(kd_s), meta_past(vd_s),
                            xm, wo, j, tq=512)
            xs, nk, nv = _attn_small(sinks[j], q_s, k_s, v_s, ck, cv, kn, vn, xs, wo, j)
            last = lambda t: t.reshape(BATCH, SEQ, KV_DIM)[:, -WINDOW:].reshape(
                BATCH, WINDOW, N_KV_HEADS, HEAD_DIM)
            kp_l.append(last(k_m))
            vp_l.append(last(v_m))
            ks_l.append(jnp.transpose(nk, (0, 3, 1, 2)))
            vs_l.append(jnp.transpose(nv, (0, 3, 1, 2)))
        g_ffn = norm_ffn[i].reshape(1, D_MODEL)
        xm, xs = _ffn(xm, xs, g_ffn, w_gate, w_up, w_down, i, tm=1024, tf=512)

    return (xm.reshape(BATCH, SEQ, D_MODEL), xs[nm:].reshape(DEC_BATCH, 1, D_MODEL),
            jnp.stack(pool_p), jnp.stack(kp_l), jnp.stack(vp_l),
            jnp.stack(pool_s), jnp.stack(ks_l), jnp.stack(vs_l))
```

```python
import functools

import jax
import jax.numpy as jnp
from jax import lax
from jax.experimental import pallas as pl
from jax.experimental.pallas import tpu as pltpu

F32 = jnp.float32
BF16 = jnp.bfloat16

D_MODEL = 2048
BATCH = 2
SEQ = 4096
DEPTH = 4
DEC_BATCH = 32
PAST_LEN = 16384
N_META = 16
POOL_WINDOWS = (2, 4, 8, 16)
N_POOL_GROUPS = len(POOL_WINDOWS)
POOL_GROUP = D_MODEL // N_POOL_GROUPS
POOL_STATE = max(POOL_WINDOWS) - 1
HEAD_DIM = 64
N_HEADS = D_MODEL // HEAD_DIM
N_KV_HEADS = 4
GROUP = N_HEADS // N_KV_HEADS
WINDOW = 128
ROT_DIM = HEAD_DIM // 4
ROPE_THETA = 500000.0
D_FF = 5632
EPS = 1e-6
NEG = -1e30

N_MAIN = BATCH * SEQ
N_SMALL = BATCH * N_META + DEC_BATCH
KV_DIM = N_KV_HEADS * HEAD_DIM
QK_DIM = D_MODEL + KV_DIM
QKV_DIM = D_MODEL + 2 * KV_DIM
LANES = 128

VMEM_LIMIT = 60 * 1024 * 1024


def _rms(x, g):
    ms = jnp.mean(x * x, axis=-1, keepdims=True)
    return x * lax.rsqrt(ms + EPS) * g


def _dot(a, b):
    return jnp.dot(a, b, preferred_element_type=F32)


def _dot_nt(a, b):
    return lax.dot_general(a, b, (((1,), (1,)), ((), ())), preferred_element_type=F32)


def _split_bf16(x):
    hi = x.astype(BF16)
    lo = (x - hi.astype(F32)).astype(BF16)
    return hi, lo


def _ffn_body(x_ref, xs_ref, g_ref, wg_ref, wu_ref, wd_ref, o_hbm, os_ref, h_ref, acc_ref, sem):
    i, j = pl.program_id(0), pl.program_id(1)
    last_i, last_j = pl.num_programs(0) - 1, pl.num_programs(1) - 1
    tm, ms = x_ref.shape[0], xs_ref.shape[0]

    def writeback(tile):
        rows = pl.ds(pl.multiple_of(tile * tm, tm), tm)
        return pltpu.make_async_copy(acc_ref, o_hbm.at[rows, :], sem.at[0])

    def normalise(src_ref, row0):
        h_ref[row0:row0 + src_ref.shape[0], :] = _rms(src_ref[...], g_ref[...]).astype(BF16)

    def accumulate(rows, first):
        h = h_ref[0:rows, :]
        gate = _dot(h, wg_ref[...].astype(BF16))
        up = _dot(h, wu_ref[...].astype(BF16))
        act = (gate * jax.nn.sigmoid(gate) * up).astype(BF16)
        if first and rows == tm:
            writeback(i - 1).wait()
        y = _dot(act, wd_ref[...].astype(BF16))
        acc_ref[...] = (x_ref[...] if first else acc_ref[...]) + y[:tm]
        if rows > tm:
            os_ref[...] = (xs_ref[...] if first else os_ref[...]) + y[tm:]

    pl.when(j == 0)(lambda: normalise(x_ref, 0))
    pl.when((i == 0) & (j == 0))(lambda: normalise(xs_ref, tm))
    pl.when((i == 0) & (j == 0))(lambda: accumulate(tm + ms, True))
    pl.when((i == 0) & (j > 0))(lambda: accumulate(tm + ms, False))
    pl.when((i > 0) & (j == 0))(lambda: accumulate(tm, True))
    pl.when((i > 0) & (j > 0))(lambda: accumulate(tm, False))
    pl.when(j == last_j)(lambda: writeback(i).start())
    pl.when((i == last_i) & (j == last_j))(lambda: writeback(i).wait())


def _ffn(x, xs, g, wg, wu, wd, layer, *, tm, tf):
    m, ms = x.shape[0], xs.shape[0]
    return pl.pallas_call(
        _ffn_body,
        grid=(m // tm, D_FF // tf),
        in_specs=[
            pl.BlockSpec((tm, D_MODEL), lambda i, j: (i, 0)),
            pl.BlockSpec((ms, D_MODEL), lambda i, j: (0, 0)),
            pl.BlockSpec((1, D_MODEL), lambda i, j: (0, 0)),
            pl.BlockSpec((None, D_MODEL, tf), lambda i, j: (layer, 0, j)),
            pl.BlockSpec((None, D_MODEL, tf), lambda i, j: (layer, 0, j)),
            pl.BlockSpec((None, tf, D_MODEL), lambda i, j: (layer, j, 0)),
        ],
        out_specs=[
            pl.BlockSpec(memory_space=pl.ANY),
            pl.BlockSpec((ms, D_MODEL), lambda i, j: (0, 0)),
        ],
        out_shape=[
            jax.ShapeDtypeStruct((m, D_MODEL), F32),
            jax.ShapeDtypeStruct((ms, D_MODEL), F32),
        ],
        scratch_shapes=[
            pltpu.VMEM((tm + ms, D_MODEL), BF16),
            pltpu.VMEM((tm, D_MODEL), F32),
            pltpu.SemaphoreType.DMA((1,)),
        ],
        compiler_params=pltpu.CompilerParams(
            dimension_semantics=("arbitrary", "arbitrary"), vmem_limit_bytes=VMEM_LIMIT),
        name="ffn",
    )(x, xs, g, wg, wu, wd)


def _pool_project(diffs, w_ref, sc_ref):
    ys = [_dot(d.astype(BF16), w_ref[g]) for g, d in enumerate(diffs)]
    return jnp.concatenate(ys, axis=1) * sc_ref[...]


POOL_RING = 3


def _pool_main_body(x_hbm, meta_ref, g_ref, w_ref, sc_ref, o_ref, hl_ref, ext_ref, xbuf, sems,
                    *, tp):
    t = pl.program_id(1)
    s = pl.program_id(0) * pl.num_programs(1) + t
    nsteps = pl.num_programs(0) * pl.num_programs(1)

    def fetch(step):
        slot = step % POOL_RING
        row0 = step * tp if isinstance(step, int) else pl.multiple_of(step * tp, tp)
        return pltpu.make_async_copy(x_hbm.at[pl.ds(row0, tp), :], xbuf.at[slot], sems.at[slot])

    @pl.when(s == 0)
    def _():
        for ahead in range(POOL_RING - 1):
            fetch(ahead).start()

    @pl.when(s + POOL_RING - 1 < nsteps)
    def _():
        fetch(s + POOL_RING - 1).start()

    fetch(s).wait()

    @pl.when(t == 0)
    def _():
        ext_ref[0:N_META, :] = _rms(meta_ref[...], g_ref[...])

    x = xbuf[s % POOL_RING]
    h = _rms(x, g_ref[...])
    ext_ref[N_META:N_META + tp, :] = h
    diffs = []
    for g, w in enumerate(POOL_WINDOWS):
        c0, c1 = g * POOL_GROUP, (g + 1) * POOL_GROUP
        s = ext_ref[:, c0:c1]
        shift = 1
        while shift < w:
            s = s + pltpu.roll(s, shift, 0)
            shift *= 2
        diffs.append(s[N_META:] * (1.0 / w) - h[:, c0:c1])
    o_ref[...] = x + _pool_project(diffs, w_ref, sc_ref)
    tail = ext_ref[tp:tp + N_META, :]
    ext_ref[0:N_META, :] = tail

    @pl.when(t == pl.num_programs(1) - 1)
    def _():
        hl_ref[0] = tail


def _pool_main(xm, xs, g, w, sc, *, tp):
    nt = SEQ // tp
    return pl.pallas_call(
        functools.partial(_pool_main_body, tp=tp),
        grid=(BATCH, nt),
        in_specs=[
            pl.BlockSpec(memory_space=pl.ANY),
            pl.BlockSpec((N_META, D_MODEL), lambda b, t: (b, 0)),
            pl.BlockSpec((1, D_MODEL), lambda b, t: (0, 0)),
            pl.BlockSpec((N_POOL_GROUPS, POOL_GROUP, POOL_GROUP), lambda b, t: (0, 0, 0)),
            pl.BlockSpec((1, D_MODEL), lambda b, t: (0, 0)),
        ],
        out_specs=[
            pl.BlockSpec((tp, D_MODEL), lambda b, t: (b * nt + t, 0)),
            pl.BlockSpec((1, N_META, D_MODEL), lambda b, t: (b, 0, 0)),
        ],
        out_shape=[
            jax.ShapeDtypeStruct((N_MAIN, D_MODEL), F32),
            jax.ShapeDtypeStruct((BATCH, N_META, D_MODEL), F32),
        ],
        scratch_shapes=[
            pltpu.VMEM((tp + N_META, D_MODEL), F32),
            pltpu.VMEM((POOL_RING, tp, D_MODEL), F32),
            pltpu.SemaphoreType.DMA((POOL_RING,)),
        ],
        compiler_params=pltpu.CompilerParams(
            dimension_semantics=("arbitrary", "arbitrary"), vmem_limit_bytes=VMEM_LIMIT),
        name="pool_main",
    )(xm, xs, g, w, sc)


def _pool_small_body(x_ref, st_ref, g_ref, w_ref, sc_ref, o_ref, ns_ref, ext_ref):
    nm = BATCH * N_META
    x = x_ref[...]
    h = _rms(x, g_ref[...])
    for b in range(BATCH):
        ext_ref[b, 0:N_META, :] = jnp.zeros((N_META, D_MODEL), F32)
        ext_ref[b, N_META:2 * N_META, :] = h[b * N_META:(b + 1) * N_META]
    hs = h[nm:]
    row = lax.broadcasted_iota(jnp.int32, (N_META, 1), 0)
    diffs = []
    for g, w in enumerate(POOL_WINDOWS):
        c0, c1 = g * POOL_GROUP, (g + 1) * POOL_GROUP
        cnt = jnp.minimum(row + 1, w).astype(F32)
        parts = []
        for b in range(BATCH):
            s = ext_ref[b, N_META:2 * N_META, c0:c1]
            for k in range(1, w):
                s = s + ext_ref[b, N_META - k:2 * N_META - k, c0:c1]
            parts.append(s / cnt)
        s = hs[:, c0:c1]
        for k in range(1, w):
            s = s + st_ref[POOL_STATE - k, :, c0:c1]
        parts.append(s / float(w))
        diffs.append(jnp.concatenate(parts, axis=0) - h[:, c0:c1])
    o_ref[...] = x + _pool_project(diffs, w_ref, sc_ref)
    ns_ref[0:POOL_STATE - 1] = st_ref[1:POOL_STATE]
    ns_ref[POOL_STATE - 1] = hs


def _pool_small(xs, st, layer, g, w, sc):
    single = pl.Buffered(1)

    def whole(shape):
        return pl.BlockSpec(shape, lambda i: (0,) * len(shape), pipeline_mode=single)

    state_shape = (POOL_STATE, DEC_BATCH, D_MODEL)
    return pl.pallas_call(
        _pool_small_body,
        grid=(1,),
        in_specs=[
            whole(xs.shape),
            pl.BlockSpec((None,) + state_shape, lambda i: (layer, 0, 0, 0), pipeline_mode=single),
            whole(g.shape), whole(w.shape), whole(sc.shape),
        ],
        out_specs=[whole((N_SMALL, D_MODEL)), whole(state_shape)],
        out_shape=[
            jax.ShapeDtypeStruct((N_SMALL, D_MODEL), F32),
            jax.ShapeDtypeStruct((POOL_STATE, DEC_BATCH, D_MODEL), F32),
        ],
        scratch_shapes=[pltpu.VMEM((BATCH, 2 * N_META, D_MODEL), F32)],
        compiler_params=pltpu.CompilerParams(vmem_limit_bytes=VMEM_LIMIT),
        name="pool_small",
    )(xs, st, g, w, sc)


QKV_CHUNK = 4 * HEAD_DIM


def _qkv_rows(x_ref, tab_refs, g_ref, w_ref, gain_ref, ones_ref, out_refs, h_ref):
    cos_ref, sa_ref, sb_ref = tab_refs
    q_ref, k_ref, v_ref, kd_ref, vd_ref = out_refs
    h_ref[...] = _rms(x_ref[...], g_ref[...]).astype(BF16)
    cos, sa, sb = cos_ref[...], sa_ref[...], sb_ref[...]
    half = ROT_DIM // 2

    def norm_rope(a, c0):
        sq_hi, sq_lo = _split_bf16(a * a)
        ss = _dot(sq_hi, ones_ref[...]) + _dot(sq_lo, ones_ref[...])
        y = a * lax.rsqrt(ss * (1.0 / HEAD_DIM) + EPS) * gain_ref[:, c0:c0 + QKV_CHUNK]
        tiles = []
        for u in range(QKV_CHUNK // LANES):
            yu = y[:, u * LANES:(u + 1) * LANES]
            tiles.append(yu * cos + pltpu.roll(yu, LANES - half, 1) * sa + pltpu.roll(yu, half, 1) * sb)
        return jnp.concatenate(tiles, axis=1)

    lower = lax.broadcasted_iota(jnp.int32, (1, LANES), 1) < HEAD_DIM

    def twice(t):
        tiles = []
        for u in range(KV_DIM // LANES):
            tu = t[:, u * LANES:(u + 1) * LANES]
            swapped = pltpu.roll(tu, HEAD_DIM, 1)
            tiles += [jnp.where(lower, tu, swapped), jnp.where(lower, swapped, tu)]
        return jnp.concatenate(tiles, axis=1).astype(BF16)

    wide = 2 * QKV_CHUNK
    nwide = QKV_DIM // wide
    project = lambda c: _dot(h_ref[...], w_ref[:, c * wide:(c + 1) * wide])
    a = project(0)
    for c in range(nwide):
        a_next = project(c + 1) if c + 1 < nwide else None
        c0 = c * wide
        if c0 < D_MODEL:
            for u in range(2):
                y = norm_rope(a[:, u * QKV_CHUNK:(u + 1) * QKV_CHUNK], c0 + u * QKV_CHUNK)
                q_ref[:, c0 + u * QKV_CHUNK:c0 + (u + 1) * QKV_CHUNK] = (
                    y * HEAD_DIM ** -0.5).astype(BF16)
        else:
            k = norm_rope(a[:, :KV_DIM], D_MODEL)
            v = a[:, KV_DIM:]
            k_ref[...] = k
            v_ref[...] = v
            kd_ref[...] = twice(k)
            vd_ref[...] = twice(v)
        a = a_next


def _qkv_body(*refs):
    x_ref, xs_ref, g_ref, w_ref, gain_ref = refs[:5]
    tabs, tabs_s, ones_ref = refs[5:8], refs[8:11], refs[11]
    outs, outs_s = refs[12:17], refs[17:22]
    wb_ref, h_ref, hs_ref = refs[22:]

    @pl.when(pl.program_id(0) == 0)
    def _():
        wb_ref[...] = w_ref[...].astype(BF16)
        _qkv_rows(xs_ref, tabs_s, g_ref, wb_ref, gain_ref, ones_ref, outs_s, hs_ref)

    half = x_ref.shape[0] // 2
    for r in range(2):
        rows = pl.ds(r * half, half)
        _qkv_rows(x_ref.at[rows], [t.at[rows] for t in tabs], g_ref, wb_ref, gain_ref, ones_ref,
                  [o.at[rows] for o in outs], h_ref.at[rows])


def _qkv(x, xs, g, w, layer, gain, tabs, tabs_s, ones, *, tm):
    m, ms = x.shape[0], xs.shape[0]
    nt = tabs[0].shape[0] // tm
    row = lambda i: (i, 0)
    tab = lambda i: (i % nt, 0)
    fixed = lambda i: (0, 0)
    widths = ((D_MODEL, BF16), (KV_DIM, F32), (KV_DIM, F32), (2 * KV_DIM, BF16), (2 * KV_DIM, BF16))
    return pl.pallas_call(
        _qkv_body,
        grid=(m // tm,),
        in_specs=[
            pl.BlockSpec((tm, D_MODEL), row),
            pl.BlockSpec((ms, D_MODEL), fixed),
            pl.BlockSpec((1, D_MODEL), fixed),
            pl.BlockSpec((None, D_MODEL, QKV_DIM), lambda i: (layer, 0, 0),
                         pipeline_mode=pl.Buffered(1)),
            pl.BlockSpec((1, QK_DIM), fixed),
        ]
        + [pl.BlockSpec((tm, LANES), tab)] * 3
        + [pl.BlockSpec((ms, LANES), fixed)] * 3
        + [pl.BlockSpec((QKV_CHUNK, QKV_CHUNK), fixed)],
        out_specs=[pl.BlockSpec((tm, n), row) for n, _ in widths]
        + [pl.BlockSpec((ms, n), fixed) for n, _ in widths],
        out_shape=[jax.ShapeDtypeStruct((m, n), dt) for n, dt in widths]
        + [jax.ShapeDtypeStruct((ms, n), dt) for n, dt in widths],
        scratch_shapes=[
            pltpu.VMEM((D_MODEL, QKV_DIM), BF16),
            pltpu.VMEM((tm, D_MODEL), BF16),
            pltpu.VMEM((ms, D_MODEL), BF16),
        ],
        compiler_params=pltpu.CompilerParams(
            dimension_semantics=("arbitrary",), vmem_limit_bytes=VMEM_LIMIT),
        name="qkv",
    )(x, xs, g, w, gain, *tabs, *tabs_s, ones)


def _softmax_av(s, sink, v):
    mx = jnp.maximum(jnp.max(s, axis=-1, keepdims=True), sink)
    p = jnp.exp(s - mx)
    den = jnp.sum(p, axis=-1, keepdims=True) + jnp.exp(sink - mx)
    return _dot(p.astype(BF16), v) / den


SLOTS = 4
CHUNK = SLOTS * HEAD_DIM


def _attn_main_body(sink_ref, q_ref, k_ref, v_ref, km_ref, vm_ref, x_ref, wo_ref, o_ref, oh_ref,
                    *, nblk):
    rows2 = 2 * WINDOW
    r = lax.broadcasted_iota(jnp.int32, (rows2, WINDOW), 0) % WINDOW
    c = lax.broadcasted_iota(jnp.int32, (rows2, WINDOW), 1)
    prev = c > r
    top = lax.broadcasted_iota(jnp.int32, (rows2, 1), 0) < WINDOW
    lane_slot = lax.broadcasted_iota(jnp.int32, (rows2, CHUNK), 1) // HEAD_DIM
    slot_mask = [(lane_slot == s).astype(F32).astype(BF16) for s in range(SLOTS)]
    zero = jnp.zeros((rows2, WINDOW), F32)
    step = pl.program_id(1)
    seen = jnp.logical_not(prev) | (c >= WINDOW - N_META)

    def block(i):
        row0 = i * WINDOW
        cur0 = pl.multiple_of(step * (nblk * WINDOW) + row0, WINDOW)
        past0 = pl.multiple_of(jnp.maximum(cur0 - WINDOW, 0), WINDOW)
        may_open = i == 0
        opens = cur0 == 0
        visible = seen | jnp.logical_not(opens)
        q_rows = pl.ds(row0, WINDOW)

        def keys_values(kv):
            lanes = slice(kv * LANES, (kv + 1) * LANES)
            k_past = k_ref[0, pl.ds(past0, WINDOW), lanes]
            v_past = v_ref[0, pl.ds(past0, WINDOW), lanes]
            if may_open:
                k_past = jnp.where(opens, km_ref[0, :, lanes], k_past)
                v_past = jnp.where(opens, vm_ref[0, :, lanes], v_past)
            kd = jnp.concatenate([k_past, k_ref[0, pl.ds(cur0, WINDOW), lanes]], axis=0)
            vd = jnp.concatenate([v_past, v_ref[0, pl.ds(cur0, WINDOW), lanes]], axis=0)
            return jnp.concatenate([kd, kd], axis=1), jnp.concatenate([vd, vd], axis=1)

        def scores(kv, k4):
            g0 = kv * GROUP * HEAD_DIM
            qst = jnp.concatenate(
                [q_ref[q_rows, g0:g0 + CHUNK], q_ref[q_rows, g0 + CHUNK:g0 + 2 * CHUNK]], axis=0)
            return [_dot_nt(qst, k4 * slot_mask[s]) for s in range(SLOTS)]

        def weights(kv, s, sc):
            sf = jnp.where(prev, sc[:, :WINDOW], sc[:, WINDOW:])
            if may_open:
                sf = jnp.where(visible, sf, NEG)
            sink = jnp.where(top, sink_ref[kv * GROUP + s], sink_ref[kv * GROUP + SLOTS + s])
            mx = jnp.maximum(jnp.max(sf, axis=-1, keepdims=True), sink)
            p = jnp.exp(sf - mx)
            den = jnp.sum(p, axis=-1, keepdims=True) + jnp.exp(sink - mx)
            p = p * (1.0 / den)
            p2 = jnp.concatenate([jnp.where(prev, p, zero), jnp.where(prev, zero, p)], axis=1)
            return p2.astype(BF16)

        def outputs(kv, ps, v4):
            acc = jnp.zeros((rows2, CHUNK), F32)
            for s in range(SLOTS):
                acc = acc + _dot(ps[s], v4 * slot_mask[s])
            g0 = kv * GROUP * HEAD_DIM
            oh_ref[q_rows, g0:g0 + CHUNK] = acc[:WINDOW].astype(BF16)
            oh_ref[q_rows, g0 + CHUNK:g0 + 2 * CHUNK] = acc[WINDOW:].astype(BF16)

        kvs = [keys_values(kv) for kv in range(N_KV_HEADS)]
        sc = scores(0, kvs[0][0])
        ps = None
        for kv in range(N_KV_HEADS):
            sc_next = scores(kv + 1, kvs[kv + 1][0]) if kv + 1 < N_KV_HEADS else None
            if ps is not None:
                outputs(kv - 1, ps, kvs[kv - 1][1])
            ps = [weights(kv, s, sc[s]) for s in range(SLOTS)]
            sc = sc_next
        outputs(N_KV_HEADS - 1, ps, kvs[N_KV_HEADS - 1][1])

    for i in range(nblk):
        block(i)
    o_ref[...] = x_ref[...] + _dot(oh_ref[...], wo_ref[...])


def _attn_main(sinks, q, kd, vd, km, vm, x, wo, layer, *, tq):
    nt = SEQ // tq
    row = lambda b, t: (b * nt + t, 0)
    batch = lambda b, t: (b, 0, 0)
    single = pl.Buffered(1)
    return pl.pallas_call(
        functools.partial(_attn_main_body, nblk=tq // WINDOW),
        grid=(BATCH, nt),
        in_specs=[
            pl.BlockSpec(memory_space=pltpu.SMEM),
            pl.BlockSpec((tq, D_MODEL), row),
            pl.BlockSpec((1, SEQ, 2 * KV_DIM), batch),
            pl.BlockSpec((1, SEQ, 2 * KV_DIM), batch),
            pl.BlockSpec((1, WINDOW, 2 * KV_DIM), batch),
            pl.BlockSpec((1, WINDOW, 2 * KV_DIM), batch),
            pl.BlockSpec((tq, D_MODEL), row),
            pl.BlockSpec((None, D_MODEL, D_MODEL), lambda b, t: (layer, 0, 0), pipeline_mode=single),
        ],
        out_specs=pl.BlockSpec((tq, D_MODEL), row),
        out_shape=jax.ShapeDtypeStruct((N_MAIN, D_MODEL), F32),
        scratch_shapes=[pltpu.VMEM((tq, D_MODEL), BF16)],
        compiler_params=pltpu.CompilerParams(
            dimension_semantics=("arbitrary", "arbitrary"), vmem_limit_bytes=VMEM_LIMIT),
        name="attn_main",
    )(sinks, q, kd, vd, km, vm, x, wo)


def _attn_small_body(sink_ref, q_ref, k_ref, v_ref, ck_hbm, cv_hbm, kn_ref, vn_ref, x_ref, wo_hbm,
                     o_ref, nk_ref, nv_ref, oh_ref, ck_ref, cv_ref, wo_ref, sems, *, layer):
    fetches = [pltpu.make_async_copy(src.at[layer], dst, sems.at[n])
               for n, (src, dst) in enumerate(((ck_hbm, ck_ref), (cv_hbm, cv_ref), (wo_hbm, wo_ref)))]
    for fetch in fetches:
        fetch.start()

    nm = BATCH * N_META
    assert nm == DEC_BATCH
    nq = GROUP * DEC_BATCH
    lower = lax.broadcasted_iota(jnp.int32, (1, LANES), 1) < HEAD_DIM

    def stacked(row0, kv):
        tiles = []
        for h in range(kv * GROUP, (kv + 1) * GROUP, 2):
            pair = q_ref[row0:row0 + DEC_BATCH, h * HEAD_DIM:(h + 2) * HEAD_DIM].astype(F32)
            swapped = pltpu.roll(pair, HEAD_DIM, 1)
            tiles += [jnp.where(lower, pair, swapped), jnp.where(lower, swapped, pair)]
        return jnp.concatenate(tiles, axis=0)

    def sink_column(kv):
        return jnp.concatenate([jnp.full((DEC_BATCH, 1), sink_ref[h], F32)
                                for h in range(kv * GROUP, (kv + 1) * GROUP)], axis=0)

    def unstack(o, row0, kv):
        for gi, h in enumerate(range(kv * GROUP, (kv + 1) * GROUP)):
            oh_ref[row0:row0 + DEC_BATCH, h * HEAD_DIM:(h + 1) * HEAD_DIM] = (
                o[gi * DEC_BATCH:(gi + 1) * DEC_BATCH].astype(BF16))

    r = lax.broadcasted_iota(jnp.int32, (nq, nm), 0) % nm
    c = lax.broadcasted_iota(jnp.int32, (nq, nm), 1)
    mmask = (c <= r) & ((r < N_META) == (c < N_META))
    km = k_ref[0:nm, :].astype(BF16)
    vm = v_ref[0:nm, :].astype(BF16)
    for kv in range(N_KV_HEADS):
        ks = slice(kv * HEAD_DIM, (kv + 1) * HEAD_DIM)
        qm = stacked(0, kv)[:, :HEAD_DIM].astype(BF16)
        s = jnp.where(mmask, _dot_nt(qm, km[:, ks]), NEG)
        unstack(_softmax_av(s, sink_column(kv), vm[:, ks]), 0, kv)

    wide = DEC_BATCH * HEAD_DIM
    own = (lax.broadcasted_iota(jnp.int32, (nq, wide), 0) % DEC_BATCH
           == lax.broadcasted_iota(jnp.int32, (nq, wide), 1) // HEAD_DIM).astype(F32)
    live = lax.broadcasted_iota(jnp.int32, (nq, WINDOW), 1) >= 1
    fetches[0].wait()
    fetches[1].wait()
    for kv in range(N_KV_HEADS):
        ks = slice(kv * HEAD_DIM, (kv + 1) * HEAD_DIM)
        kts = ck_ref[:, kv].reshape(wide, WINDOW).astype(BF16)
        vts = cv_ref[:, kv].reshape(wide, WINDOW).astype(BF16)
        qd = stacked(nm, kv)
        qs = qd[:, :HEAD_DIM]
        qblk = (jnp.tile(qd, (1, wide // LANES)) * own).astype(BF16)
        sink = sink_column(kv)
        k_new = jnp.tile(k_ref[nm:, ks], (GROUP, 1))
        v_new = jnp.tile(v_ref[nm:, ks], (GROUP, 1))
        s = jnp.where(live, _dot(qblk, kts), NEG)
        s_new = jnp.sum(qs * k_new, axis=-1, keepdims=True)
        mx = jnp.maximum(jnp.maximum(jnp.max(s, axis=-1, keepdims=True), s_new), sink)
        p = jnp.exp(s - mx)
        p_new = jnp.exp(s_new - mx)
        den = jnp.sum(p, axis=-1, keepdims=True) + p_new + jnp.exp(sink - mx)
        spread = _dot_nt(p.astype(BF16), vts) * own
        folded = spread[:, :LANES]
        for t in range(1, wide // LANES):
            folded = folded + spread[:, t * LANES:(t + 1) * LANES]
        pv = (folded + pltpu.roll(folded, HEAD_DIM, 1))[:, :HEAD_DIM]
        unstack((pv + p_new * v_new) / den, nm, kv)

    last = lax.broadcasted_iota(jnp.int32, (KV_DIM, WINDOW), 1) == WINDOW - 1
    for b in range(DEC_BATCH):
        for cache, new, out in ((ck_ref, kn_ref, nk_ref), (cv_ref, vn_ref, nv_ref)):
            rolled = pltpu.roll(cache[b].reshape(KV_DIM, WINDOW), WINDOW - 1, 1)
            out[b] = jnp.where(last, new[:, b:b + 1], rolled).reshape(N_KV_HEADS, HEAD_DIM, WINDOW)

    fetches[2].wait()
    o_ref[...] = x_ref[...] + _dot(oh_ref[...], wo_ref[...])


def _attn_small(sinks, q, k, v, ck, cv, kn, vn, x, wo, layer):
    single = pl.Buffered(1)
    cache_shape = (DEC_BATCH, N_KV_HEADS, HEAD_DIM, WINDOW)

    def whole(shape):
        return pl.BlockSpec(shape, lambda i: (0,) * len(shape), pipeline_mode=single)

    anywhere = pl.BlockSpec(memory_space=pl.ANY)
    return pl.pallas_call(
        functools.partial(_attn_small_body, layer=layer),
        grid=(1,),
        in_specs=[pl.BlockSpec(memory_space=pltpu.SMEM)]
        + [whole(a.shape) for a in (q, k, v)] + [anywhere, anywhere]
        + [whole(a.shape) for a in (kn, vn, x)] + [anywhere],
        out_specs=[whole((N_SMALL, D_MODEL)), whole(cache_shape), whole(cache_shape)],
        out_shape=[
            jax.ShapeDtypeStruct((N_SMALL, D_MODEL), F32),
            jax.ShapeDtypeStruct(cache_shape, F32),
            jax.ShapeDtypeStruct(cache_shape, F32),
        ],
        scratch_shapes=[
            pltpu.VMEM((N_SMALL, D_MODEL), BF16),
            pltpu.VMEM(cache_shape, F32),
            pltpu.VMEM(cache_shape, F32),
            pltpu.VMEM((D_MODEL, D_MODEL), BF16),
            pltpu.SemaphoreType.DMA((3,)),
        ],
        compiler_params=pltpu.CompilerParams(vmem_limit_bytes=VMEM_LIMIT),
        name="attn_small",
    )(sinks, q, k, v, ck, cv, kn, vn, x, wo)


def _rope_tables(pos):
    half = ROT_DIM // 2
    n = pos.shape[0]
    inv = jnp.float32(ROPE_THETA) ** (-jnp.arange(half, dtype=F32) * 2.0 / ROT_DIM)
    ang = pos.astype(F32)[:, None] * inv[None, :]
    cos, sin = jnp.cos(ang), jnp.sin(ang)
    rest = HEAD_DIM - ROT_DIM
    cos64 = jnp.concatenate([cos, cos, jnp.ones((n, rest), F32)], axis=1)
    sa64 = jnp.concatenate([-sin, jnp.zeros((n, half + rest), F32)], axis=1)
    sb64 = jnp.concatenate([jnp.zeros((n, half), F32), sin, jnp.zeros((n, rest), F32)], axis=1)
    rep = LANES // HEAD_DIM
    return tuple(jnp.tile(t, (1, rep)) for t in (cos64, sa64, sb64))


def kernel(x_prompt, x_sample, state_pool, cache_k, cache_v, meta_tokens, norm_mix, norm_ffn,
           pool_w, pool_scale, w_qkv, w_o, q_norm, k_norm, sinks, w_gate, w_up, w_down):
    assert x_prompt.shape == (BATCH, SEQ, D_MODEL) and x_sample.shape == (DEC_BATCH, 1, D_MODEL)
    nm = BATCH * N_META
    meta = meta_tokens.astype(F32)
    xm = x_prompt.reshape(N_MAIN, D_MODEL)
    xs = jnp.concatenate([meta, meta, x_sample.reshape(DEC_BATCH, D_MODEL)], axis=0)

    tabs_main = _rope_tables(N_META + jnp.arange(SEQ))
    tabs_small = _rope_tables(jnp.concatenate(
        [jnp.arange(N_META), jnp.arange(N_META), jnp.full((DEC_BATCH,), PAST_LEN)]))
    head_of_col = jnp.arange(QKV_CHUNK) // HEAD_DIM
    ones = (head_of_col[:, None] == head_of_col[None, :]).astype(BF16)

    wo = w_o.astype(BF16)
    st = jnp.swapaxes(state_pool, 1, 2)
    ck = jnp.transpose(cache_k, (0, 1, 3, 4, 2))
    cv = jnp.transpose(cache_v, (0, 1, 3, 4, 2))

    pool_p, pool_s, kp_l, vp_l, ks_l, vs_l = [], [], [], [], [], []
    for i in range(DEPTH):
        j = i // 2
        g_mix = norm_mix[i].reshape(1, D_MODEL)
        if i % 2 == 0:
            w = pool_w[j].astype(BF16)
            sc = pool_scale[j].reshape(1, D_MODEL)
            xm_new, h_last = _pool_main(xm, xs, g_mix, w, sc, tp=512)
            xs, ns = _pool_small(xs, st, j, g_mix, w, sc)
            xm = xm_new
            pool_p.append(h_last[:, N_META - POOL_STATE:])
            pool_s.append(jnp.swapaxes(ns, 0, 1))
        else:
            gain = jnp.concatenate(
                [jnp.tile(q_norm[j], N_HEADS), jnp.tile(k_norm[j], N_KV_HEADS)]).reshape(1, QK_DIM)
            (q_m, k_m, v_m, kd_m, vd_m, q_s, k_s, v_s, kd_s, vd_s) = _qkv(
                xm, xs, g_mix, w_qkv, j, gain, tabs_main, tabs_small, ones, tm=512)

            def meta_past(t):
                t = t[:nm].reshape(BATCH, N_META, 2 * KV_DIM)
                return jnp.pad(t, ((0, 0), (WINDOW - N_META, 0), (0, 0)))

            kn = k_s[nm:].T
            vn = v_s[nm:].T
            xm = _attn_main(sinks[j], q_m, kd_m.reshape(BATCH, SEQ, 2 * KV_DIM),
                            vd_m.reshape(BATCH, SEQ, 2 * KV_DIM), meta_past(kd_s), meta_past(vd_s),
                            xm, wo, j, tq=512)
            xs, nk, nv = _attn_small(sinks[j], q_s, k_s, v_s, ck, cv, kn, vn, xs, wo, j)
            last = lambda t: t.reshape(BATCH, SEQ, KV_DIM)[:, -WINDOW:].reshape(
                BATCH, WINDOW, N_KV_HEADS, HEAD_DIM)
            kp_l.append(last(k_m))
            vp_l.append(last(v_m))
            ks_l.append(jnp.transpose(nk, (0, 3, 1, 2)))
            vs_l.append(jnp.transpose(nv, (0, 3, 1, 2)))
        g_ffn = norm_ffn[i].reshape(1, D_MODEL)
        xm, xs = _ffn(xm, xs, g_ffn, w_gate, w_up, w_down, i, tm=1024, tf=512)

    return (xm.reshape(BATCH, SEQ, D_MODEL), xs[nm:].reshape(DEC_BATCH, 1, D_MODEL),
            jnp.stack(pool_p), jnp.stack(kp_l), jnp.stack(vp_l),
            jnp.stack(pool_s), jnp.stack(ks_l), jnp.stack(vs_l))
```
